```python
import math
import jax, jax.numpy as jnp
from jax import lax
import numpy as np

D_MODEL = 1024
BATCH = 8
SEQ = 8192
DEPTH = 2

CHUNK = 64
Q_BLOCK = 128
HEAD_DIM = 64
ROPE_THETA = 10000.0
EPS = 1e-6
A_HEADS = 8
A_LAT = 128
IDX_HEADS = 4
IDX_DIM = 64
TOPK_MAX = 256
B_HEADS = 4
B_VDIM = 2 * HEAD_DIM
C_GROUPS = 8
C_WIDTH = C_GROUPS * HEAD_DIM
SGU_BLOCK = 128
D_WIDTH = 512
CONV_W = 31
D_FF = 2816
PLE_DIM = 256

EVEN_SIZES = (A_HEADS * HEAD_DIM, HEAD_DIM, A_LAT, IDX_HEADS * IDX_DIM, IDX_DIM, IDX_HEADS,
              B_HEADS * 2 * HEAD_DIM, B_HEADS * 2 * HEAD_DIM, B_HEADS * B_VDIM)
EVEN_IN = sum(EVEN_SIZES)
EVEN_OUT = A_HEADS * HEAD_DIM + B_HEADS * B_VDIM
ODD_IN = 2 * C_WIDTH + 2 * D_WIDTH
ODD_OUT = C_WIDTH + D_WIDTH
N_EVEN = (DEPTH + 1) // 2
N_ODD = DEPTH // 2

kernel_name = "chunk_causal_hybrid_dsa_diff_sgu_conv"


def rms_norm(x, g):
    xf = x.astype(jnp.float32)
    y = xf * lax.rsqrt(jnp.mean(xf * xf, axis=-1, keepdims=True) + EPS)
    return (y * g.astype(jnp.float32)).astype(x.dtype)


def layer_norm(x, g, b):
    xf = x.astype(jnp.float32)
    mu = jnp.mean(xf, axis=-1, keepdims=True)
    var = jnp.mean(jnp.square(xf - mu), axis=-1, keepdims=True)
    y = (xf - mu) * lax.rsqrt(var + EPS)
    return (y * g.astype(jnp.float32) + b.astype(jnp.float32)).astype(x.dtype)


def swiglu(x, w_gate, w_up, w_down):
    return (jax.nn.silu(x @ w_gate) * (x @ w_up)) @ w_down


def rope_tables(pos):
    inv = ROPE_THETA ** (-jnp.arange(0, HEAD_DIM, 2, dtype=jnp.float32) / HEAD_DIM)
    ang = pos.astype(jnp.float32)[..., None] * inv
    return jnp.cos(ang), jnp.sin(ang)


def apply_rope(x, cos, sin):
    x1, x2 = jnp.split(x, 2, axis=-1)
    c = cos[:, :, None, :]
    s = sin[:, :, None, :]
    return jnp.concatenate([x1 * c - x2 * s, x2 * c + x1 * s], axis=-1).astype(x.dtype)


def split_cols(z, sizes):
    out, off = [], 0
    for n in sizes:
        out.append(z[..., off:off + n])
        off += n
    return out


def to_blocks(x):
    b, s = x.shape[:2]
    return jnp.moveaxis(x.reshape(b, s // Q_BLOCK, Q_BLOCK, *x.shape[2:]), 1, 0)


def from_blocks(y):
    nb, b, q = y.shape[:3]
    return jnp.moveaxis(y, 0, 1).reshape(b, nb * q, *y.shape[3:])


def dsa_attention(q, k, v_lat, iq, ik, iw, w_uv):
    s_len = q.shape[1]
    top_k = min(TOPK_MAX, s_len // 4)
    key_chunk = jnp.arange(s_len) // CHUNK
    scale = HEAD_DIM ** -0.5
    gather = jax.vmap(lambda a, i: a[i])

    def block(args):
        qb, iqb, iwb, j = args
        q_chunk = (j * Q_BLOCK + jnp.arange(Q_BLOCK)) // CHUNK
        mask = key_chunk[None, :] <= q_chunk[:, None]
        dots = jnp.einsum('bqhe,bse->bqhs', iqb, ik).astype(jnp.float32)
        score = jnp.einsum('bqh,bqhs->bqs', iwb.astype(jnp.float32), jax.nn.relu(dots))
        score = jnp.where(mask, score, -jnp.inf)
        _, sel = lax.top_k(score, top_k)
        valid = (sel // CHUNK) <= q_chunk[None, :, None]
        k_sel = gather(k, sel)
        v_sel = gather(v_lat, sel)
        logits = jnp.einsum('bqhd,bqkd->bhqk', qb, k_sel).astype(jnp.float32) * scale
        logits = jnp.where(valid[:, None], logits, -jnp.inf)
        prob = jax.nn.softmax(logits, axis=-1).astype(v_lat.dtype)
        o_lat = jnp.einsum('bhqk,bqkl->bqhl', prob, v_sel)
        return jnp.einsum('bqhl,hld->bqhd', o_lat, w_uv)

    nb = s_len // Q_BLOCK
    o = lax.map(block, (to_blocks(q), to_blocks(iq), to_blocks(iw), jnp.arange(nb)))
    return from_blocks(o)


def diff_attention(q, k, v, lam, subln_g, lam_init):
    s_len = q.shape[1]
    key_chunk = jnp.arange(s_len) // CHUNK
    scale = HEAD_DIM ** -0.5

    def block(args):
        qb, j = args
        q_chunk = (j * Q_BLOCK + jnp.arange(Q_BLOCK)) // CHUNK
        mask = key_chunk[None, :] <= q_chunk[:, None]
        s = jnp.einsum('bqhcd,bshcd->cbhqs', qb, k).astype(jnp.float32) * scale
        prob = jax.nn.softmax(jnp.where(mask, s, -jnp.inf), axis=-1)
        a = (prob[0] - lam * prob[1]).astype(v.dtype)
        return jnp.einsum('bhqs,bshe->bqhe', a, v)

    nb = s_len // Q_BLOCK
    o = from_blocks(lax.map(block, (to_blocks(q), jnp.arange(nb))))
    return rms_norm(o, subln_g) * (1.0 - lam_init)


def even_mixer(xn, cos, sin, w_in, w_out, a_q_g, a_k_g, a_w_uv, b_q_g, b_k_g,
               lq1, lk1, lq2, lk2, subln_g, lam_init):
    bsz, s_len = xn.shape[:2]
    aq, ak, av, iq, ik, iw, bq, bk, bv = split_cols(xn @ w_in, EVEN_SIZES)
    aq = apply_rope(rms_norm(aq.reshape(bsz, s_len, A_HEADS, HEAD_DIM), a_q_g), cos, sin)
    ak = apply_rope(rms_norm(ak, a_k_g)[:, :, None], cos, sin)[:, :, 0]
    iq = apply_rope(iq.reshape(bsz, s_len, IDX_HEADS, IDX_DIM), cos, sin)
    ik = apply_rope(ik[:, :, None], cos, sin)[:, :, 0]
    a_out = dsa_attention(aq, ak, av, iq, ik, iw, a_w_uv).reshape(bsz, s_len, A_HEADS * HEAD_DIM)
    bq = rms_norm(bq.reshape(bsz, s_len, B_HEADS * 2, HEAD_DIM), b_q_g)
    bk = rms_norm(bk.reshape(bsz, s_len, B_HEADS * 2, HEAD_DIM), b_k_g)
    bq = apply_rope(bq, cos, sin).reshape(bsz, s_len, B_HEADS, 2, HEAD_DIM)
    bk = apply_rope(bk, cos, sin).reshape(bsz, s_len, B_HEADS, 2, HEAD_DIM)
    bv = bv.reshape(bsz, s_len, B_HEADS, B_VDIM)
    lam = (jnp.exp(jnp.sum(lq1.astype(jnp.float32) * lk1.astype(jnp.float32)))
           - jnp.exp(jnp.sum(lq2.astype(jnp.float32) * lk2.astype(jnp.float32))) + lam_init)
    b_out = diff_attention(bq, bk, bv, lam, subln_g, lam_init).reshape(bsz, s_len, B_HEADS * B_VDIM)
    return jnp.concatenate([a_out, b_out], axis=-1) @ w_out


def odd_mixer(xn, w_in, w_out, c_ln_g, c_ln_b, c_w_s, c_b_s, d_conv_w, d_conv_b, d_ln_g, d_ln_b):
    bsz, s_len = xn.shape[:2]
    zc, zd = split_cols(xn @ w_in, (2 * C_WIDTH, 2 * D_WIDTH))
    u, v = jnp.split(jax.nn.gelu(zc), 2, axis=-1)
    v = layer_norm(v, c_ln_g, c_ln_b)
    vb = v.reshape(bsz, s_len // SGU_BLOCK, SGU_BLOCK, C_GROUPS, C_WIDTH // C_GROUPS)
    pc = jnp.arange(SGU_BLOCK) // CHUNK
    w_s = jnp.where((pc[:, None] >= pc[None, :])[None], c_w_s, 0.0)
    sg = jnp.einsum('gij,bnjgc->bnigc', w_s, vb) + c_b_s.T[None, None, :, :, None]
    c_out = u * sg.reshape(bsz, s_len, C_WIDTH)
    a, g = jnp.split(zd, 2, axis=-1)
    hd = a * jax.nn.sigmoid(g)
    hd = lax.conv_general_dilated(hd, d_conv_w[:, None, :], window_strides=(1,),
                                  padding=[(CONV_W - 1, 0)],
                                  dimension_numbers=('NWC', 'WIO', 'NWC'),
                                  feature_group_count=D_WIDTH) + d_conv_b
    d_out = jax.nn.silu(layer_norm(hd, d_ln_g, d_ln_b))
    return jnp.concatenate([c_out, d_out], axis=-1) @ w_out


def setup_inputs(seed: int = 0) -> dict:
    key = jax.random.key(seed)
    ks = list(jax.random.split(key, 48))

    def nrm(shape, scale):
        return jax.random.normal(ks.pop(), shape, jnp.float32) * scale

    def gain(shape):
        return 1.0 + nrm(shape, 0.02)

    D = D_MODEL
    return {
        "x": nrm((BATCH, SEQ, D), 1.0),
        "p": nrm((DEPTH, BATCH, SEQ, PLE_DIM), 1.0),
        "pos": (jax.random.randint(ks.pop(), (BATCH, 1), 0, 4096, dtype=jnp.int32)
                + jnp.arange(SEQ, dtype=jnp.int32)[None, :]).astype(jnp.int32),
        "ffn1_g": gain((DEPTH, D)),
        "ffn1_wg": nrm((DEPTH, D, D_FF), D ** -0.5),
        "ffn1_wu": nrm((DEPTH, D, D_FF), D ** -0.5),
        "ffn1_wd": nrm((DEPTH, D_FF, D), D_FF ** -0.5),
        "mix_g": gain((DEPTH, D)),
        "ffn2_g": gain((DEPTH, D)),
        "ffn2_wg": nrm((DEPTH, D, D_FF), D ** -0.5),
        "ffn2_wu": nrm((DEPTH, D, D_FF), D ** -0.5),
        "ffn2_wd": nrm((DEPTH, D_FF, D), D_FF ** -0.5),
        "ple_g": gain((DEPTH, D)),
        "ple_wgate": nrm((DEPTH, D, D), D ** -0.5),
        "ple_wproj": nrm((DEPTH, PLE_DIM, D), PLE_DIM ** -0.5),
        "ev_w_in": nrm((N_EVEN, D, EVEN_IN), D ** -0.5),
        "ev_w_out": nrm((N_EVEN, EVEN_OUT, D), EVEN_OUT ** -0.5),
        "a_q_g": gain((N_EVEN, HEAD_DIM)),
        "a_k_g": gain((N_EVEN, HEAD_DIM)),
        "a_w_uv": nrm((N_EVEN, A_HEADS, A_LAT, HEAD_DIM), A_LAT ** -0.5),
        "b_q_g": gain((N_EVEN, HEAD_DIM)),
        "b_k_g": gain((N_EVEN, HEAD_DIM)),
        "b_lam_q1": nrm((N_EVEN, HEAD_DIM), 0.1),
        "b_lam_k1": nrm((N_EVEN, HEAD_DIM), 0.1),
        "b_lam_q2": nrm((N_EVEN, HEAD_DIM), 0.1),
        "b_lam_k2": nrm((N_EVEN, HEAD_DIM), 0.1),
        "b_subln_g": gain((N_EVEN, B_VDIM)),
        "od_w_in": nrm((N_ODD, D, ODD_IN), D ** -0.5),
        "od_w_out": nrm((N_ODD, ODD_OUT, D), ODD_OUT ** -0.5),
        "c_ln_g": gain((N_ODD, C_WIDTH)),
        "c_ln_b": nrm((N_ODD, C_WIDTH), 0.02),
        "c_w_s": nrm((N_ODD, C_GROUPS, SGU_BLOCK, SGU_BLOCK), SGU_BLOCK ** -0.5),
        "c_b_s": 1.0 + nrm((N_ODD, C_GROUPS, SGU_BLOCK), 0.1),
        "d_conv_w": nrm((N_ODD, CONV_W, D_WIDTH), CONV_W ** -0.5),
        "d_conv_b": nrm((N_ODD, D_WIDTH), 0.02),
        "d_ln_g": gain((N_ODD, D_WIDTH)),
        "d_ln_b": nrm((N_ODD, D_WIDTH), 0.02),
    }


def reference(x, p, pos, ffn1_g, ffn1_wg, ffn1_wu, ffn1_wd, mix_g, ffn2_g, ffn2_wg, ffn2_wu,
              ffn2_wd, ple_g, ple_wgate, ple_wproj, ev_w_in, ev_w_out, a_q_g, a_k_g, a_w_uv,
              b_q_g, b_k_g, b_lam_q1, b_lam_k1, b_lam_q2, b_lam_k2, b_subln_g, od_w_in,
              od_w_out, c_ln_g, c_ln_b, c_w_s, c_b_s, d_conv_w, d_conv_b, d_ln_g, d_ln_b):
    cos, sin = rope_tables(pos)
    h = x
    for layer in range(DEPTH):
        h = h + 0.5 * swiglu(rms_norm(h, ffn1_g[layer]), ffn1_wg[layer], ffn1_wu[layer], ffn1_wd[layer])
        xn = rms_norm(h, mix_g[layer])
        if layer % 2 == 0:
            e = layer // 2
            lam_init = 0.8 - 0.6 * math.exp(-0.3 * layer)
            mix = even_mixer(xn, cos, sin, ev_w_in[e], ev_w_out[e], a_q_g[e], a_k_g[e], a_w_uv[e],
                             b_q_g[e], b_k_g[e], b_lam_q1[e], b_lam_k1[e], b_lam_q2[e],
                             b_lam_k2[e], b_subln_g[e], lam_init)
        else:
            o = layer // 2
            mix = odd_mixer(xn, od_w_in[o], od_w_out[o], c_ln_g[o], c_ln_b[o], c_w_s[o], c_b_s[o],
                            d_conv_w[o], d_conv_b[o], d_ln_g[o], d_ln_b[o])
        h = h + mix
        h = h + 0.5 * swiglu(rms_norm(h, ffn2_g[layer]), ffn2_wg[layer], ffn2_wu[layer], ffn2_wd[layer])
        gate = jax.nn.sigmoid(rms_norm(h, ple_g[layer]) @ ple_wgate[layer])
        h = h + gate * (p[layer] @ ple_wproj[layer])
    return h
```

```python
import functools
import math

import jax
import jax.numpy as jnp
from jax import lax
from jax.experimental import pallas as pl
from jax.experimental.pallas import tpu as pltpu

F32 = jnp.float32
BF16 = jnp.bfloat16
I32 = jnp.int32

EPS = 1e-6
HEAD_DIM = 64
CHUNK = 64
CHUNK_SHIFT = 6
ROPE_THETA = 10000.0
A_HEADS = 8
A_LAT = 128
IDX_HEADS = 4
TOPK_MAX = 256
B_HEADS = 4
C_GROUPS = 8
SGU_BLOCK = 128
CONV_W = 31
CONV_HALO = 32

LANES = 128
INT_MIN = -(2 ** 31)
NEG_BIAS = -1e30
M_INIT = -1e29

FF_CHUNK = 256
TM_FFN = 512
TM_PROJ = 512
TM_ODD = 256
TQ_DSA = 128
TK_DSA = 512
T_DIFF = 512
TIE_BLOCK = 256

VMEM_LIMIT_BYTES = 56 * 1024 * 1024


def _params(*sem):
    return pltpu.CompilerParams(dimension_semantics=sem, vmem_limit_bytes=VMEM_LIMIT_BYTES)


def _const_spec(shape):
    zeros = (0,) * len(shape)
    return pl.BlockSpec(shape, lambda *_: zeros)


def _rms(x, g):
    ms = jnp.mean(x * x, axis=-1, keepdims=True)
    return x * lax.rsqrt(ms + EPS) * g


def _layer_norm(x, g, b):
    mu = jnp.mean(x, axis=-1, keepdims=True)
    xc = x - mu
    var = jnp.mean(xc * xc, axis=-1, keepdims=True)
    return xc * lax.rsqrt(var + EPS) * g + b


def _dot(a, b):
    return jnp.dot(a, b, preferred_element_type=F32)


def _dot_nt(a, b):
    return lax.dot_general(a, b, (((1,), (1,)), ((), ())), preferred_element_type=F32)


def _ffn_kernel(*refs, has_mix, has_ple, n_chunks):
    refs = list(refs)
    h_ref = refs.pop(0)
    if has_mix:
        a_ref, b_ref, woa_ref, wob_ref = refs[:4]
        refs = refs[4:]
    g_ref, wgu_ref, wd_ref = refs[:3]
    refs = refs[3:]
    if has_ple:
        p_ref, pg_ref, wgate_ref, wproj_ref = refs[:4]
        refs = refs[4:]
    (o_ref,) = refs

    h = h_ref[...]
    if has_mix:
        h = h + _dot(a_ref[...], woa_ref[...]) + _dot(b_ref[...], wob_ref[...])
    xn = _rms(h, g_ref[...]).astype(BF16)
    acc = jnp.zeros(h.shape, F32)
    for c in range(n_chunks):
        gu = _dot(xn, wgu_ref[:, c * 2 * FF_CHUNK:(c + 1) * 2 * FF_CHUNK])
        gt = gu[:, :FF_CHUNK]
        up = gu[:, FF_CHUNK:]
        act = (gt * jax.nn.sigmoid(gt) * up).astype(BF16)
        acc = acc + _dot(act, wd_ref[c * FF_CHUNK:(c + 1) * FF_CHUNK, :])
    h = h + 0.5 * acc
    if has_ple:
        xg = _rms(h, pg_ref[...]).astype(BF16)
        gate = jax.nn.sigmoid(_dot(xg, wgate_ref[...]))
        h = h + gate * _dot(p_ref[...].astype(BF16), wproj_ref[...])
    o_ref[...] = h


def _ffn_call(h, g, wgu, wd, mix=None, ple=None):
    n, d = h.shape
    tm = min(TM_FFN, n)
    d_ff = wd.shape[0]
    n_chunks = d_ff // FF_CHUNK
    tok = lambda i: (i, 0)
    args = [h]
    specs = [pl.BlockSpec((tm, d), tok)]
    if mix is not None:
        a, b, woa, wob = mix
        args += [a, b, woa, wob]
        specs += [pl.BlockSpec((tm, a.shape[1]), tok), pl.BlockSpec((tm, b.shape[1]), tok),
                  _const_spec(woa.shape), _const_spec(wob.shape)]
    args += [g, wgu, wd]
    specs += [_const_spec(g.shape), _const_spec(wgu.shape), _const_spec(wd.shape)]
    if ple is not None:
        p, pg, wgate, wproj = ple
        args += [p, pg, wgate, wproj]
        specs += [pl.BlockSpec((tm, p.shape[1]), tok), _const_spec(pg.shape),
                  _const_spec(wgate.shape), _const_spec(wproj.shape)]
    kern = functools.partial(_ffn_kernel, has_mix=mix is not None, has_ple=ple is not None,
                             n_chunks=n_chunks)
    return pl.pallas_call(
        kern,
        grid=(n // tm,),
        in_specs=specs,
        out_specs=pl.BlockSpec((tm, d), tok),
        out_shape=jax.ShapeDtypeStruct((n, d), F32),
        compiler_params=_params("parallel"),
        name="ffn",
    )(*args)


_NR_WIDTHS = (A_HEADS * HEAD_DIM, 2 * HEAD_DIM, B_HEADS * 2 * HEAD_DIM, B_HEADS * 2 * HEAD_DIM)
_R_WIDTHS = (IDX_HEADS * HEAD_DIM, 2 * HEAD_DIM)
_P_WIDTHS = (A_LAT, B_HEADS * 2 * HEAD_DIM, LANES)
_NR_W = sum(_NR_WIDTHS)
_R_W = sum(_R_WIDTHS)


def _even_proj_kernel(h_ref, pos_ref, g_ref, inv_ref, sgn_ref, wnr_ref, wr_ref, wp_ref,
                      gn_ref, gnr_ref, gmat_ref,
                      aq_ref, akd_ref, bq_ref, bk_ref, iq_ref, ikd_ref, av_ref, bv_ref, iw_ref):
    xn = _rms(h_ref[...], g_ref[...]).astype(BF16)
    ang = pos_ref[...].astype(F32) * inv_ref[...]
    cos = jnp.cos(ang)
    sin = jnp.sin(ang) * sgn_ref[...]
    gmat = gmat_ref[...]

    off = 0
    for width, out_ref in zip(_NR_WIDTHS, (aq_ref, akd_ref, bq_ref, bk_ref)):
        x = _dot(xn, wnr_ref[:, off:off + width])
        xr = _dot(xn, wnr_ref[:, _NR_W + off:_NR_W + off + width])
        for s in range(width // LANES):
            sl = slice(s * LANES, (s + 1) * LANES)
            gsl = slice(off + s * LANES, off + (s + 1) * LANES)
            xs = x[:, sl]
            ms = _dot((xs * xs).astype(BF16), gmat)
            y = (xs * (gn_ref[:, gsl] * cos) + xr[:, sl] * (gnr_ref[:, gsl] * sin)) * lax.rsqrt(ms + EPS)
            out_ref[:, sl] = y.astype(out_ref.dtype)
        off += width

    off = 0
    for width, out_ref in zip(_R_WIDTHS, (iq_ref, ikd_ref)):
        x = _dot(xn, wr_ref[:, off:off + width])
        xr = _dot(xn, wr_ref[:, _R_W + off:_R_W + off + width])
        for s in range(width // LANES):
            sl = slice(s * LANES, (s + 1) * LANES)
            out_ref[:, sl] = (x[:, sl] * cos + xr[:, sl] * sin).astype(out_ref.dtype)
        off += width

    off = 0
    for width, out_ref in zip(_P_WIDTHS, (av_ref, bv_ref, iw_ref)):
        out_ref[...] = _dot(xn, wp_ref[:, off:off + width]).astype(out_ref.dtype)
        off += width


def _swap_halves(w):
    lead = w.shape[:-1]
    w4 = w.reshape(*lead, -1, 2, HEAD_DIM // 2)
    return w4[..., ::-1, :].reshape(*lead, -1)


def _even_proj_call(h, pos, g, w_in, a_q_g, a_k_g, b_q_g, b_k_g):
    n, d = h.shape
    tm = min(TM_PROJ, n)
    hd = HEAD_DIM
    sizes = (A_HEADS * hd, hd, A_LAT, IDX_HEADS * hd, hd, IDX_HEADS,
             B_HEADS * 2 * hd, B_HEADS * 2 * hd, B_HEADS * 2 * hd)
    offs = [0]
    for s in sizes:
        offs.append(offs[-1] + s)
    w_aq, w_ak, w_av, w_iq, w_ik, w_iw, w_bq, w_bk, w_bv = (
        w_in[:, offs[i]:offs[i + 1]] for i in range(9))
    dup = lambda w: jnp.concatenate([w, w], axis=1)
    nr = jnp.concatenate([w_aq, dup(w_ak), w_bq, w_bk], axis=1)
    wnr = jnp.concatenate([nr, _swap_halves(nr)], axis=1).astype(BF16)
    r = jnp.concatenate([w_iq, dup(w_ik)], axis=1)
    wr = jnp.concatenate([r, _swap_halves(r)], axis=1).astype(BF16)
    wp = jnp.concatenate([w_av, w_bv, jnp.pad(w_iw, ((0, 0), (0, LANES - IDX_HEADS)))], axis=1).astype(BF16)

    scale = hd ** -0.5
    gn = jnp.concatenate([jnp.tile(a_q_g, A_HEADS) * scale, jnp.tile(a_k_g, 2),
                          jnp.tile(b_q_g, 2 * B_HEADS) * scale, jnp.tile(b_k_g, 2 * B_HEADS)])
    gnr = _swap_halves(gn)
    gn = gn.reshape(1, -1).astype(F32)
    gnr = gnr.reshape(1, -1).astype(F32)

    inv = ROPE_THETA ** (-jnp.arange(0, hd, 2, dtype=F32) / hd)
    inv128 = jnp.tile(inv, LANES // (hd // 2)).reshape(1, LANES)
    sgn = jnp.tile(jnp.concatenate([-jnp.ones(hd // 2, F32), jnp.ones(hd // 2, F32)]),
                   LANES // hd).reshape(1, LANES)
    head_of_lane = jnp.arange(LANES) // hd
    gmat = ((head_of_lane[:, None] == head_of_lane[None, :]).astype(F32) / hd).astype(BF16)

    tok = lambda i: (i, 0)
    out_widths = (A_HEADS * hd, 2 * hd, B_HEADS * 2 * hd, B_HEADS * 2 * hd,
                  IDX_HEADS * hd, 2 * hd, A_LAT, B_HEADS * 2 * hd, LANES)
    out_dtypes = (BF16,) * 8 + (F32,)
    consts = (g, inv128, sgn, wnr, wr, wp, gn, gnr, gmat)
    return pl.pallas_call(
        _even_proj_kernel,
        grid=(n // tm,),
        in_specs=[pl.BlockSpec((tm, d), tok), pl.BlockSpec((tm, 1), tok)]
                 + [_const_spec(c.shape) for c in consts],
        out_specs=[pl.BlockSpec((tm, w), tok) for w in out_widths],
        out_shape=[jax.ShapeDtypeStruct((n, w), dt) for w, dt in zip(out_widths, out_dtypes)],
        compiler_params=_params("parallel"),
        name="even_proj",
    )(h, pos.reshape(n, 1), *consts)


def _stack_heads(x, n_heads):
    lane = lax.broadcasted_iota(I32, (x.shape[0], LANES), 1)
    parts = []
    for hh in range(n_heads):
        slab = x[:, (hh // 2) * LANES:(hh // 2 + 1) * LANES]
        keep = (lane < HEAD_DIM) if hh % 2 == 0 else (lane >= HEAD_DIM)
        parts.append(jnp.where(keep, slab, jnp.zeros_like(slab)))
    return jnp.concatenate(parts, axis=0)


def _dsa_kernel(aq_ref, iq_ref, iw_ref, k_ref, v_ref, ik_ref, tri_ref, wuv_ref, o_ref,
                key_sc, bias_sc, *, top_k):
    tq = aq_ref.shape[0]
    tk = k_ref.shape[1]
    q0 = pl.program_id(1) * tq
    n_kb = (q0 + tq + tk - 1) // tk

    iq_all = _stack_heads(iq_ref[...], IDX_HEADS)
    iw = iw_ref[...]
    qchunk = (q0 + lax.broadcasted_iota(I32, (tq, tk), 0)) >> CHUNK_SHIFT
    kcol = lax.broadcasted_iota(I32, (tq, tk), 1)

    def score_body(kb, carry):
        dots = jnp.maximum(_dot_nt(iq_all, ik_ref[kb]), 0.0)
        sc = dots[0:tq] * iw[:, 0:1]
        for hh in range(1, IDX_HEADS):
            sc = sc + dots[hh * tq:(hh + 1) * tq] * iw[:, hh:hh + 1]
        bits = pltpu.bitcast(sc, I32)
        key = bits ^ ((bits >> 31) & 0x7FFFFFFF)
        key = jnp.where(bits == INT_MIN, 0, key)
        allowed = ((kb * tk + kcol) >> CHUNK_SHIFT) <= qchunk
        key_sc[kb] = jnp.where(allowed, key, INT_MIN)
        return carry

    lax.fori_loop(0, n_kb, score_body, 0)

    def count_rows(pred_fn):
        def body(kb, acc):
            kk = key_sc[kb]
            for c in range(tk // LANES):
                acc = acc + jnp.where(pred_fn(kk[:, c * LANES:(c + 1) * LANES]), 1.0, 0.0)
            return acc
        acc = lax.fori_loop(0, n_kb, body, jnp.zeros((tq, LANES), F32))
        return jnp.sum(acc, axis=1, keepdims=True)

    def bit_body(b, t):
        cand = t ^ lax.shift_left(jnp.int32(1), 31 - b)
        cnt = count_rows(lambda kk: kk >= cand)
        return jnp.where(cnt >= top_k, cand, t)

    t = lax.fori_loop(0, 32, bit_body, jnp.full((tq, LANES), INT_MIN, I32))
    n_gt = count_rows(lambda kk: kk > t)
    need = jnp.where(t[:, 0:1] == INT_MIN, 0.0, top_k - n_gt)

    tri = tri_ref[...]
    t_w = jnp.concatenate([t] * (TIE_BLOCK // LANES), axis=1)

    def bias_body(kb, carry):
        kk_all = key_sc[kb]
        for c in range(tk // TIE_BLOCK):
            kk = kk_all[:, c * TIE_BLOCK:(c + 1) * TIE_BLOCK]
            eq = kk == t_w
            rank = _dot(jnp.where(eq, 1.0, 0.0).astype(BF16), tri) + carry
            tie_bias = jnp.where(rank <= need, 0.0, NEG_BIAS)
            bias = jnp.where(kk > t_w, 0.0, jnp.where(eq, tie_bias, NEG_BIAS))
            bias_sc[kb, :, c * TIE_BLOCK:(c + 1) * TIE_BLOCK] = bias
            carry = rank[:, TIE_BLOCK - 1:TIE_BLOCK]
        return carry

    lax.fori_loop(0, n_kb, bias_body, jnp.zeros((tq, 1), F32))

    q_all = _stack_heads(aq_ref[...], A_HEADS)
    rows = A_HEADS * tq

    def att_body(kb, carry):
        m, l, acc = carry
        s = _dot_nt(q_all, k_ref[kb])
        s = (s.reshape(A_HEADS, tq, tk) + bias_sc[kb][None]).reshape(rows, tk)
        m_new = jnp.maximum(m, jnp.max(s, axis=1, keepdims=True))
        alpha = jnp.exp(m - m_new)
        p = jnp.exp(s - m_new)
        l = alpha * l + jnp.sum(p, axis=1, keepdims=True)
        acc = alpha * acc + _dot(p.astype(BF16), v_ref[kb])
        return m_new, l, acc

    init = (jnp.full((rows, 1), M_INIT, F32), jnp.zeros((rows, 1), F32),
            jnp.zeros((rows, A_LAT), F32))
    _, l, acc = lax.fori_loop(0, n_kb, att_body, init)
    o_lat = (acc / l).astype(BF16)
    o_all = jnp.concatenate([o_lat[hh * tq:(hh + 1) * tq] for hh in range(A_HEADS)], axis=1)
    o_ref[...] = _dot(o_all, wuv_ref[...]).astype(o_ref.dtype)


def _dsa_call(aq, iq, iw, akd, av, ikd, w_uv):
    b, s, _ = aq.shape
    tq = min(TQ_DSA, s)
    tk = min(TK_DSA, s)
    top_k = min(TOPK_MAX, s // 4)
    n_kb = s // tk
    blk = lambda x: x.reshape(b, n_kb, tk, x.shape[-1])
    tri = (jnp.arange(TIE_BLOCK)[:, None] <= jnp.arange(TIE_BLOCK)[None, :]).astype(BF16)
    eye = jnp.eye(A_HEADS, dtype=w_uv.dtype)
    wuv_bd = (w_uv[:, :, None, :] * eye[:, None, :, None]).reshape(A_HEADS * A_LAT, A_HEADS * HEAD_DIM)
    wuv_bd = wuv_bd.astype(BF16)

    qmap = lambda bi, i: (bi, i, 0)
    kvmap = lambda bi, i: (bi, 0, 0, 0)
    kv_spec = pl.BlockSpec((None, n_kb, tk, LANES), kvmap)
    return pl.pallas_call(
        functools.partial(_dsa_kernel, top_k=top_k),
        grid=(b, s // tq),
        in_specs=[pl.BlockSpec((None, tq, aq.shape[-1]), qmap),
                  pl.BlockSpec((None, tq, iq.shape[-1]), qmap),
                  pl.BlockSpec((None, tq, LANES), qmap),
                  kv_spec, kv_spec, kv_spec,
                  _const_spec(tri.shape), _const_spec(wuv_bd.shape)],
        out_specs=pl.BlockSpec((None, tq, A_HEADS * HEAD_DIM), qmap),
        out_shape=jax.ShapeDtypeStruct((b, s, A_HEADS * HEAD_DIM), BF16),
        scratch_shapes=[pltpu.VMEM((n_kb, tq, tk), I32), pltpu.VMEM((n_kb, tq, tk), F32)],
        compiler_params=_params("parallel", "arbitrary"),
        name="dsa",
    )(aq, iq, iw, blk(akd), blk(av), blk(ikd), tri, wuv_bd)


def _diff_kernel(lam_ref, q_ref, k_ref, v_ref, g_ref, o_ref, *, lam_init):
    t = q_ref.shape[0]
    i = pl.program_id(2)
    lam_v = lam_ref[...]
    lam = (jnp.exp(jnp.sum(lam_v[0:1] * lam_v[1:2], axis=1, keepdims=True))
           - jnp.exp(jnp.sum(lam_v[2:3] * lam_v[3:4], axis=1, keepdims=True)) + lam_init)

    q = q_ref[...]
    lane = lax.broadcasted_iota(I32, q.shape, 1)
    zero = jnp.zeros_like(q)
    q2 = jnp.concatenate([jnp.where(lane < HEAD_DIM, q, zero),
                          jnp.where(lane >= HEAD_DIM, q, zero)], axis=0)

    def step(j, carry, masked):
        m, l, acc = carry
        s = _dot_nt(q2, k_ref[j])
        if masked:
            qc = lax.broadcasted_iota(I32, (2 * t, t), 0)
            qc = jnp.where(qc >= t, qc - t, qc) >> CHUNK_SHIFT
            kc = lax.broadcasted_iota(I32, (2 * t, t), 1) >> CHUNK_SHIFT
            s = jnp.where(kc <= qc, s, NEG_BIAS)
        m_new = jnp.maximum(m, jnp.max(s, axis=1, keepdims=True))
        alpha = jnp.exp(m - m_new)
        p = jnp.exp(s - m_new)
        l = alpha * l + jnp.sum(p, axis=1, keepdims=True)
        acc = alpha * acc + _dot(p.astype(BF16), v_ref[j])
        return m_new, l, acc

    init = (jnp.full((2 * t, 1), M_INIT, F32), jnp.zeros((2 * t, 1), F32),
            jnp.zeros((2 * t, v_ref.shape[-1]), F32))
    carry = lax.fori_loop(0, i, functools.partial(step, masked=False), init)
    _, l, acc = step(i, carry, True)
    o = acc / l
    o = o[:t] - lam * o[t:]
    o_ref[...] = (_rms(o, g_ref[...]) * (1.0 - lam_init)).astype(o_ref.dtype)


def _diff_call(bq, bk, bv, lam_vecs, subln_g, lam_init):
    b, s, w = bq.shape
    t = min(T_DIFF, s)
    nb = s // t
    k4 = bk.reshape(b, nb, t, w)
    v4 = bv.reshape(b, nb, t, w)
    qmap = lambda bi, hh, i: (bi, i, hh)
    kvmap = lambda bi, hh, i: (bi, 0, 0, hh)
    return pl.pallas_call(
        functools.partial(_diff_kernel, lam_init=lam_init),
        grid=(b, B_HEADS, nb),
        in_specs=[_const_spec(lam_vecs.shape),
                  pl.BlockSpec((None, t, LANES), qmap),
                  pl.BlockSpec((None, nb, t, LANES), kvmap),
                  pl.BlockSpec((None, nb, t, LANES), kvmap),
                  _const_spec(subln_g.shape)],
        out_specs=pl.BlockSpec((None, t, LANES), qmap),
        out_shape=jax.ShapeDtypeStruct((b, s, w), BF16),
        compiler_params=_params("parallel", "parallel", "arbitrary"),
        name="diff_attn",
    )(lam_vecs, bq, k4, v4, subln_g)


def _odd_kernel(h_ref, g_ref, wc_ref, wd_ref, clg_ref, clb_ref, ws_ref, sb_ref,
                cw_ref, cb_ref, dlg_ref, dlb_ref, woc_ref, wod_ref, o_ref, buf_sc, *, tiles_per_seq):
    tm = h_ref.shape[0]
    cwid = clg_ref.shape[1]

    h = h_ref[...]
    xn = _rms(h, g_ref[...]).astype(BF16)

    zc = jax.nn.gelu(_dot(xn, wc_ref[...]), approximate=True)
    u = zc[:, :cwid]
    v = _layer_norm(zc[:, cwid:], clg_ref[...], clb_ref[...]).astype(BF16)
    pi = lax.broadcasted_iota(I32, (SGU_BLOCK, SGU_BLOCK), 0) >> CHUNK_SHIFT
    pj = lax.broadcasted_iota(I32, (SGU_BLOCK, SGU_BLOCK), 1) >> CHUNK_SHIFT
    causal = pi >= pj
    w_s = [jnp.where(causal, ws_ref[gi], jnp.zeros((SGU_BLOCK, SGU_BLOCK), BF16)) for gi in range(C_GROUPS)]
    lane = lax.broadcasted_iota(I32, (SGU_BLOCK, LANES), 1)
    sgb = sb_ref[...]
    rows_out = []
    for r in range(tm // SGU_BLOCK):
        rs = slice(r * SGU_BLOCK, (r + 1) * SGU_BLOCK)
        slabs = []
        for pr in range(cwid // LANES):
            vs = v[rs, pr * LANES:(pr + 1) * LANES]
            lo = _dot(w_s[2 * pr], vs)
            hi = _dot(w_s[2 * pr + 1], vs)
            slabs.append(jnp.where(lane < HEAD_DIM, lo, hi))
        rows_out.append(jnp.concatenate(slabs, axis=1) + sgb)
    c_out = (u * jnp.concatenate(rows_out, axis=0)).astype(BF16)

    zd = _dot(xn, wd_ref[...])
    dwid = zd.shape[1] // 2
    hd = zd[:, :dwid] * jax.nn.sigmoid(zd[:, dwid:])

    @pl.when(pl.program_id(0) % tiles_per_seq == 0)
    def _():
        buf_sc[0:CONV_HALO, :] = jnp.zeros((CONV_HALO, dwid), F32)

    buf_sc[CONV_HALO:CONV_HALO + tm, :] = hd
    conv = jnp.zeros((tm, dwid), F32) + cb_ref[...]
    base = CONV_HALO - (CONV_W - 1)
    for w in range(CONV_W):
        conv = conv + buf_sc[base + w:base + w + tm, :] * cw_ref[w:w + 1, :]
    buf_sc[0:CONV_HALO, :] = buf_sc[tm:tm + CONV_HALO, :]
    d_out = _layer_norm(conv, dlg_ref[...], dlb_ref[...])
    d_out = (d_out * jax.nn.sigmoid(d_out)).astype(BF16)

    o_ref[...] = h + _dot(c_out, woc_ref[...]) + _dot(d_out, wod_ref[...])


def _odd_call(h, seq_len, g, w_in, w_out, c_ln_g, c_ln_b, c_w_s, c_b_s, d_conv_w, d_conv_b, d_ln_g, d_ln_b):
    n, d = h.shape
    tm = min(TM_ODD, seq_len)
    cwid = c_ln_g.shape[0]
    dwid = d_ln_g.shape[0]
    row = lambda x: x.reshape(1, -1).astype(F32)
    wc = w_in[:, :2 * cwid].astype(BF16)
    wd = w_in[:, 2 * cwid:].astype(BF16)
    sgb = jnp.repeat(c_b_s.T, cwid // C_GROUPS, axis=1).astype(F32)
    consts = (row(g), wc, wd, row(c_ln_g), row(c_ln_b), c_w_s.astype(BF16), sgb,
              d_conv_w.astype(F32), row(d_conv_b), row(d_ln_g), row(d_ln_b),
              w_out[:cwid].astype(BF16), w_out[cwid:].astype(BF16))
    tok = lambda i: (i, 0)
    return pl.pallas_call(
        functools.partial(_odd_kernel, tiles_per_seq=seq_len // tm),
        grid=(n // tm,),
        in_specs=[pl.BlockSpec((tm, d), tok)] + [_const_spec(c.shape) for c in consts],
        out_specs=pl.BlockSpec((tm, d), tok),
        out_shape=jax.ShapeDtypeStruct((n, d), F32),
        scratch_shapes=[pltpu.VMEM((tm + CONV_HALO, dwid), F32)],
        compiler_params=_params("arbitrary"),
        name="odd_mixer",
    )(h, *consts)


def _ffn_weights(wg, wu, wd):
    d, d_ff = wg.shape
    nc = d_ff // FF_CHUNK
    wgu = jnp.concatenate([wg.reshape(d, nc, 1, FF_CHUNK), wu.reshape(d, nc, 1, FF_CHUNK)], axis=2)
    return wgu.reshape(d, 2 * d_ff).astype(BF16), wd.astype(BF16)


def kernel(x, p, pos, ffn1_g, ffn1_wg, ffn1_wu, ffn1_wd, mix_g, ffn2_g, ffn2_wg, ffn2_wu, ffn2_wd, ple_g, ple_wgate, ple_wproj, ev_w_in, ev_w_out, a_q_g, a_k_g, a_w_uv, b_q_g, b_k_g, b_lam_q1, b_lam_k1, b_lam_q2, b_lam_k2, b_subln_g, od_w_in, od_w_out, c_ln_g, c_ln_b, c_w_s, c_b_s, d_conv_w, d_conv_b, d_ln_g, d_ln_b):
    bsz, s_len, d = x.shape
    n = bsz * s_len
    depth = p.shape[0]
    row = lambda v: v.reshape(1, -1).astype(F32)
    h = x.reshape(n, d)
    for layer in range(depth):
        wgu1, wd1 = _ffn_weights(ffn1_wg[layer], ffn1_wu[layer], ffn1_wd[layer])
        wgu2, wd2 = _ffn_weights(ffn2_wg[layer], ffn2_wu[layer], ffn2_wd[layer])
        ple = (p[layer].reshape(n, -1), row(ple_g[layer]), ple_wgate[layer].astype(BF16),
               ple_wproj[layer].astype(BF16))
        h = _ffn_call(h, row(ffn1_g[layer]), wgu1, wd1)
        if layer % 2 == 0:
            e = layer // 2
            lam_init = 0.8 - 0.6 * math.exp(-0.3 * layer)
            aq, akd, bq, bk, iq, ikd, av, bv, iw = _even_proj_call(
                h, pos, row(mix_g[layer]), ev_w_in[e], a_q_g[e], a_k_g[e], b_q_g[e], b_k_g[e])
            r3 = lambda a: a.reshape(bsz, s_len, a.shape[-1])
            a_out = _dsa_call(r3(aq), r3(iq), r3(iw), r3(akd), r3(av), r3(ikd), a_w_uv[e])
            lam_vecs = jnp.stack([b_lam_q1[e], b_lam_k1[e], b_lam_q2[e], b_lam_k2[e]]).astype(F32)
            b_out = _diff_call(r3(bq), r3(bk), r3(bv), lam_vecs, row(b_subln_g[e]), lam_init)
            a_w = A_HEADS * HEAD_DIM
            mix = (a_out.reshape(n, -1), b_out.reshape(n, -1),
                   ev_w_out[e][:a_w].astype(BF16), ev_w_out[e][a_w:].astype(BF16))
            h = _ffn_call(h, row(ffn2_g[layer]), wgu2, wd2, mix=mix, ple=ple)
        else:
            o = layer // 2
            h = _odd_call(h, s_len, mix_g[layer], od_w_in[o], od_w_out[o], c_ln_g[o], c_ln_b[o],
                          c_w_s[o], c_b_s[o], d_conv_w[o], d_conv_b[o], d_ln_g[o], d_ln_b[o])
            h = _ffn_call(h, row(ffn2_g[layer]), wgu2, wd2, ple=ple)
    return h.reshape(bsz, s_len, d)
```

```python
import functools
import math

import jax
import jax.numpy as jnp
from jax import lax
from jax.experimental import pallas as pl
from jax.experimental.pallas import tpu as pltpu

F32 = jnp.float32
BF16 = jnp.bfloat16
I32 = jnp.int32

EPS = 1e-6
HEAD_DIM = 64
CHUNK = 64
CHUNK_SHIFT = 6
ROPE_THETA = 10000.0
A_HEADS = 8
A_LAT = 128
IDX_HEADS = 4
TOPK_MAX = 256
B_HEADS = 4
C_GROUPS = 8
SGU_BLOCK = 128
CONV_W = 31
CONV_HALO = 32

LANES = 128
INT_MIN = -(2 ** 31)
NEG_BIAS = -1e30
M_INIT = -1e29
MAX_SHIFT = 60.0
BF16_SLACK = 1.02

FF_CHUNK = 256
TM_FFN = 512
TM_PROJ = 512
TM_ODD = 256
TQ_DSA = 128
TK_DSA = 512
T_DIFF = 512
TIE_BLOCK = 256
COUNT_ROWS = 64
ATT_UNROLL = 4

VMEM_LIMIT_BYTES = 56 * 1024 * 1024


def _params(*sem):
    return pltpu.CompilerParams(dimension_semantics=sem, vmem_limit_bytes=VMEM_LIMIT_BYTES)


def _const_spec(shape):
    zeros = (0,) * len(shape)
    return pl.BlockSpec(shape, lambda *_: zeros)


def _rms(x, g):
    ms = jnp.mean(x * x, axis=-1, keepdims=True)
    return x * lax.rsqrt(ms + EPS) * g


def _layer_norm(x, g, b):
    mu = jnp.mean(x, axis=-1, keepdims=True)
    xc = x - mu
    var = jnp.mean(xc * xc, axis=-1, keepdims=True)
    return xc * lax.rsqrt(var + EPS) * g + b


def _unrolled_loop(n, step, carry):
    done = 0
    unroll = ATT_UNROLL
    while unroll >= 1:
        def group(g, c, unroll=unroll, done=done):
            for u in range(unroll):
                c = step(done + g * unroll + u, c)
            return c

        n_groups = lax.shift_right_logical(n - done, unroll.bit_length() - 1)
        carry = lax.fori_loop(0, n_groups, group, carry)
        done = done + n_groups * unroll
        unroll //= 2
    return carry


def _dot(a, b):
    return jnp.dot(a, b, preferred_element_type=F32)


def _dot_nt(a, b):
    return lax.dot_general(a, b, (((1,), (1,)), ((), ())), preferred_element_type=F32)


def _ffn_kernel(*refs, has_mix, has_ple, n_chunks):
    refs = list(refs)
    h_ref = refs.pop(0)
    if has_mix:
        a_ref, b_ref, woa_ref, wob_ref = refs[:4]
        refs = refs[4:]
    g_ref, wg_ref, wu_ref, wd_ref = refs[:4]
    refs = refs[4:]
    if has_ple:
        p_ref, pg_ref, wgate_ref, wproj_ref = refs[:4]
        refs = refs[4:]
    (o_ref,) = refs

    h = h_ref[...]
    if has_mix:
        h = h + _dot(a_ref[...], woa_ref[...]) + _dot(b_ref[...], wob_ref[...])
    xn = _rms(h, g_ref[...]).astype(BF16)
    acc = jnp.zeros(h.shape, F32)
    for c in range(n_chunks):
        gt = _dot(xn, wg_ref[:, c * FF_CHUNK:(c + 1) * FF_CHUNK])
        up = _dot(xn, wu_ref[:, c * FF_CHUNK:(c + 1) * FF_CHUNK])
        act = (gt * jax.nn.sigmoid(gt) * up).astype(BF16)
        acc = acc + _dot(act, wd_ref[c * FF_CHUNK:(c + 1) * FF_CHUNK, :])
    h = h + 0.5 * acc
    if has_ple:
        xg = _rms(h, pg_ref[...]).astype(BF16)
        gate = jax.nn.sigmoid(_dot(xg, wgate_ref[...]))
        h = h + gate * _dot(p_ref[...].astype(BF16), wproj_ref[...])
    o_ref[...] = h


def _ffn_call(h, g, wg, wu, wd, mix=None, ple=None):
    n, d = h.shape
    tm = min(TM_FFN, n)
    d_ff = wd.shape[0]
    n_chunks = d_ff // FF_CHUNK
    tok = lambda i: (i, 0)
    args = [h]
    specs = [pl.BlockSpec((tm, d), tok)]
    if mix is not None:
        a, b, woa, wob = mix
        args += [a, b, woa, wob]
        specs += [pl.BlockSpec((tm, a.shape[1]), tok), pl.BlockSpec((tm, b.shape[1]), tok),
                  _const_spec(woa.shape), _const_spec(wob.shape)]
    args += [g, wg, wu, wd]
    specs += [_const_spec(g.shape), _const_spec(wg.shape), _const_spec(wu.shape), _const_spec(wd.shape)]
    if ple is not None:
        p, pg, wgate, wproj = ple
        args += [p, pg, wgate, wproj]
        specs += [pl.BlockSpec((tm, p.shape[1]), tok), _const_spec(pg.shape),
                  _const_spec(wgate.shape), _const_spec(wproj.shape)]
    kern = functools.partial(_ffn_kernel, has_mix=mix is not None, has_ple=ple is not None,
                             n_chunks=n_chunks)
    return pl.pallas_call(
        kern,
        grid=(n // tm,),
        in_specs=specs,
        out_specs=pl.BlockSpec((tm, d), tok),
        out_shape=jax.ShapeDtypeStruct((n, d), F32),
        compiler_params=_params("parallel"),
        name="ffn",
    )(*args)


_NR_WIDTHS = (A_HEADS * HEAD_DIM, 2 * HEAD_DIM, B_HEADS * 2 * HEAD_DIM, B_HEADS * 2 * HEAD_DIM)
_R_WIDTHS = (IDX_HEADS * HEAD_DIM, 2 * HEAD_DIM)
_P_WIDTHS = (A_LAT, B_HEADS * 2 * HEAD_DIM)
IW_ROWS = 8
_NR_W = sum(_NR_WIDTHS)
_R_W = sum(_R_WIDTHS)


def _even_proj_kernel(h_ref, pos_ref, g_ref, inv_ref, sgn_ref, wnr_ref, wr_ref, wp_ref, wiwt_ref,
                      gn_ref, gnr_ref, gmat_ref,
                      aq_ref, akd_ref, bq_ref, bk_ref, iq_ref, ikd_ref, av_ref, bv_ref, iwt_ref):
    xn = _rms(h_ref[...], g_ref[...]).astype(BF16)
    ang = pos_ref[...].astype(F32) * inv_ref[...]
    cos = jnp.cos(ang)
    sin = jnp.sin(ang) * sgn_ref[...]
    gmat = gmat_ref[...]

    off = 0
    for width, out_ref in zip(_NR_WIDTHS, (aq_ref, akd_ref, bq_ref, bk_ref)):
        x = _dot(xn, wnr_ref[:, off:off + width])
        xr = _dot(xn, wnr_ref[:, _NR_W + off:_NR_W + off + width])
        for s in range(width // LANES):
            sl = slice(s * LANES, (s + 1) * LANES)
            gsl = slice(off + s * LANES, off + (s + 1) * LANES)
            xs = x[:, sl]
            ms = _dot((xs * xs).astype(BF16), gmat)
            y = (xs * (gn_ref[:, gsl] * cos) + xr[:, sl] * (gnr_ref[:, gsl] * sin)) * lax.rsqrt(ms + EPS)
            out_ref[:, sl] = y.astype(out_ref.dtype)
        off += width

    off = 0
    for width, out_ref in zip(_R_WIDTHS, (iq_ref, ikd_ref)):
        x = _dot(xn, wr_ref[:, off:off + width])
        xr = _dot(xn, wr_ref[:, _R_W + off:_R_W + off + width])
        for s in range(width // LANES):
            sl = slice(s * LANES, (s + 1) * LANES)
            out_ref[:, sl] = (x[:, sl] * cos + xr[:, sl] * sin).astype(out_ref.dtype)
        off += width

    off = 0
    for width, out_ref in zip(_P_WIDTHS, (av_ref, bv_ref)):
        out_ref[...] = _dot(xn, wp_ref[:, off:off + width]).astype(out_ref.dtype)
        off += width
    iwt_ref[...] = _dot_nt(wiwt_ref[...], xn)


def _swap_halves(w):
    lead = w.shape[:-1]
    w4 = w.reshape(*lead, -1, 2, HEAD_DIM // 2)
    return w4[..., ::-1, :].reshape(*lead, -1)


def _even_proj_call(h, pos, g, w_in, a_q_g, a_k_g, b_q_g, b_k_g):
    n, d = h.shape
    tm = min(TM_PROJ, n)
    hd = HEAD_DIM
    sizes = (A_HEADS * hd, hd, A_LAT, IDX_HEADS * hd, hd, IDX_HEADS,
             B_HEADS * 2 * hd, B_HEADS * 2 * hd, B_HEADS * 2 * hd)
    offs = [0]
    for s in sizes:
        offs.append(offs[-1] + s)
    w_aq, w_ak, w_av, w_iq, w_ik, w_iw, w_bq, w_bk, w_bv = (
        w_in[:, offs[i]:offs[i + 1]] for i in range(9))
    dup = lambda w: jnp.concatenate([w, w], axis=1)
    nr = jnp.concatenate([w_aq, dup(w_ak), w_bq, w_bk], axis=1)
    wnr = jnp.concatenate([nr, _swap_halves(nr)], axis=1).astype(BF16)
    r = jnp.concatenate([w_iq, dup(w_ik)], axis=1)
    wr = jnp.concatenate([r, _swap_halves(r)], axis=1).astype(BF16)
    wp = jnp.concatenate([w_av, w_bv], axis=1).astype(BF16)
    wiwt = jnp.pad(w_iw.T, ((0, IW_ROWS - IDX_HEADS), (0, 0))).astype(BF16)

    scale = hd ** -0.5 * math.log2(math.e)
    gn =jnp.concatenate([jnp.tile(a_q_g, A_HEADS) * scale, jnp.tile(a_k_g, 2),
                          jnp.tile(b_q_g, 2 * B_HEADS) * scale, jnp.tile(b_k_g, 2 * B_HEADS)])
    gnr = _swap_halves(gn)
    gn = gn.reshape(1, -1).astype(F32)
    gnr = gnr.reshape(1, -1).astype(F32)

    inv = ROPE_THETA ** (-jnp.arange(0, hd, 2, dtype=F32) / hd)
    inv128 = jnp.tile(inv, LANES // (hd // 2)).reshape(1, LANES)
    sgn = jnp.tile(jnp.concatenate([-jnp.ones(hd // 2, F32), jnp.ones(hd // 2, F32)]),
                   LANES // hd).reshape(1, LANES)
    head_of_lane = jnp.arange(LANES) // hd
    gmat = ((head_of_lane[:, None] == head_of_lane[None, :]).astype(F32) / hd).astype(BF16)

    tok = lambda i: (i, 0)
    out_widths = (A_HEADS * hd, 2 * hd, B_HEADS * 2 * hd, B_HEADS * 2 * hd,
                  IDX_HEADS * hd, 2 * hd, A_LAT, B_HEADS * 2 * hd)
    consts = (g, inv128, sgn, wnr, wr, wp, wiwt, gn, gnr, gmat)
    return pl.pallas_call(
        _even_proj_kernel,
        grid=(n // tm,),
        in_specs=[pl.BlockSpec((tm, d), tok), pl.BlockSpec((tm, 1), tok)]
                 + [_const_spec(c.shape) for c in consts],
        out_specs=[pl.BlockSpec((tm, w), tok) for w in out_widths]
                  + [pl.BlockSpec((IW_ROWS, tm), lambda i: (0, i))],
        out_shape=[jax.ShapeDtypeStruct((n, w), BF16) for w in out_widths]
                  + [jax.ShapeDtypeStruct((IW_ROWS, n), F32)],
        compiler_params=_params("parallel"),
        name="even_proj",
    )(h, pos.reshape(n, 1), *consts)


def _stack_heads(x, n_heads):
    lane = lax.broadcasted_iota(I32, (x.shape[0], LANES), 1)
    parts = []
    for hh in range(n_heads):
        slab = x[:, (hh // 2) * LANES:(hh // 2 + 1) * LANES]
        keep = (lane < HEAD_DIM) if hh % 2 == 0 else (lane >= HEAD_DIM)
        parts.append(jnp.where(keep, slab, jnp.zeros_like(slab)))
    return jnp.concatenate(parts, axis=0)


def _dsa_kernel(shift_ref, aq_ref, iq_ref, iwt_ref, k_ref, v_ref, ik_ref, tri_ref, wuv_ref, o_ref,
                key_sc, bias_sc, *, top_k, bounded):
    tq = aq_ref.shape[0]
    tk = k_ref.shape[1]
    q0 = pl.program_id(1) * tq
    n_kb = (q0 + tq + tk - 1) // tk

    iq_all = _stack_heads(iq_ref[...], IDX_HEADS)
    iwt = iwt_ref[...]
    qchunk = (q0 + lax.broadcasted_iota(I32, (tk, tq), 1)) >> CHUNK_SHIFT
    krow = lax.broadcasted_iota(I32, (tk, tq), 0)

    def score_body(kb, carry):
        dots = jnp.maximum(_dot_nt(ik_ref[kb], iq_all), 0.0)
        sc = dots[:, 0:tq] * iwt[0:1, :]
        for hh in range(1, IDX_HEADS):
            sc = sc + dots[:, hh * tq:(hh + 1) * tq] * iwt[hh:hh + 1, :]
        bits = pltpu.bitcast(sc, I32)
        key = bits ^ ((bits >> 31) & 0x7FFFFFFF)
        key = jnp.where(bits == INT_MIN, 0, key)
        allowed = ((kb * tk + krow) >> CHUNK_SHIFT) <= qchunk
        key_sc[kb] = jnp.where(allowed, key, INT_MIN)
        return carry

    lax.fori_loop(0, n_kb, score_body, 0)

    def count_keys(pred_fn):
        def body(kb, acc):
            hit = jnp.where(pred_fn(key_sc[kb]), 1.0, 0.0)
            return acc + jnp.sum(hit.reshape(tk // COUNT_ROWS, COUNT_ROWS, tq), axis=0)
        acc = lax.fori_loop(0, n_kb, body, jnp.zeros((COUNT_ROWS, tq), F32))
        return jnp.sum(acc, axis=0, keepdims=True)

    def bit_body(b, t):
        cand = t ^ lax.shift_left(jnp.int32(1), 31 - b)
        cnt = count_keys(lambda kk: kk >= cand)
        return jnp.where(cnt >= top_k, cand, t)

    t = lax.fori_loop(0, 32, bit_body, jnp.full((1, tq), INT_MIN, I32))
    n_gt = count_keys(lambda kk: kk > t)
    need = jnp.where(t == INT_MIN, 0.0, top_k - n_gt)

    tri = tri_ref[...]
    sel_bias = -shift_ref[0]

    def bias_body(kb, carry):
        kk_all = key_sc[kb]
        for c in range(tk // TIE_BLOCK):
            kk = kk_all[c * TIE_BLOCK:(c + 1) * TIE_BLOCK]
            eq = kk == t
            rank = _dot(tri, jnp.where(eq, 1.0, 0.0).astype(BF16)) + carry
            tie_bias = jnp.where(rank <= need, sel_bias, NEG_BIAS)
            bias = jnp.where(kk > t, sel_bias, jnp.where(eq, tie_bias, NEG_BIAS))
            bias_sc[kb, c * TIE_BLOCK:(c + 1) * TIE_BLOCK, :] = bias.astype(BF16)
            carry = rank[TIE_BLOCK - 1:TIE_BLOCK, :]
        return carry

    lax.fori_loop(0, n_kb, bias_body, jnp.zeros((1, tq), F32))

    rows = A_HEADS * tq
    eye = lax.broadcasted_iota(I32, (tq, tq), 0) == lax.broadcasted_iota(I32, (tq, tq), 1)
    eye = jnp.where(eye, 1.0, 0.0).astype(BF16)
    qe = jnp.concatenate([_stack_heads(aq_ref[...], A_HEADS),
                          jnp.concatenate([eye] * A_HEADS, axis=0)], axis=1)
    ones = jnp.ones((tk, LANES), BF16)

    def logits(kb):
        return _dot_nt(qe, jnp.concatenate([k_ref[kb], bias_sc[kb]], axis=1))

    def values(kb):
        return jnp.concatenate([v_ref[kb], ones], axis=1)

    acc0 = jnp.zeros((rows, A_LAT + LANES), F32)
    if bounded:
        def att_step(kb, acc):
            return acc + _dot(jnp.exp2(logits(kb)).astype(BF16), values(kb))

        acc = _unrolled_loop(n_kb, att_step, acc0)
    else:
        def softmax_step(kb, m, acc, s):
            m_new = jnp.maximum(m, jnp.max(s, axis=1, keepdims=True))
            p = jnp.exp2(s - m_new).astype(BF16)
            return m_new, jnp.exp2(m - m_new) * acc + _dot(p, values(kb))

        def att_body(kb, carry):
            m, acc, s = carry
            s_next = logits(kb + 1)
            m, acc = softmax_step(kb, m, acc, s)
            return m, acc, s_next

        m, acc, s = lax.fori_loop(0, n_kb - 1, att_body, (jnp.full((rows, 1), M_INIT, F32), acc0, logits(0)))
        _, acc = softmax_step(n_kb - 1, m, acc, s)
    o_lat = (acc[:, :A_LAT] / acc[:, A_LAT:]).astype(BF16)
    o_all = jnp.concatenate([o_lat[hh * tq:(hh + 1) * tq] for hh in range(A_HEADS)], axis=1)
    o_ref[...] = _dot(o_all, wuv_ref[...]).astype(o_ref.dtype)


def _dsa_call(shift, aq, iq, iwt, akd, av, ikd, w_uv, *, bounded):
    b, s, _ = aq.shape
    tq = min(TQ_DSA, s)
    tk = min(TK_DSA, s)
    assert tq == LANES, "the one-hot mask columns pair one query tile with 128 contraction lanes"
    top_k = min(TOPK_MAX, s // 4)
    n_kb = s // tk
    nq = s // tq
    blk = lambda x: x.reshape(b, n_kb, tk, x.shape[-1])
    tri = (jnp.arange(TIE_BLOCK)[:, None] >= jnp.arange(TIE_BLOCK)[None, :]).astype(BF16)
    eye = jnp.eye(A_HEADS, dtype=w_uv.dtype)
    wuv_bd = (w_uv[:, :, None, :] * eye[:, None, :, None]).reshape(A_HEADS * A_LAT, A_HEADS * HEAD_DIM)
    wuv_bd = wuv_bd.astype(BF16)

    qmap = lambda bi, i: (bi, i, 0)
    kvmap = lambda bi, i: (bi, 0, 0, 0)
    kv_spec = pl.BlockSpec((None, n_kb, tk, LANES), kvmap)
    return pl.pallas_call(
        functools.partial(_dsa_kernel, top_k=top_k, bounded=bounded),
        grid=(b, nq),
        in_specs=[pl.BlockSpec(memory_space=pltpu.SMEM),
                  pl.BlockSpec((None, tq, aq.shape[-1]), qmap),
                  pl.BlockSpec((None, tq, iq.shape[-1]), qmap),
                  pl.BlockSpec((iwt.shape[0], tq), lambda bi, i: (0, bi * nq + i)),
                  kv_spec, kv_spec, kv_spec,
                  _const_spec(tri.shape), _const_spec(wuv_bd.shape)],
        out_specs=pl.BlockSpec((None, tq, A_HEADS * HEAD_DIM), qmap),
        out_shape=jax.ShapeDtypeStruct((b, s, A_HEADS * HEAD_DIM), BF16),
        scratch_shapes=[pltpu.VMEM((n_kb, tk, tq), I32), pltpu.VMEM((n_kb, tk, tq), BF16)],
        compiler_params=_params("parallel", "arbitrary"),
        name="dsa",
    )(shift, aq, iq, iwt, blk(akd), blk(av), blk(ikd), tri, wuv_bd)


def _diff_kernel(shift_ref, lam_ref, q_ref, k_ref, v_ref, g_ref, o_ref, *, lam_init, bounded):
    t = q_ref.shape[0]
    i = pl.program_id(2)
    lam_v = lam_ref[...]
    lam = (jnp.exp(jnp.sum(lam_v[0:1] * lam_v[1:2], axis=1, keepdims=True))
           - jnp.exp(jnp.sum(lam_v[2:3] * lam_v[3:4], axis=1, keepdims=True)) + lam_init)

    q = q_ref[...]
    lane = lax.broadcasted_iota(I32, q.shape, 1)
    zero = jnp.zeros_like(q)
    q2 = jnp.concatenate([jnp.where(lane < HEAD_DIM, q, zero),
                          jnp.where(lane >= HEAD_DIM, q, zero)], axis=0)

    vdim = v_ref.shape[-1]
    ones = jnp.ones((t, LANES), BF16)

    def values(j):
        return jnp.concatenate([v_ref[j], ones], axis=1)

    acc0 = jnp.zeros((2 * t, vdim + LANES), F32)

    if bounded:
        assert t // CHUNK < LANES
        q_lane = lax.broadcasted_iota(I32, (2 * t, LANES), 1)
        q_chunk = lax.broadcasted_iota(I32, (2 * t, LANES), 0)
        q_chunk = jnp.where(q_chunk >= t, q_chunk - t, q_chunk) >> CHUNK_SHIFT
        q_extra = jnp.where(q_lane == 0, 1.0, jnp.where(q_lane - 1 > q_chunk, NEG_BIAS, 0.0))
        q2 = jnp.concatenate([q2, q_extra.astype(BF16)], axis=1)
        k_lane = lax.broadcasted_iota(I32, (t, LANES), 1)
        k_chunk = lax.broadcasted_iota(I32, (t, LANES), 0) >> CHUNK_SHIFT
        k_plain = jnp.where(k_lane == 0, -shift_ref[0], 0.0)
        k_diag = jnp.where(k_lane - 1 == k_chunk, 1.0, k_plain).astype(BF16)
        k_plain = k_plain.astype(BF16)

        def step(j, acc):
            k_extra = jnp.where(j == i, k_diag, k_plain)
            s = _dot_nt(q2, jnp.concatenate([k_ref[j], k_extra], axis=1))
            return acc + _dot(jnp.exp2(s).astype(BF16), values(j))

        acc = _unrolled_loop(i + 1, step, acc0)
    else:
        qc = lax.broadcasted_iota(I32, (2 * t, t), 0)
        qc = jnp.where(qc >= t, qc - t, qc) >> CHUNK_SHIFT
        kc = lax.broadcasted_iota(I32, (2 * t, t), 1) >> CHUNK_SHIFT

        def logits(j):
            return _dot_nt(q2, k_ref[j])

        def softmax_step(j, m, acc, s):
            m_new = jnp.maximum(m, jnp.max(s, axis=1, keepdims=True))
            p = jnp.exp2(s - m_new).astype(BF16)
            return m_new, jnp.exp2(m - m_new) * acc + _dot(p, values(j))

        def body(j, carry):
            m, acc, s = carry
            s_next = logits(j + 1)
            m, acc = softmax_step(j, m, acc, s)
            return m, acc, s_next

        m, acc, s = lax.fori_loop(0, i, body, (jnp.full((2 * t, 1), M_INIT, F32), acc0, logits(0)))
        _, acc = softmax_step(i, m, acc, jnp.where(kc <= qc, s, NEG_BIAS))
    o = acc[:, :vdim] / acc[:, vdim:]
    o = o[:t] - lam * o[t:]
    o_ref[...] = (_rms(o, g_ref[...]) * (1.0 - lam_init)).astype(o_ref.dtype)


def _diff_call(shift, bq, bk, bv, lam_vecs, subln_g, lam_init, *, bounded):
    b, s, w = bq.shape
    t = min(T_DIFF, s)
    nb = s // t
    k4 = bk.reshape(b, nb, t, w)
    v4 = bv.reshape(b, nb, t, w)
    qmap = lambda bi, hh, i: (bi, i, hh)
    kvmap = lambda bi, hh, i: (bi, 0, 0, hh)
    return pl.pallas_call(
        functools.partial(_diff_kernel, lam_init=lam_init, bounded=bounded),
        grid=(b, B_HEADS, nb),
        in_specs=[pl.BlockSpec(memory_space=pltpu.SMEM),
                  _const_spec(lam_vecs.shape),
                  pl.BlockSpec((None, t, LANES), qmap),
                  pl.BlockSpec((None, nb, t, LANES), kvmap),
                  pl.BlockSpec((None, nb, t, LANES), kvmap),
                  _const_spec(subln_g.shape)],
        out_specs=pl.BlockSpec((None, t, LANES), qmap),
        out_shape=jax.ShapeDtypeStruct((b, s, w), BF16),
        compiler_params=_params("parallel", "parallel", "arbitrary"),
        name="diff_attn",
    )(shift, lam_vecs, bq, k4, v4, subln_g)


def _odd_kernel(h_ref, g_ref, wc_ref, wd_ref, clg_ref, clb_ref, ws_ref, sb_ref,
                cw_ref, cb_ref, dlg_ref, dlb_ref, woc_ref, wod_ref, o_ref, buf_sc, *, tiles_per_seq):
    tm = h_ref.shape[0]
    cwid = clg_ref.shape[1]

    h = h_ref[...]
    xn = _rms(h, g_ref[...]).astype(BF16)

    zc = jax.nn.gelu(_dot(xn, wc_ref[...]), approximate=True)
    u = zc[:, :cwid]
    v = _layer_norm(zc[:, cwid:], clg_ref[...], clb_ref[...]).astype(BF16)
    pi = lax.broadcasted_iota(I32, (SGU_BLOCK, SGU_BLOCK), 0) >> CHUNK_SHIFT
    pj = lax.broadcasted_iota(I32, (SGU_BLOCK, SGU_BLOCK), 1) >> CHUNK_SHIFT
    causal = pi >= pj
    w_s = [jnp.where(causal, ws_ref[gi], jnp.zeros((SGU_BLOCK, SGU_BLOCK), BF16)) for gi in range(C_GROUPS)]
    lane = lax.broadcasted_iota(I32, (SGU_BLOCK, LANES), 1)
    sgb = sb_ref[...]
    rows_out = []
    for r in range(tm // SGU_BLOCK):
        rs = slice(r * SGU_BLOCK, (r + 1) * SGU_BLOCK)
        slabs = []
        for pr in range(cwid // LANES):
            vs = v[rs, pr * LANES:(pr + 1) * LANES]
            lo = _dot(w_s[2 * pr], vs)
            hi = _dot(w_s[2 * pr + 1], vs)
            slabs.append(jnp.where(lane < HEAD_DIM, lo, hi))
        rows_out.append(jnp.concatenate(slabs, axis=1) + sgb)
    c_out = (u * jnp.concatenate(rows_out, axis=0)).astype(BF16)

    zd = _dot(xn, wd_ref[...])
    dwid = zd.shape[1] // 2
    hd = zd[:, :dwid] * jax.nn.sigmoid(zd[:, dwid:])

    @pl.when(pl.program_id(0) % tiles_per_seq == 0)
    def _():
        buf_sc[0:CONV_HALO, :] = jnp.zeros((CONV_HALO, dwid), F32)

    buf_sc[CONV_HALO:CONV_HALO + tm, :] = hd
    conv = jnp.zeros((tm, dwid), F32) + cb_ref[...]
    base = CONV_HALO - (CONV_W - 1)
    for w in range(CONV_W):
        conv = conv + buf_sc[base + w:base + w + tm, :] * cw_ref[w:w + 1, :]
    buf_sc[0:CONV_HALO, :] = buf_sc[tm:tm + CONV_HALO, :]
    d_out = _layer_norm(conv, dlg_ref[...], dlb_ref[...])
    d_out = (d_out * jax.nn.sigmoid(d_out)).astype(BF16)

    o_ref[...] = h + _dot(c_out, woc_ref[...]) + _dot(d_out, wod_ref[...])


def _odd_call(h, seq_len, g, w_in, w_out, c_ln_g, c_ln_b, c_w_s, c_b_s, d_conv_w, d_conv_b, d_ln_g, d_ln_b):
    n, d = h.shape
    tm = min(TM_ODD, seq_len)
    cwid = c_ln_g.shape[0]
    dwid = d_ln_g.shape[0]
    row = lambda x: x.reshape(1, -1).astype(F32)
    wc = w_in[:, :2 * cwid].astype(BF16)
    wd = w_in[:, 2 * cwid:].astype(BF16)
    sgb = jnp.repeat(c_b_s.T, cwid // C_GROUPS, axis=1).astype(F32)
    consts = (row(g), wc, wd, row(c_ln_g), row(c_ln_b), c_w_s.astype(BF16), sgb,
              d_conv_w.astype(F32), row(d_conv_b), row(d_ln_g), row(d_ln_b),
              w_out[:cwid].astype(BF16), w_out[cwid:].astype(BF16))
    tok = lambda i: (i, 0)
    return pl.pallas_call(
        functools.partial(_odd_kernel, tiles_per_seq=seq_len // tm),
        grid=(n // tm,),
        in_specs=[pl.BlockSpec((tm, d), tok)] + [_const_spec(c.shape) for c in consts],
        out_specs=pl.BlockSpec((tm, d), tok),
        out_shape=jax.ShapeDtypeStruct((n, d), F32),
        scratch_shapes=[pltpu.VMEM((tm + CONV_HALO, dwid), F32)],
        compiler_params=_params("arbitrary"),
        name="odd_mixer",
    )(h, *consts)


def _logit_bound(gq, gk):
    unit = HEAD_DIM * HEAD_DIM ** -0.5 * math.log2(math.e) * BF16_SLACK
    return (unit * jnp.max(jnp.abs(gq)) * jnp.max(jnp.abs(gk))).astype(F32).reshape(1)


def _bounded_or_online(bound, call, *args):
    return lax.cond(bound[0] <= MAX_SHIFT,
                    lambda: call(bound, *args, bounded=True),
                    lambda: call(jnp.zeros_like(bound), *args, bounded=False))
def kernel(x, p, pos, ffn1_g, ffn1_wg, ffn1_wu, ffn1_wd, mix_g, ffn2_g, ffn2_wg, ffn2_wu, ffn2_wd, ple_g, ple_wgate, ple_wproj, ev_w_in, ev_w_out, a_q_g, a_k_g, a_w_uv, b_q_g, b_k_g, b_lam_q1, b_lam_k1, b_lam_q2, b_lam_k2, b_subln_g, od_w_in, od_w_out, c_ln_g, c_ln_b, c_w_s, c_b_s, d_conv_w, d_conv_b, d_ln_g, d_ln_b):
    bsz, s_len, d = x.shape
    n = bsz * s_len
    depth = p.shape[0]
    row = lambda v: v.reshape(1, -1).astype(F32)
    h = x.reshape(n, d)
    for layer in range(depth):
        ffn1 = (row(ffn1_g[layer]), ffn1_wg[layer].astype(BF16), ffn1_wu[layer].astype(BF16),
                ffn1_wd[layer].astype(BF16))
        ffn2 = (row(ffn2_g[layer]), ffn2_wg[layer].astype(BF16), ffn2_wu[layer].astype(BF16),
                ffn2_wd[layer].astype(BF16))
        ple = (p[layer].reshape(n, -1), row(ple_g[layer]), ple_wgate[layer].astype(BF16),
               ple_wproj[layer].astype(BF16))
        h = _ffn_call(h, *ffn1)
        if layer % 2 == 0:
            e = layer // 2
            lam_init = 0.8 - 0.6 * math.exp(-0.3 * layer)
            aq, akd, bq, bk, iq, ikd, av, bv, iwt = _even_proj_call(
                h, pos, row(mix_g[layer]), ev_w_in[e], a_q_g[e], a_k_g[e], b_q_g[e], b_k_g[e])
            r3 = lambda a: a.reshape(bsz, s_len, a.shape[-1])
            a_out = _bounded_or_online(
                _logit_bound(a_q_g[e], a_k_g[e]), _dsa_call,
                r3(aq), r3(iq), iwt, r3(akd), r3(av), r3(ikd), a_w_uv[e])
            lam_vecs = jnp.stack([b_lam_q1[e], b_lam_k1[e], b_lam_q2[e], b_lam_k2[e]]).astype(F32)
            b_out = _bounded_or_online(
                _logit_bound(b_q_g[e], b_k_g[e]), _diff_call,
                r3(bq), r3(bk), r3(bv), lam_vecs, row(b_subln_g[e]), lam_init)
            a_w = A_HEADS * HEAD_DIM
            mix = (a_out.reshape(n, -1), b_out.reshape(n, -1),
                   ev_w_out[e][:a_w].astype(BF16), ev_w_out[e][a_w:].astype(BF16))
            h = _ffn_call(h, *ffn2, mix=mix, ple=ple)
        else:
            o = layer // 2
            h = _odd_call(h, s_len, mix_g[layer], od_w_in[o], od_w_out[o], c_ln_g[o], c_ln_b[o],
                          c_w_s[o], c_b_s[o], d_conv_w[o], d_conv_b[o], d_ln_g[o], d_ln_b[o])
            h = _ffn_call(h, *ffn2, ple=ple)
    return h.reshape(bsz, s_len, d)
```

```python
import functools
import math

import jax
import jax.numpy as jnp
from jax import lax
from jax.experimental import pallas as pl
from jax.experimental.pallas import tpu as pltpu

F32 = jnp.float32
BF16 = jnp.bfloat16
I32 = jnp.int32

EPS = 1e-6
HEAD_DIM = 64
CHUNK = 64
CHUNK_SHIFT = 6
ROPE_THETA = 10000.0
A_HEADS = 8
A_LAT = 128
IDX_HEADS = 4
TOPK_MAX = 256
B_HEADS = 4
C_GROUPS = 8
SGU_BLOCK = 128
CONV_W = 31
CONV_HALO = 32

LANES = 128
SUBLANES = 8
INT_MIN = -(2 ** 31)
NEG_BIAS = -1e30
M_INIT = -1e29
MAX_SHIFT = 60.0
BF16_SLACK = 1.02

FF_CHUNK = 256
TM_FFN = 512
TM_PROJ = 512
TM_ODD = 256
TQ_DSA = 128
TK_DSA = 512
T_DIFF = 512
TIE_BLOCK = 256
WORD_BITS = 32
ATT_UNROLL = 4

VMEM_LIMIT_BYTES = 56 * 1024 * 1024


def _params(*sem):
    return pltpu.CompilerParams(dimension_semantics=sem, vmem_limit_bytes=VMEM_LIMIT_BYTES)


def _const_spec(shape):
    zeros = (0,) * len(shape)
    return pl.BlockSpec(shape, lambda *_: zeros)


def _rms(x, g):
    ms = jnp.mean(x * x, axis=-1, keepdims=True)
    return x * lax.rsqrt(ms + EPS) * g


def _layer_norm(x, g, b):
    mu = jnp.mean(x, axis=-1, keepdims=True)
    xc = x - mu
    var = jnp.mean(xc * xc, axis=-1, keepdims=True)
    return xc * lax.rsqrt(var + EPS) * g + b


def _unrolled_loop(n, step, carry):
    done = 0
    unroll = ATT_UNROLL
    while unroll >= 1:
        def group(g, c, unroll=unroll, done=done):
            for u in range(unroll):
                c = step(done + g * unroll + u, c)
            return c

        n_groups = lax.shift_right_logical(n - done, unroll.bit_length() - 1)
        carry = lax.fori_loop(0, n_groups, group, carry)
        done = done + n_groups * unroll
        unroll //= 2
    return carry


def _dot(a, b):
    return jnp.dot(a, b, preferred_element_type=F32)


def _dot_nt(a, b):
    return lax.dot_general(a, b, (((1,), (1,)), ((), ())), preferred_element_type=F32)


def _ffn_kernel(*refs, has_mix, has_ple, n_chunks):
    refs = list(refs)
    h_ref = refs.pop(0)
    if has_mix:
        a_ref, b_ref, woa_ref, wob_ref = refs[:4]
        refs = refs[4:]
    g_ref, wg_ref, wu_ref, wd_ref = refs[:4]
    refs = refs[4:]
    if has_ple:
        p_ref, pg_ref, wgate_ref, wproj_ref = refs[:4]
        refs = refs[4:]
    (o_ref,) = refs

    h = h_ref[...]
    if has_mix:
        h = h + _dot(a_ref[...], woa_ref[...]) + _dot(b_ref[...], wob_ref[...])
    xn = _rms(h, g_ref[...]).astype(BF16)
    acc = jnp.zeros(h.shape, F32)
    for c in range(n_chunks):
        gt = _dot(xn, wg_ref[:, c * FF_CHUNK:(c + 1) * FF_CHUNK])
        up = _dot(xn, wu_ref[:, c * FF_CHUNK:(c + 1) * FF_CHUNK])
        act = (gt * jax.nn.sigmoid(gt) * up).astype(BF16)
        acc = acc + _dot(act, wd_ref[c * FF_CHUNK:(c + 1) * FF_CHUNK, :])
    h = h + 0.5 * acc
    if has_ple:
        xg = _rms(h, pg_ref[...]).astype(BF16)
        gate = jax.nn.sigmoid(_dot(xg, wgate_ref[...]))
        h = h + gate * _dot(p_ref[...].astype(BF16), wproj_ref[...])
    o_ref[...] = h


def _ffn_call(h, g, wg, wu, wd, mix=None, ple=None):
    n, d = h.shape
    tm = min(TM_FFN, n)
    d_ff = wd.shape[0]
    n_chunks = d_ff // FF_CHUNK
    tok = lambda i: (i, 0)
    args = [h]
    specs = [pl.BlockSpec((tm, d), tok)]
    if mix is not None:
        a, b, woa, wob = mix
        args += [a, b, woa, wob]
        specs += [pl.BlockSpec((tm, a.shape[1]), tok), pl.BlockSpec((tm, b.shape[1]), tok),
                  _const_spec(woa.shape), _const_spec(wob.shape)]
    args += [g, wg, wu, wd]
    specs += [_const_spec(g.shape), _const_spec(wg.shape), _const_spec(wu.shape), _const_spec(wd.shape)]
    if ple is not None:
        p, pg, wgate, wproj = ple
        args += [p, pg, wgate, wproj]
        specs += [pl.BlockSpec((tm, p.shape[1]), tok), _const_spec(pg.shape),
                  _const_spec(wgate.shape), _const_spec(wproj.shape)]
    kern = functools.partial(_ffn_kernel, has_mix=mix is not None, has_ple=ple is not None,
                             n_chunks=n_chunks)
    return pl.pallas_call(
        kern,
        grid=(n // tm,),
        in_specs=specs,
        out_specs=pl.BlockSpec((tm, d), tok),
        out_shape=jax.ShapeDtypeStruct((n, d), F32),
        compiler_params=_params("parallel"),
        name="ffn",
    )(*args)


_NR_WIDTHS = (A_HEADS * HEAD_DIM, 2 * HEAD_DIM, B_HEADS * 2 * HEAD_DIM, B_HEADS * 2 * HEAD_DIM)
_R_WIDTHS = (IDX_HEADS * HEAD_DIM, 2 * HEAD_DIM)
_P_WIDTHS = (A_LAT, B_HEADS * 2 * HEAD_DIM)
IW_ROWS = 8


def _even_proj_kernel(h_ref, pos_ref, g_ref, inv_ref, sgn_ref, wnr_ref, wr_ref, wp_ref, wiwt_ref,
                      gn_ref, gnr_ref, gmat_ref,
                      aq_ref, akd_ref, bq_ref, bk_ref, iq_ref, ikd_ref, av_ref, bv_ref, iwt_ref):
    xn = _rms(h_ref[...], g_ref[...]).astype(BF16)
    ang = pos_ref[...].astype(F32) * inv_ref[...]
    cos = jnp.cos(ang)
    sin = jnp.sin(ang) * sgn_ref[...]
    gmat = gmat_ref[...]
    first_half = lax.broadcasted_iota(I32, cos.shape, 1) % HEAD_DIM < HEAD_DIM // 2

    def swap_halves(xs):
        return jnp.where(first_half, pltpu.roll(xs, LANES - HEAD_DIM // 2, 1), pltpu.roll(xs, HEAD_DIM // 2, 1))

    off = 0
    for width, out_ref in zip(_NR_WIDTHS, (aq_ref, akd_ref, bq_ref, bk_ref)):
        x = _dot(xn, wnr_ref[:, off:off + width])
        for s in range(width // LANES):
            sl = slice(s * LANES, (s + 1) * LANES)
            gsl = slice(off + s * LANES, off + (s + 1) * LANES)
            xs = x[:, sl]
            ms = _dot((xs * xs).astype(BF16), gmat)
            y = (xs * (gn_ref[:, gsl] * cos) + swap_halves(xs) * (gnr_ref[:, gsl] * sin)) * lax.rsqrt(ms + EPS)
            out_ref[:, sl] = y.astype(out_ref.dtype)
        off += width

    off = 0
    for width, out_ref in zip(_R_WIDTHS, (iq_ref, ikd_ref)):
        x = _dot(xn, wr_ref[:, off:off + width])
        for s in range(width // LANES):
            xs = x[:, s * LANES:(s + 1) * LANES]
            out_ref[:, s * LANES:(s + 1) * LANES] = (xs * cos + swap_halves(xs) * sin).astype(out_ref.dtype)
        off += width

    off = 0
    for width, out_ref in zip(_P_WIDTHS, (av_ref, bv_ref)):
        out_ref[...] = _dot(xn, wp_ref[:, off:off + width]).astype(out_ref.dtype)
        off += width
    iwt_ref[...] = _dot_nt(wiwt_ref[...], xn)


def _swap_halves(w):
    lead = w.shape[:-1]
    w4 = w.reshape(*lead, -1, 2, HEAD_DIM // 2)
    return w4[..., ::-1, :].reshape(*lead, -1)


def _even_proj_call(h, pos, g, w_in, a_q_g, a_k_g, b_q_g, b_k_g):
    n, d = h.shape
    tm = min(TM_PROJ, n)
    hd = HEAD_DIM
    sizes = (A_HEADS * hd, hd, A_LAT, IDX_HEADS * hd, hd, IDX_HEADS,
             B_HEADS * 2 * hd, B_HEADS * 2 * hd, B_HEADS * 2 * hd)
    offs = [0]
    for s in sizes:
        offs.append(offs[-1] + s)
    w_aq, w_ak, w_av, w_iq, w_ik, w_iw, w_bq, w_bk, w_bv = (
        w_in[:, offs[i]:offs[i + 1]] for i in range(9))
    dup = lambda w: jnp.concatenate([w, w], axis=1)
    wnr = jnp.concatenate([w_aq, dup(w_ak), w_bq, w_bk], axis=1).astype(BF16)
    wr = jnp.concatenate([w_iq, dup(w_ik)], axis=1).astype(BF16)
    wp = jnp.concatenate([w_av, w_bv], axis=1).astype(BF16)
    wiwt = jnp.pad(w_iw.T, ((0, IW_ROWS - IDX_HEADS), (0, 0))).astype(BF16)

    scale = hd ** -0.5 * math.log2(math.e)
    gn =jnp.concatenate([jnp.tile(a_q_g, A_HEADS) * scale, jnp.tile(a_k_g, 2),
                          jnp.tile(b_q_g, 2 * B_HEADS) * scale, jnp.tile(b_k_g, 2 * B_HEADS)])
    gnr = _swap_halves(gn)
    gn = gn.reshape(1, -1).astype(F32)
    gnr = gnr.reshape(1, -1).astype(F32)

    inv = ROPE_THETA ** (-jnp.arange(0, hd, 2, dtype=F32) / hd)
    inv128 = jnp.tile(inv, LANES // (hd // 2)).reshape(1, LANES)
    sgn = jnp.tile(jnp.concatenate([-jnp.ones(hd // 2, F32), jnp.ones(hd // 2, F32)]),
                   LANES // hd).reshape(1, LANES)
    head_of_lane = jnp.arange(LANES) // hd
    gmat = ((head_of_lane[:, None] == head_of_lane[None, :]).astype(F32) / hd).astype(BF16)

    tok = lambda i: (i, 0)
    out_widths = (A_HEADS * hd, 2 * hd, B_HEADS * 2 * hd, B_HEADS * 2 * hd,
                  IDX_HEADS * hd, 2 * hd, A_LAT, B_HEADS * 2 * hd)
    consts = (g, inv128, sgn, wnr, wr, wp, wiwt, gn, gnr, gmat)
    return pl.pallas_call(
        _even_proj_kernel,
        grid=(n // tm,),
        in_specs=[pl.BlockSpec((tm, d), tok), pl.BlockSpec((tm, 1), tok)]
                 + [_const_spec(c.shape) for c in consts],
        out_specs=[pl.BlockSpec((tm, w), tok) for w in out_widths]
                  + [pl.BlockSpec((IW_ROWS, tm), lambda i: (0, i))],
        out_shape=[jax.ShapeDtypeStruct((n, w), BF16) for w in out_widths]
                  + [jax.ShapeDtypeStruct((IW_ROWS, n), F32)],
        compiler_params=_params("parallel"),
        name="even_proj",
    )(h, pos.reshape(n, 1), *consts)


def _bit_planes(words):
    a = list(words)
    j, m = WORD_BITS // 2, 0x0000FFFF
    while j:
        k = 0
        while k < WORD_BITS:
            t = (a[k] ^ lax.shift_right_logical(a[k + j], j)) & m
            a[k] = a[k] ^ t
            a[k + j] = a[k + j] ^ lax.shift_left(t, j)
            k = (k + j + 1) & ~j
        j >>= 1
        m ^= m << j
    return a


def _stack_heads(x, n_heads):
    lane = lax.broadcasted_iota(I32, (x.shape[0], LANES), 1)
    parts = []
    for hh in range(n_heads):
        slab = x[:, (hh // 2) * LANES:(hh // 2 + 1) * LANES]
        keep = (lane < HEAD_DIM) if hh % 2 == 0 else (lane >= HEAD_DIM)
        parts.append(jnp.where(keep, slab, jnp.zeros_like(slab)))
    return jnp.concatenate(parts, axis=0)


def _dsa_kernel(shift_ref, aq_ref, iq_ref, iwt_ref, k_ref, v_ref, ik_ref, tri_ref, wuv_ref, o_ref,
                key_sc, bias_sc, plane_sc, alive_sc, *, top_k, bounded):
    tq = aq_ref.shape[0]
    tk = k_ref.shape[1]
    q0 = pl.program_id(1) * tq
    n_kb = (q0 + tq + tk - 1) // tk

    iq_all = _stack_heads(iq_ref[...], IDX_HEADS)
    iwt = iwt_ref[...]
    qchunk = (q0 + lax.broadcasted_iota(I32, (tk, tq), 1)) >> CHUNK_SHIFT
    krow = lax.broadcasted_iota(I32, (tk, tq), 0)
    groups = tk // (WORD_BITS * 8)

    def score_block(kb, masked):
        dots = jnp.maximum(_dot_nt(ik_ref[kb], iq_all), 0.0)
        sc = dots[:, 0:tq] * iwt[0:1, :]
        for hh in range(1, IDX_HEADS):
            sc = sc + dots[:, hh * tq:(hh + 1) * tq] * iwt[hh:hh + 1, :]
        bits = pltpu.bitcast(sc, I32)
        key = jnp.where(bits < 0, INT_MIN - bits, bits)
        if masked:
            key = jnp.where(((kb * tk + krow) >> CHUNK_SHIFT) <= qchunk, key, INT_MIN)
        key_sc[kb] = key
        u = key ^ INT_MIN
        for g in range(groups):
            base = g * WORD_BITS * 8
            planes = _bit_planes([u[base + 8 * j:base + 8 * j + 8] for j in range(WORD_BITS)])
            for p in range(WORD_BITS):
                plane_sc[p, kb, 8 * g:8 * g + 8, :] = planes[p]
        alive_sc[kb] = jnp.full((8 * groups, tq), -1, I32)

    def score_body(kb, carry):
        score_block(kb, False)
        return carry

    lax.fori_loop(0, n_kb - 1, score_body, 0)
    score_block(n_kb - 1, True)

    def radix_body(b, carry):
        t_u, k_rem, keep = carry
        first = jnp.where(b == 0, jnp.int32(-1), jnp.int32(0))
        b_prev = jnp.maximum(b - 1, 0)

        def pass_step(kb, acc):
            a = alive_sc[kb] & ((plane_sc[b_prev, kb] ^ keep) | first)
            alive_sc[kb] = a
            return acc + lax.population_count(a & plane_sc[b, kb])

        acc = _unrolled_loop(n_kb, pass_step, jnp.zeros((8 * groups, tq), I32))
        cnt = jnp.sum(acc.astype(F32), axis=0, keepdims=True)
        take = cnt >= k_rem
        t_u = t_u | jnp.where(take, lax.shift_left(jnp.int32(1), 31 - b), 0)
        return t_u, jnp.where(take, k_rem, k_rem - cnt), jnp.where(take, 0, -1)

    init = (jnp.zeros((1, tq), I32), jnp.full((1, tq), float(top_k), F32), jnp.zeros((1, tq), I32))
    t_u, k_rem, keep = lax.fori_loop(0, WORD_BITS, radix_body, init)

    def eq_body(kb, acc):
        return acc + lax.population_count(alive_sc[kb] & (plane_sc[WORD_BITS - 1, kb] ^ keep))

    n_eq = lax.fori_loop(0, n_kb, eq_body, jnp.zeros((8 * groups, tq), I32))
    n_eq = jnp.sum(n_eq.astype(F32), axis=0, keepdims=True)
    t = t_u ^ INT_MIN
    few = t == INT_MIN
    need = jnp.where(few, 0.0, k_rem)
    n_tied = jnp.sum(jnp.where(few | (n_eq == need), 0.0, 1.0))

    sel_bias = -shift_ref[0]

    def plain_bias():
        t_min = jnp.where(few, INT_MIN + 1, t)

        def body(kb, carry):
            bias_sc[kb] = jnp.where(key_sc[kb] >= t_min, sel_bias, NEG_BIAS).astype(BF16)
            return carry

        lax.fori_loop(0, n_kb, body, 0)

    def ranked_bias():
        tri = tri_ref[...]

        def body(kb, carry):
            kk_all = key_sc[kb]
            for c in range(tk // TIE_BLOCK):
                kk = kk_all[c * TIE_BLOCK:(c + 1) * TIE_BLOCK]
                eq = kk == t
                rank = _dot(tri, jnp.where(eq, 1.0, 0.0).astype(BF16)) + carry
                tie_bias = jnp.where(rank <= need, sel_bias, NEG_BIAS)
                bias = jnp.where(kk > t, sel_bias, jnp.where(eq, tie_bias, NEG_BIAS))
                bias_sc[kb, c * TIE_BLOCK:(c + 1) * TIE_BLOCK, :] = bias.astype(BF16)
                carry = rank[TIE_BLOCK - 1:TIE_BLOCK, :]
            return carry

        lax.fori_loop(0, n_kb, body, jnp.zeros((1, tq), F32))

    lax.cond(n_tied == 0.0, plain_bias, ranked_bias)

    rows = A_HEADS * tq
    eye = lax.broadcasted_iota(I32, (tq, tq), 0) == lax.broadcasted_iota(I32, (tq, tq), 1)
    eye = jnp.where(eye, 1.0, 0.0).astype(BF16)
    qe = jnp.concatenate([_stack_heads(aq_ref[...], A_HEADS),
                          jnp.concatenate([eye] * A_HEADS, axis=0)], axis=1)
    ones = jnp.ones((tk, LANES), BF16)

    def logits(kb):
        return _dot_nt(qe, jnp.concatenate([k_ref[kb], bias_sc[kb]], axis=1))

    def values(kb):
        return jnp.concatenate([v_ref[kb], ones], axis=1)

    acc0 = jnp.zeros((rows, A_LAT + LANES), F32)
    if bounded:
        def att_step(kb, acc):
            return acc + _dot(jnp.exp2(logits(kb)).astype(BF16), values(kb))

        acc = _unrolled_loop(n_kb, att_step, acc0)
    else:
        def softmax_step(kb, m, acc, s):
            m_new = jnp.maximum(m, jnp.max(s, axis=1, keepdims=True))
            p = jnp.exp2(s - m_new).astype(BF16)
            return m_new, jnp.exp2(m - m_new) * acc + _dot(p, values(kb))

        def att_body(kb, carry):
            m, acc, s = carry
            s_next = logits(kb + 1)
            m, acc = softmax_step(kb, m, acc, s)
            return m, acc, s_next

        m, acc, s = lax.fori_loop(0, n_kb - 1, att_body, (jnp.full((rows, 1), M_INIT, F32), acc0, logits(0)))
        _, acc = softmax_step(n_kb - 1, m, acc, s)
    o_lat = (acc[:, :A_LAT] / acc[:, A_LAT:]).astype(BF16)
    o_all = jnp.concatenate([o_lat[hh * tq:(hh + 1) * tq] for hh in range(A_HEADS)], axis=1)
    o_ref[...] = _dot(o_all, wuv_ref[...]).astype(o_ref.dtype)


def _dsa_call(shift, aq, iq, iwt, akd, av, ikd, w_uv, *, bounded):
    b, s, _ = aq.shape
    tq = min(TQ_DSA, s)
    tk = min(TK_DSA, s)
    assert tq == LANES, "the one-hot mask columns pair one query tile with 128 contraction lanes"
    top_k = min(TOPK_MAX, s // 4)
    n_kb = s // tk
    nq = s // tq
    blk = lambda x: x.reshape(b, n_kb, tk, x.shape[-1])
    tri = (jnp.arange(TIE_BLOCK)[:, None] >= jnp.arange(TIE_BLOCK)[None, :]).astype(BF16)
    eye = jnp.eye(A_HEADS, dtype=w_uv.dtype)
    wuv_bd = (w_uv[:, :, None, :] * eye[:, None, :, None]).reshape(A_HEADS * A_LAT, A_HEADS * HEAD_DIM)
    wuv_bd = wuv_bd.astype(BF16)

    qmap = lambda bi, i: (bi, i, 0)
    kvmap = lambda bi, i: (bi, 0, 0, 0)
    kv_spec = pl.BlockSpec((None, n_kb, tk, LANES), kvmap)
    return pl.pallas_call(
        functools.partial(_dsa_kernel, top_k=top_k, bounded=bounded),
        grid=(b, nq),
        in_specs=[pl.BlockSpec(memory_space=pltpu.SMEM),
                  pl.BlockSpec((None, tq, aq.shape[-1]), qmap),
                  pl.BlockSpec((None, tq, iq.shape[-1]), qmap),
                  pl.BlockSpec((iwt.shape[0], tq), lambda bi, i: (0, bi * nq + i)),
                  kv_spec, kv_spec, kv_spec,
                  _const_spec(tri.shape), _const_spec(wuv_bd.shape)],
        out_specs=pl.BlockSpec((None, tq, A_HEADS * HEAD_DIM), qmap),
        out_shape=jax.ShapeDtypeStruct((b, s, A_HEADS * HEAD_DIM), BF16),
        scratch_shapes=[pltpu.VMEM((n_kb, tk, tq), I32), pltpu.VMEM((n_kb, tk, tq), BF16),
                        pltpu.VMEM((WORD_BITS, n_kb, tk // WORD_BITS, tq), I32),
                        pltpu.VMEM((n_kb, tk // WORD_BITS, tq), I32)],
        compiler_params=_params("parallel", "arbitrary"),
        name="dsa",
    )(shift, aq, iq, iwt, blk(akd), blk(av), blk(ikd), tri, wuv_bd)


def _diff_kernel(shift_ref, lam_ref, q_ref, k_ref, v_ref, g_ref, o_ref, *, lam_init, bounded):
    t = q_ref.shape[0]
    i = pl.program_id(2)
    lam_v = lam_ref[...]
    lam = (jnp.exp(jnp.sum(lam_v[0:1] * lam_v[1:2], axis=1, keepdims=True))
           - jnp.exp(jnp.sum(lam_v[2:3] * lam_v[3:4], axis=1, keepdims=True)) + lam_init)

    q = q_ref[...]
    lane = lax.broadcasted_iota(I32, q.shape, 1)
    zero = jnp.zeros_like(q)
    q2 = jnp.concatenate([jnp.where(lane < HEAD_DIM, q, zero),
                          jnp.where(lane >= HEAD_DIM, q, zero)], axis=0)

    vdim = v_ref.shape[-1]
    ones = jnp.ones((t, LANES), BF16)

    def values(j):
        return jnp.concatenate([v_ref[j], ones], axis=1)

    acc0 = jnp.zeros((2 * t, vdim + LANES), F32)

    if bounded:
        assert t // CHUNK < LANES
        q_lane = lax.broadcasted_iota(I32, (2 * t, LANES), 1)
        q_chunk = lax.broadcasted_iota(I32, (2 * t, LANES), 0)
        q_chunk = jnp.where(q_chunk >= t, q_chunk - t, q_chunk) >> CHUNK_SHIFT
        q_extra = jnp.where(q_lane == 0, 1.0, jnp.where(q_lane - 1 > q_chunk, NEG_BIAS, 0.0))
        q2 = jnp.concatenate([q2, q_extra.astype(BF16)], axis=1)
        k_lane = lax.broadcasted_iota(I32, (t, LANES), 1)
        k_chunk = lax.broadcasted_iota(I32, (t, LANES), 0) >> CHUNK_SHIFT
        k_plain = jnp.where(k_lane == 0, -shift_ref[0], 0.0)
        k_diag = jnp.where(k_lane - 1 == k_chunk, 1.0, k_plain).astype(BF16)
        k_plain = k_plain.astype(BF16)

        def step(j, acc):
            k_extra = jnp.where(j == i, k_diag, k_plain)
            s = _dot_nt(q2, jnp.concatenate([k_ref[j], k_extra], axis=1))
            return acc + _dot(jnp.exp2(s).astype(BF16), values(j))

        acc = _unrolled_loop(i + 1, step, acc0)
    else:
        qc = lax.broadcasted_iota(I32, (2 * t, t), 0)
        qc = jnp.where(qc >= t, qc - t, qc) >> CHUNK_SHIFT
        kc = lax.broadcasted_iota(I32, (2 * t, t), 1) >> CHUNK_SHIFT

        def logits(j):
            return _dot_nt(q2, k_ref[j])

        def softmax_step(j, m, acc, s):
            m_new = jnp.maximum(m, jnp.max(s, axis=1, keepdims=True))
            p = jnp.exp2(s - m_new).astype(BF16)
            return m_new, jnp.exp2(m - m_new) * acc + _dot(p, values(j))

        def body(j, carry):
            m, acc, s = carry
            s_next = logits(j + 1)
            m, acc = softmax_step(j, m, acc, s)
            return m, acc, s_next

        m, acc, s = lax.fori_loop(0, i, body, (jnp.full((2 * t, 1), M_INIT, F32), acc0, logits(0)))
        _, acc = softmax_step(i, m, acc, jnp.where(kc <= qc, s, NEG_BIAS))
    o = acc[:, :vdim] / acc[:, vdim:]
    o = o[:t] - lam * o[t:]
    o_ref[...] = (_rms(o, g_ref[...]) * (1.0 - lam_init)).astype(o_ref.dtype)


def _diff_call(shift, bq, bk, bv, lam_vecs, subln_g, lam_init, *, bounded):
    b, s, w = bq.shape
    t = min(T_DIFF, s)
    nb = s // t
    k4 = bk.reshape(b, nb, t, w)
    v4 = bv.reshape(b, nb, t, w)
    qmap = lambda bi, hh, i: (bi, i, hh)
    kvmap = lambda bi, hh, i: (bi, 0, 0, hh)
    return pl.pallas_call(
        functools.partial(_diff_kernel, lam_init=lam_init, bounded=bounded),
        grid=(b, B_HEADS, nb),
        in_specs=[pl.BlockSpec(memory_space=pltpu.SMEM),
                  _const_spec(lam_vecs.shape),
                  pl.BlockSpec((None, t, LANES), qmap),
                  pl.BlockSpec((None, nb, t, LANES), kvmap),
                  pl.BlockSpec((None, nb, t, LANES), kvmap),
                  _const_spec(subln_g.shape)],
        out_specs=pl.BlockSpec((None, t, LANES), qmap),
        out_shape=jax.ShapeDtypeStruct((b, s, w), BF16),
        compiler_params=_params("parallel", "parallel", "arbitrary"),
        name="diff_attn",
    )(shift, lam_vecs, bq, k4, v4, subln_g)


def _odd_kernel(h_ref, g_ref, wc_ref, wd_ref, clg_ref, clb_ref, ws_ref, sb_ref,
                cw_ref, cb_ref, dlg_ref, dlb_ref, woc_ref, wod_ref, o_ref, buf_sc, *, tiles_per_seq):
    tm = h_ref.shape[0]
    cwid = clg_ref.shape[1]

    h = h_ref[...]
    xn = _rms(h, g_ref[...]).astype(BF16)

    zc = jax.nn.gelu(_dot(xn, wc_ref[...]), approximate=True)
    u = zc[:, :cwid]
    v = _layer_norm(zc[:, cwid:], clg_ref[...], clb_ref[...]).astype(BF16)
    pi = lax.broadcasted_iota(I32, (SGU_BLOCK, SGU_BLOCK), 0) >> CHUNK_SHIFT
    pj = lax.broadcasted_iota(I32, (SGU_BLOCK, SGU_BLOCK), 1) >> CHUNK_SHIFT
    causal = pi >= pj
    w_s = [jnp.where(causal, ws_ref[gi], jnp.zeros((SGU_BLOCK, SGU_BLOCK), BF16)) for gi in range(C_GROUPS)]
    lane = lax.broadcasted_iota(I32, (SGU_BLOCK, LANES), 1)
    sgb = sb_ref[...]
    rows_out = []
    for r in range(tm // SGU_BLOCK):
        rs = slice(r * SGU_BLOCK, (r + 1) * SGU_BLOCK)
        slabs = []
        for pr in range(cwid // LANES):
            vs = v[rs, pr * LANES:(pr + 1) * LANES]
            lo = _dot(w_s[2 * pr], vs)
            hi = _dot(w_s[2 * pr + 1], vs)
            slabs.append(jnp.where(lane < HEAD_DIM, lo, hi))
        rows_out.append(jnp.concatenate(slabs, axis=1) + sgb)
    c_out = (u * jnp.concatenate(rows_out, axis=0)).astype(BF16)

    zd = _dot(xn, wd_ref[...])
    dwid = zd.shape[1] // 2
    hd = zd[:, :dwid] * jax.nn.sigmoid(zd[:, dwid:])

    @pl.when(pl.program_id(0) % tiles_per_seq == 0)
    def _():
        buf_sc[0:CONV_HALO, :] = jnp.zeros((CONV_HALO, dwid), F32)

    buf_sc[CONV_HALO:CONV_HALO + tm, :] = hd
    buf_sc[CONV_HALO + tm:, :] = jnp.zeros((SUBLANES, dwid), F32)
    conv = jnp.zeros((tm, dwid), F32) + cb_ref[...]
    base = CONV_HALO - (CONV_W - 1)
    for r in range(SUBLANES):
        group = None
        for w in range(CONV_W):
            if (base + w) % SUBLANES == r:
                a = (base + w) // SUBLANES * SUBLANES
                term = buf_sc[a:a + tm + SUBLANES, :] * cw_ref[w:w + 1, :]
                group = term if group is None else group + term
        conv = conv + group[r:r + tm]
    buf_sc[0:CONV_HALO, :] = buf_sc[tm:tm + CONV_HALO, :]
    d_out = _layer_norm(conv, dlg_ref[...], dlb_ref[...])
    d_out = (d_out * jax.nn.sigmoid(d_out)).astype(BF16)

    o_ref[...] = h + _dot(c_out, woc_ref[...]) + _dot(d_out, wod_ref[...])


def _odd_call(h, seq_len, g, w_in, w_out, c_ln_g, c_ln_b, c_w_s, c_b_s, d_conv_w, d_conv_b, d_ln_g, d_ln_b):
    n, d = h.shape
    tm = min(TM_ODD, seq_len)
    cwid = c_ln_g.shape[0]
    dwid = d_ln_g.shape[0]
    row = lambda x: x.reshape(1, -1).astype(F32)
    wc = w_in[:, :2 * cwid].astype(BF16)
    wd = w_in[:, 2 * cwid:].astype(BF16)
    sgb = jnp.repeat(c_b_s.T, cwid // C_GROUPS, axis=1).astype(F32)
    consts = (row(g), wc, wd, row(c_ln_g), row(c_ln_b), c_w_s.astype(BF16), sgb,
              d_conv_w.astype(F32), row(d_conv_b), row(d_ln_g), row(d_ln_b),
              w_out[:cwid].astype(BF16), w_out[cwid:].astype(BF16))
    tok = lambda i: (i, 0)
    return pl.pallas_call(
        functools.partial(_odd_kernel, tiles_per_seq=seq_len // tm),
        grid=(n // tm,),
        in_specs=[pl.BlockSpec((tm, d), tok)] + [_const_spec(c.shape) for c in consts],
        out_specs=pl.BlockSpec((tm, d), tok),
        out_shape=jax.ShapeDtypeStruct((n, d), F32),
        scratch_shapes=[pltpu.VMEM((tm + CONV_HALO + SUBLANES, dwid), F32)],
        compiler_params=_params("arbitrary"),
        name="odd_mixer",
    )(h, *consts)


def _logit_bound(gq, gk):
    unit = HEAD_DIM * HEAD_DIM ** -0.5 * math.log2(math.e) * BF16_SLACK
    return (unit * jnp.max(jnp.abs(gq)) * jnp.max(jnp.abs(gk))).astype(F32).reshape(1)


def _bounded_or_online(bound, call, *args):
    return lax.cond(bound[0] <= MAX_SHIFT,
                    lambda: call(bound, *args, bounded=True),
                    lambda: call(jnp.zeros_like(bound), *args, bounded=False))
def kernel(x, p, pos, ffn1_g, ffn1_wg, ffn1_wu, ffn1_wd, mix_g, ffn2_g, ffn2_wg, ffn2_wu, ffn2_wd, ple_g, ple_wgate, ple_wproj, ev_w_in, ev_w_out, a_q_g, a_k_g, a_w_uv, b_q_g, b_k_g, b_lam_q1, b_lam_k1, b_lam_q2, b_lam_k2, b_subln_g, od_w_in, od_w_out, c_ln_g, c_ln_b, c_w_s, c_b_s, d_conv_w, d_conv_b, d_ln_g, d_ln_b):
    bsz, s_len, d = x.shape
    n = bsz * s_len
    depth = p.shape[0]
    row = lambda v: v.reshape(1, -1).astype(F32)
    h = x.reshape(n, d)
    for layer in range(depth):
        ffn1 = (row(ffn1_g[layer]), ffn1_wg[layer].astype(BF16), ffn1_wu[layer].astype(BF16),
                ffn1_wd[layer].astype(BF16))
        ffn2 = (row(ffn2_g[layer]), ffn2_wg[layer].astype(BF16), ffn2_wu[layer].astype(BF16),
                ffn2_wd[layer].astype(BF16))
        ple = (p[layer].reshape(n, -1), row(ple_g[layer]), ple_wgate[layer].astype(BF16),
               ple_wproj[layer].astype(BF16))
        h = _ffn_call(h, *ffn1)
        if layer % 2 == 0:
            e = layer // 2
            lam_init = 0.8 - 0.6 * math.exp(-0.3 * layer)
            aq, akd, bq, bk, iq, ikd, av, bv, iwt = _even_proj_call(
                h, pos, row(mix_g[layer]), ev_w_in[e], a_q_g[e], a_k_g[e], b_q_g[e], b_k_g[e])
            r3 = lambda a: a.reshape(bsz, s_len, a.shape[-1])
            a_out = _bounded_or_online(
                _logit_bound(a_q_g[e], a_k_g[e]), _dsa_call,
                r3(aq), r3(iq), iwt, r3(akd), r3(av), r3(ikd), a_w_uv[e])
            lam_vecs = jnp.stack([b_lam_q1[e], b_lam_k1[e], b_lam_q2[e], b_lam_k2[e]]).astype(F32)
            b_out = _bounded_or_online(
                _logit_bound(b_q_g[e], b_k_g[e]), _diff_call,
                r3(bq), r3(bk), r3(bv), lam_vecs, row(b_subln_g[e]), lam_init)
            a_w = A_HEADS * HEAD_DIM
            mix = (a_out.reshape(n, -1), b_out.reshape(n, -1),
                   ev_w_out[e][:a_w].astype(BF16), ev_w_out[e][a_w:].astype(BF16))
            h = _ffn_call(h, *ffn2, mix=mix, ple=ple)
        else:
            o = layer // 2
            h = _odd_call(h, s_len, mix_g[layer], od_w_in[o], od_w_out[o], c_ln_g[o], c_ln_b[o],
                          c_w_s[o], c_b_s[o], d_conv_w[o], d_conv_b[o], d_ln_g[o], d_ln_b[o])
            h = _ffn_call(h, *ffn2, ple=ple)
    return h.reshape(bsz, s_len, d)
```

```python
import functools
import math

import jax
import jax.numpy as jnp
from jax import lax
from jax.experimental import pallas as pl
from jax.experimental.pallas import tpu as pltpu

F32 = jnp.float32
BF16 = jnp.bfloat16
I32 = jnp.int32

EPS = 1e-6
HEAD_DIM = 64
CHUNK = 64
CHUNK_SHIFT = 6
ROPE_THETA = 10000.0
A_HEADS = 8
A_LAT = 128
IDX_HEADS = 4
TOPK_MAX = 256
B_HEADS = 4
C_GROUPS = 8
SGU_BLOCK = 128
CONV_W = 31
CONV_HALO = 32

LANES = 128
SUBLANES = 8
INT_MIN = -(2 ** 31)
NEG_BIAS = -1e30
M_INIT = -1e29
MAX_SHIFT = 60.0
BF16_SLACK = 1.02

FF_CHUNK = 256
TM_FFN = 512
TM_PROJ = 512
TM_ODD = 256
TQ_DSA = 128
TK_DSA = 512
T_DIFF = 512
TIE_BLOCK = 256
WORD_BITS = 32
ATT_UNROLL = 4

VMEM_LIMIT_BYTES = 56 * 1024 * 1024


def _params(*sem):
    return pltpu.CompilerParams(dimension_semantics=sem, vmem_limit_bytes=VMEM_LIMIT_BYTES)


def _const_spec(shape):
    zeros = (0,) * len(shape)
    return pl.BlockSpec(shape, lambda *_: zeros)


def _rms(x, g):
    ms = jnp.mean(x * x, axis=-1, keepdims=True)
    return x * lax.rsqrt(ms + EPS) * g


def _layer_norm(x, g, b):
    mu = jnp.mean(x, axis=-1, keepdims=True)
    xc = x - mu
    var = jnp.mean(xc * xc, axis=-1, keepdims=True)
    return xc * lax.rsqrt(var + EPS) * g + b


def _grouped_loop(n, group_step, carry):
    done = 0
    unroll = ATT_UNROLL
    while unroll >= 1:
        def group(g, c, unroll=unroll, done=done):
            first = done + g * unroll
            return group_step([first + u for u in range(unroll)], c)

        n_groups = lax.shift_right_logical(n - done, unroll.bit_length() - 1)
        carry = lax.fori_loop(0, n_groups, group, carry)
        done = done + n_groups * unroll
        unroll //= 2
    return carry


def _unrolled_loop(n, step, carry):
    def group_step(ks, c):
        for k in ks:
            c = step(k, c)
        return c

    return _grouped_loop(n, group_step, carry)


def _dot(a, b):
    return jnp.dot(a, b, preferred_element_type=F32)


def _dot_nt(a, b):
    return lax.dot_general(a, b, (((1,), (1,)), ((), ())), preferred_element_type=F32)


def _ffn_kernel(*refs, has_mix, has_ple, n_chunks):
    refs = list(refs)
    h_ref = refs.pop(0)
    if has_mix:
        a_ref, b_ref, woa_ref, wob_ref = refs[:4]
        refs = refs[4:]
    g_ref, wg_ref, wu_ref, wd_ref = refs[:4]
    refs = refs[4:]
    if has_ple:
        p_ref, pg_ref, wgate_ref, wproj_ref = refs[:4]
        refs = refs[4:]
    (o_ref,) = refs

    h = h_ref[...]
    if has_mix:
        h = h + _dot(a_ref[...], woa_ref[...]) + _dot(b_ref[...], wob_ref[...])
    xn = _rms(h, g_ref[...]).astype(BF16)
    acc = jnp.zeros(h.shape, F32)
    for c in range(n_chunks):
        gt = _dot(xn, wg_ref[:, c * FF_CHUNK:(c + 1) * FF_CHUNK])
        up = _dot(xn, wu_ref[:, c * FF_CHUNK:(c + 1) * FF_CHUNK])
        act = (gt * jax.nn.sigmoid(gt) * up).astype(BF16)
        acc = acc + _dot(act, wd_ref[c * FF_CHUNK:(c + 1) * FF_CHUNK, :])
    h = h + 0.5 * acc
    if has_ple:
        xg = _rms(h, pg_ref[...]).astype(BF16)
        gate = jax.nn.sigmoid(_dot(xg, wgate_ref[...]))
        h = h + gate * _dot(p_ref[...].astype(BF16), wproj_ref[...])
    o_ref[...] = h


def _ffn_call(h, g, wg, wu, wd, mix=None, ple=None):
    n, d = h.shape
    tm = min(TM_FFN, n)
    d_ff = wd.shape[0]
    n_chunks = d_ff // FF_CHUNK
    tok = lambda i: (i, 0)
    args = [h]
    specs = [pl.BlockSpec((tm, d), tok)]
    if mix is not None:
        a, b, woa, wob = mix
        args += [a, b, woa, wob]
        specs += [pl.BlockSpec((tm, a.shape[1]), tok), pl.BlockSpec((tm, b.shape[1]), tok),
                  _const_spec(woa.shape), _const_spec(wob.shape)]
    args += [g, wg, wu, wd]
    specs += [_const_spec(g.shape), _const_spec(wg.shape), _const_spec(wu.shape), _const_spec(wd.shape)]
    if ple is not None:
        p, pg, wgate, wproj = ple
        args += [p, pg, wgate, wproj]
        specs += [pl.BlockSpec((tm, p.shape[1]), tok), _const_spec(pg.shape),
                  _const_spec(wgate.shape), _const_spec(wproj.shape)]
    kern = functools.partial(_ffn_kernel, has_mix=mix is not None, has_ple=ple is not None,
                             n_chunks=n_chunks)
    return pl.pallas_call(
        kern,
        grid=(n // tm,),
        in_specs=specs,
        out_specs=pl.BlockSpec((tm, d), tok),
        out_shape=jax.ShapeDtypeStruct((n, d), F32),
        compiler_params=_params("parallel"),
        name="ffn",
    )(*args)


_NR_WIDTHS = (A_HEADS * HEAD_DIM, 2 * HEAD_DIM, B_HEADS * 2 * HEAD_DIM, B_HEADS * 2 * HEAD_DIM)
_R_WIDTHS = (IDX_HEADS * HEAD_DIM, 2 * HEAD_DIM)
_P_WIDTHS = (A_LAT, B_HEADS * 2 * HEAD_DIM)
IW_ROWS = 8


def _even_proj_kernel(h_ref, pos_ref, g_ref, inv_ref, sgn_ref, wnr_ref, wr_ref, wp_ref, wiwt_ref,
                      gn_ref, gnr_ref, gmat_ref,
                      aq_ref, akd_ref, bq_ref, bk_ref, iq_ref, ikd_ref, av_ref, bv_ref, iwt_ref):
    xn = _rms(h_ref[...], g_ref[...]).astype(BF16)
    ang = pos_ref[...].astype(F32) * inv_ref[...]
    cos = jnp.cos(ang)
    sin = jnp.sin(ang) * sgn_ref[...]
    gmat = gmat_ref[...]
    first_half = lax.broadcasted_iota(I32, cos.shape, 1) % HEAD_DIM < HEAD_DIM // 2

    def swap_halves(xs):
        return jnp.where(first_half, pltpu.roll(xs, LANES - HEAD_DIM // 2, 1), pltpu.roll(xs, HEAD_DIM // 2, 1))

    off = 0
    for width, out_ref in zip(_NR_WIDTHS, (aq_ref, akd_ref, bq_ref, bk_ref)):
        x = _dot(xn, wnr_ref[:, off:off + width])
        for s in range(width // LANES):
            sl = slice(s * LANES, (s + 1) * LANES)
            gsl = slice(off + s * LANES, off + (s + 1) * LANES)
            xs = x[:, sl]
            ms = _dot((xs * xs).astype(BF16), gmat)
            y = (xs * (gn_ref[:, gsl] * cos) + swap_halves(xs) * (gnr_ref[:, gsl] * sin)) * lax.rsqrt(ms + EPS)
            out_ref[:, sl] = y.astype(out_ref.dtype)
        off += width

    off = 0
    for width, out_ref in zip(_R_WIDTHS, (iq_ref, ikd_ref)):
        x = _dot(xn, wr_ref[:, off:off + width])
        for s in range(width // LANES):
            xs = x[:, s * LANES:(s + 1) * LANES]
            out_ref[:, s * LANES:(s + 1) * LANES] = (xs * cos + swap_halves(xs) * sin).astype(out_ref.dtype)
        off += width

    off = 0
    for width, out_ref in zip(_P_WIDTHS, (av_ref, bv_ref)):
        out_ref[...] = _dot(xn, wp_ref[:, off:off + width]).astype(out_ref.dtype)
        off += width
    iwt_ref[...] = _dot_nt(wiwt_ref[...], xn)


def _swap_halves(w):
    lead = w.shape[:-1]
    w4 = w.reshape(*lead, -1, 2, HEAD_DIM // 2)
    return w4[..., ::-1, :].reshape(*lead, -1)


def _even_proj_call(h, pos, g, w_in, a_q_g, a_k_g, b_q_g, b_k_g):
    n, d = h.shape
    tm = min(TM_PROJ, n)
    hd = HEAD_DIM
    sizes = (A_HEADS * hd, hd, A_LAT, IDX_HEADS * hd, hd, IDX_HEADS,
             B_HEADS * 2 * hd, B_HEADS * 2 * hd, B_HEADS * 2 * hd)
    offs = [0]
    for s in sizes:
        offs.append(offs[-1] + s)
    w_aq, w_ak, w_av, w_iq, w_ik, w_iw, w_bq, w_bk, w_bv = (
        w_in[:, offs[i]:offs[i + 1]] for i in range(9))
    dup = lambda w: jnp.concatenate([w, w], axis=1)
    wnr = jnp.concatenate([w_aq, dup(w_ak), w_bq, w_bk], axis=1).astype(BF16)
    wr = jnp.concatenate([w_iq, dup(w_ik)], axis=1).astype(BF16)
    wp = jnp.concatenate([w_av, w_bv], axis=1).astype(BF16)
    wiwt = jnp.pad(w_iw.T, ((0, IW_ROWS - IDX_HEADS), (0, 0))).astype(BF16)

    scale = hd ** -0.5 * math.log2(math.e)
    gn =jnp.concatenate([jnp.tile(a_q_g, A_HEADS) * scale, jnp.tile(a_k_g, 2),
                          jnp.tile(b_q_g, 2 * B_HEADS) * scale, jnp.tile(b_k_g, 2 * B_HEADS)])
    gnr = _swap_halves(gn)
    gn = gn.reshape(1, -1).astype(F32)
    gnr = gnr.reshape(1, -1).astype(F32)

    inv = ROPE_THETA ** (-jnp.arange(0, hd, 2, dtype=F32) / hd)
    inv128 = jnp.tile(inv, LANES // (hd // 2)).reshape(1, LANES)
    sgn = jnp.tile(jnp.concatenate([-jnp.ones(hd // 2, F32), jnp.ones(hd // 2, F32)]),
                   LANES // hd).reshape(1, LANES)
    head_of_lane = jnp.arange(LANES) // hd
    gmat = ((head_of_lane[:, None] == head_of_lane[None, :]).astype(F32) / hd).astype(BF16)

    tok = lambda i: (i, 0)
    out_widths = (A_HEADS * hd, 2 * hd, B_HEADS * 2 * hd, B_HEADS * 2 * hd,
                  IDX_HEADS * hd, 2 * hd, A_LAT, B_HEADS * 2 * hd)
    consts = (g, inv128, sgn, wnr, wr, wp, wiwt, gn, gnr, gmat)
    return pl.pallas_call(
        _even_proj_kernel,
        grid=(n // tm,),
        in_specs=[pl.BlockSpec((tm, d), tok), pl.BlockSpec((tm, 1), tok)]
                 + [_const_spec(c.shape) for c in consts],
        out_specs=[pl.BlockSpec((tm, w), tok) for w in out_widths]
                  + [pl.BlockSpec((IW_ROWS, tm), lambda i: (0, i))],
        out_shape=[jax.ShapeDtypeStruct((n, w), BF16) for w in out_widths]
                  + [jax.ShapeDtypeStruct((IW_ROWS, n), F32)],
        compiler_params=_params("parallel"),
        name="even_proj",
    )(h, pos.reshape(n, 1), *consts)


def _bit_planes(words):
    a = list(words)
    j, m = WORD_BITS // 2, 0x0000FFFF
    while j:
        k = 0
        while k < WORD_BITS:
            t = (a[k] ^ lax.shift_right_logical(a[k + j], j)) & m
            a[k] = a[k] ^ t
            a[k + j] = a[k + j] ^ lax.shift_left(t, j)
            k = (k + j + 1) & ~j
        j >>= 1
        m ^= m << j
    return a


def _stack_heads(x, n_heads):
    lane = lax.broadcasted_iota(I32, (x.shape[0], LANES), 1)
    parts = []
    for hh in range(n_heads):
        slab = x[:, (hh // 2) * LANES:(hh // 2 + 1) * LANES]
        keep = (lane < HEAD_DIM) if hh % 2 == 0 else (lane >= HEAD_DIM)
        parts.append(jnp.where(keep, slab, jnp.zeros_like(slab)))
    return jnp.concatenate(parts, axis=0)


def _dsa_kernel(shift_ref, aq_ref, iq_ref, iwt_ref, k_ref, v_ref, ik_ref, tri_ref, wuv_ref, o_ref,
                key_sc, bias_sc, plane_sc, alive_sc, *, top_k, bounded):
    tq = aq_ref.shape[0]
    tk = k_ref.shape[1]
    q0 = pl.program_id(1) * tq
    n_kb = (q0 + tq + tk - 1) // tk

    iq_all = _stack_heads(iq_ref[...], IDX_HEADS)
    iwt = iwt_ref[...]
    qchunk = (q0 + lax.broadcasted_iota(I32, (tk, tq), 1)) >> CHUNK_SHIFT
    krow = lax.broadcasted_iota(I32, (tk, tq), 0)
    groups = tk // (WORD_BITS * SUBLANES)
    n_blocks = key_sc.shape[0]

    def score_block(kb, masked):
        dots = jnp.maximum(_dot_nt(ik_ref[kb], iq_all), 0.0)
        sc = dots[:, 0:tq] * iwt[0:1, :]
        for hh in range(1, IDX_HEADS):
            sc = sc + dots[:, hh * tq:(hh + 1) * tq] * iwt[hh:hh + 1, :]
        bits = pltpu.bitcast(sc, I32)
        key = jnp.where(bits < 0, INT_MIN - bits, bits)
        if masked:
            key = jnp.where(((kb * tk + krow) >> CHUNK_SHIFT) <= qchunk, key, INT_MIN)
        key_sc[kb] = key
        for g in range(groups):
            base = g * WORD_BITS * SUBLANES
            planes = _bit_planes([key[base + SUBLANES * j:base + SUBLANES * (j + 1)] for j in range(WORD_BITS)])
            planes[0] = ~planes[0]
            for p in range(WORD_BITS):
                plane_sc[kb, p, SUBLANES * g:SUBLANES * (g + 1), :] = planes[p]
        alive_sc[kb] = jnp.full((SUBLANES * groups, tq), -1, I32)

    def score_step(kb, carry):
        score_block(kb, False)
        return carry

    _unrolled_loop(n_kb - 1, score_step, 0)
    score_block(n_kb - 1, True)

    def clear_body(kb, carry):
        plane_sc[kb] = jnp.zeros(plane_sc.shape[1:], I32)
        alive_sc[kb] = jnp.zeros(alive_sc.shape[1:], I32)
        return carry

    lax.fori_loop(n_kb, n_blocks, clear_body, 0)

    def radix_body(b, carry):
        t_u, k_rem, keep = carry
        first = jnp.where(b == 0, jnp.int32(-1), jnp.int32(0))
        b_prev = jnp.maximum(b - 1, 0)

        acc = jnp.zeros((SUBLANES * groups, tq), I32)
        for kb in range(n_blocks):
            a = alive_sc[kb] & ((plane_sc[kb, b_prev] ^ keep) | first)
            alive_sc[kb] = a
            acc = acc + lax.population_count(a & plane_sc[kb, b])
        cnt = jnp.sum(acc.astype(F32), axis=0, keepdims=True)
        take = cnt >= k_rem
        t_u = t_u | jnp.where(take, lax.shift_left(jnp.int32(1), 31 - b), 0)
        return t_u, jnp.where(take, k_rem, k_rem - cnt), jnp.where(take, 0, -1)

    init = (jnp.zeros((1, tq), I32), jnp.full((1, tq), float(top_k), F32), jnp.zeros((1, tq), I32))
    t_u, k_rem, keep = lax.fori_loop(0, WORD_BITS, radix_body, init)

    n_eq = jnp.zeros((SUBLANES * groups, tq), I32)
    for kb in range(n_blocks):
        n_eq = n_eq + lax.population_count(alive_sc[kb] & (plane_sc[kb, WORD_BITS - 1] ^ keep))
    n_eq = jnp.sum(n_eq.astype(F32), axis=0, keepdims=True)
    t = t_u ^ INT_MIN
    few = t == INT_MIN
    need = jnp.where(few, 0.0, k_rem)
    n_tied = jnp.sum(jnp.where(few | (n_eq == need), 0.0, 1.0))

    sel_bias = -shift_ref[0]

    def plain_bias():
        t_min = jnp.where(few, INT_MIN + 1, t)

        def body(kb, carry):
            bias_sc[kb] = jnp.where(key_sc[kb] >= t_min, sel_bias, NEG_BIAS).astype(BF16)
            return carry

        lax.fori_loop(0, n_kb, body, 0)

    def ranked_bias():
        tri = tri_ref[...]

        def body(kb, carry):
            kk_all = key_sc[kb]
            for c in range(tk // TIE_BLOCK):
                kk = kk_all[c * TIE_BLOCK:(c + 1) * TIE_BLOCK]
                eq = kk == t
                rank = _dot(tri, jnp.where(eq, 1.0, 0.0).astype(BF16)) + carry
                tie_bias = jnp.where(rank <= need, sel_bias, NEG_BIAS)
                bias = jnp.where(kk > t, sel_bias, jnp.where(eq, tie_bias, NEG_BIAS))
                bias_sc[kb, c * TIE_BLOCK:(c + 1) * TIE_BLOCK, :] = bias.astype(BF16)
                carry = rank[TIE_BLOCK - 1:TIE_BLOCK, :]
            return carry

        lax.fori_loop(0, n_kb, body, jnp.zeros((1, tq), F32))

    lax.cond(n_tied == 0.0, plain_bias, ranked_bias)

    rows = A_HEADS * tq
    eye = lax.broadcasted_iota(I32, (tq, tq), 0) == lax.broadcasted_iota(I32, (tq, tq), 1)
    eye = jnp.where(eye, 1.0, 0.0).astype(BF16)
    qe = jnp.concatenate([_stack_heads(aq_ref[...], A_HEADS),
                          jnp.concatenate([eye] * A_HEADS, axis=0)], axis=1)
    ones = jnp.ones((tk, LANES), BF16)

    def logits(kb):
        return _dot_nt(qe, jnp.concatenate([k_ref[kb], bias_sc[kb]], axis=1))

    def values(kb):
        return jnp.concatenate([v_ref[kb], ones], axis=1)

    acc0 = jnp.zeros((rows, A_LAT + LANES), F32)
    if bounded:
        def att_group(kbs, acc):
            p = jnp.concatenate([jnp.exp2(logits(kb)).astype(BF16) for kb in kbs], axis=1)
            return acc + _dot(p, jnp.concatenate([values(kb) for kb in kbs], axis=0))

        acc = _grouped_loop(n_kb, att_group, acc0)
    else:
        def softmax_step(kb, m, acc, s):
            m_new = jnp.maximum(m, jnp.max(s, axis=1, keepdims=True))
            p = jnp.exp2(s - m_new).astype(BF16)
            return m_new, jnp.exp2(m - m_new) * acc + _dot(p, values(kb))

        def att_body(kb, carry):
            m, acc, s = carry
            s_next = logits(kb + 1)
            m, acc = softmax_step(kb, m, acc, s)
            return m, acc, s_next

        m, acc, s = lax.fori_loop(0, n_kb - 1, att_body, (jnp.full((rows, 1), M_INIT, F32), acc0, logits(0)))
        _, acc = softmax_step(n_kb - 1, m, acc, s)
    o_lat = (acc[:, :A_LAT] / acc[:, A_LAT:]).astype(BF16)
    o_all = jnp.concatenate([o_lat[hh * tq:(hh + 1) * tq] for hh in range(A_HEADS)], axis=1)
    o_ref[...] = _dot(o_all, wuv_ref[...]).astype(o_ref.dtype)


def _dsa_call(shift, aq, iq, iwt, akd, av, ikd, w_uv, *, bounded):
    b, s, _ = aq.shape
    tq = min(TQ_DSA, s)
    tk = min(TK_DSA, s)
    assert tq == LANES, "the one-hot mask columns pair one query tile with 128 contraction lanes"
    top_k = min(TOPK_MAX, s // 4)
    n_kb = s // tk
    nq = s // tq
    blk = lambda x: x.reshape(b, n_kb, tk, x.shape[-1])
    tri = (jnp.arange(TIE_BLOCK)[:, None] >= jnp.arange(TIE_BLOCK)[None, :]).astype(BF16)
    eye = jnp.eye(A_HEADS, dtype=w_uv.dtype)
    wuv_bd = (w_uv[:, :, None, :] * eye[:, None, :, None]).reshape(A_HEADS * A_LAT, A_HEADS * HEAD_DIM)
    wuv_bd = wuv_bd.astype(BF16)

    qmap = lambda bi, i: (bi, i, 0)
    kvmap = lambda bi, i: (bi, 0, 0, 0)
    kv_spec = pl.BlockSpec((None, n_kb, tk, LANES), kvmap)
    return pl.pallas_call(
        functools.partial(_dsa_kernel, top_k=top_k, bounded=bounded),
        grid=(b, nq),
        in_specs=[pl.BlockSpec(memory_space=pltpu.SMEM),
                  pl.BlockSpec((None, tq, aq.shape[-1]), qmap),
                  pl.BlockSpec((None, tq, iq.shape[-1]), qmap),
                  pl.BlockSpec((iwt.shape[0], tq), lambda bi, i: (0, bi * nq + i)),
                  kv_spec, kv_spec, kv_spec,
                  _const_spec(tri.shape), _const_spec(wuv_bd.shape)],
        out_specs=pl.BlockSpec((None, tq, A_HEADS * HEAD_DIM), qmap),
        out_shape=jax.ShapeDtypeStruct((b, s, A_HEADS * HEAD_DIM), BF16),
        scratch_shapes=[pltpu.VMEM((n_kb, tk, tq), I32), pltpu.VMEM((n_kb, tk, tq), BF16),
                        pltpu.VMEM((n_kb, WORD_BITS, tk // WORD_BITS, tq), I32),
                        pltpu.VMEM((n_kb, tk // WORD_BITS, tq), I32)],
        compiler_params=_params("parallel", "arbitrary"),
        name="dsa",
    )(shift, aq, iq, iwt, blk(akd), blk(av), blk(ikd), tri, wuv_bd)


def _diff_kernel(shift_ref, lam_ref, q_ref, k_ref, v_ref, g_ref, o_ref, *, lam_init, bounded):
    t = q_ref.shape[0]
    i = pl.program_id(2)
    lam_v = lam_ref[...]
    lam = (jnp.exp(jnp.sum(lam_v[0:1] * lam_v[1:2], axis=1, keepdims=True))
           - jnp.exp(jnp.sum(lam_v[2:3] * lam_v[3:4], axis=1, keepdims=True)) + lam_init)

    q = q_ref[...]
    lane = lax.broadcasted_iota(I32, q.shape, 1)
    zero = jnp.zeros_like(q)
    q2 = jnp.concatenate([jnp.where(lane < HEAD_DIM, q, zero),
                          jnp.where(lane >= HEAD_DIM, q, zero)], axis=0)

    vdim = v_ref.shape[-1]
    ones = jnp.ones((t, LANES), BF16)

    def values(j):
        return jnp.concatenate([v_ref[j], ones], axis=1)

    acc0 = jnp.zeros((2 * t, vdim + LANES), F32)

    if bounded:
        assert t // CHUNK < LANES
        q_lane = lax.broadcasted_iota(I32, (2 * t, LANES), 1)
        q_chunk = lax.broadcasted_iota(I32, (2 * t, LANES), 0)
        q_chunk = jnp.where(q_chunk >= t, q_chunk - t, q_chunk) >> CHUNK_SHIFT
        q_extra = jnp.where(q_lane == 0, 1.0, jnp.where(q_lane - 1 > q_chunk, NEG_BIAS, 0.0))
        q2 = jnp.concatenate([q2, q_extra.astype(BF16)], axis=1)
        k_lane = lax.broadcasted_iota(I32, (t, LANES), 1)
        k_chunk = lax.broadcasted_iota(I32, (t, LANES), 0) >> CHUNK_SHIFT
        k_plain = jnp.where(k_lane == 0, -shift_ref[0], 0.0)
        k_diag = jnp.where(k_lane - 1 == k_chunk, 1.0, k_plain).astype(BF16)
        k_plain = k_plain.astype(BF16)

        def probs(j):
            k_extra = jnp.where(j == i, k_diag, k_plain)
            return jnp.exp2(_dot_nt(q2, jnp.concatenate([k_ref[j], k_extra], axis=1))).astype(BF16)

        def group(js, acc):
            p = jnp.concatenate([probs(j) for j in js], axis=1)
            return acc + _dot(p, jnp.concatenate([values(j) for j in js], axis=0))

        acc = _grouped_loop(i + 1, group, acc0)
    else:
        qc = lax.broadcasted_iota(I32, (2 * t, t), 0)
        qc = jnp.where(qc >= t, qc - t, qc) >> CHUNK_SHIFT
        kc = lax.broadcasted_iota(I32, (2 * t, t), 1) >> CHUNK_SHIFT

        def logits(j):
            return _dot_nt(q2, k_ref[j])

        def softmax_step(j, m, acc, s):
            m_new = jnp.maximum(m, jnp.max(s, axis=1, keepdims=True))
            p = jnp.exp2(s - m_new).astype(BF16)
            return m_new, jnp.exp2(m - m_new) * acc + _dot(p, values(j))

        def body(j, carry):
            m, acc, s = carry
            s_next = logits(j + 1)
            m, acc = softmax_step(j, m, acc, s)
            return m, acc, s_next

        m, acc, s = lax.fori_loop(0, i, body, (jnp.full((2 * t, 1), M_INIT, F32), acc0, logits(0)))
        _, acc = softmax_step(i, m, acc, jnp.where(kc <= qc, s, NEG_BIAS))
    o = acc[:, :vdim] / acc[:, vdim:]
    o = o[:t] - lam * o[t:]
    o_ref[...] = (_rms(o, g_ref[...]) * (1.0 - lam_init)).astype(o_ref.dtype)


def _diff_call(shift, bq, bk, bv, lam_vecs, subln_g, lam_init, *, bounded):
    b, s, w = bq.shape
    t = min(T_DIFF, s)
    nb = s // t
    k4 = bk.reshape(b, nb, t, w)
    v4 = bv.reshape(b, nb, t, w)
    qmap = lambda bi, hh, i: (bi, i, hh)
    kvmap = lambda bi, hh, i: (bi, 0, 0, hh)
    return pl.pallas_call(
        functools.partial(_diff_kernel, lam_init=lam_init, bounded=bounded),
        grid=(b, B_HEADS, nb),
        in_specs=[pl.BlockSpec(memory_space=pltpu.SMEM),
                  _const_spec(lam_vecs.shape),
                  pl.BlockSpec((None, t, LANES), qmap),
                  pl.BlockSpec((None, nb, t, LANES), kvmap),
                  pl.BlockSpec((None, nb, t, LANES), kvmap),
                  _const_spec(subln_g.shape)],
        out_specs=pl.BlockSpec((None, t, LANES), qmap),
        out_shape=jax.ShapeDtypeStruct((b, s, w), BF16),
        compiler_params=_params("parallel", "parallel", "arbitrary"),
        name="diff_attn",
    )(shift, lam_vecs, bq, k4, v4, subln_g)


def _odd_kernel(h_ref, g_ref, wc_ref, wd_ref, clg_ref, clb_ref, ws_ref, sb_ref,
                cw_ref, cb_ref, dlg_ref, dlb_ref, woc_ref, wod_ref, o_ref, buf_sc, *, tiles_per_seq):
    tm = h_ref.shape[0]
    cwid = clg_ref.shape[1]

    h = h_ref[...]
    xn = _rms(h, g_ref[...]).astype(BF16)

    zc = jax.nn.gelu(_dot(xn, wc_ref[...]), approximate=True)
    u = zc[:, :cwid]
    v = _layer_norm(zc[:, cwid:], clg_ref[...], clb_ref[...]).astype(BF16)
    pi = lax.broadcasted_iota(I32, (SGU_BLOCK, SGU_BLOCK), 0) >> CHUNK_SHIFT
    pj = lax.broadcasted_iota(I32, (SGU_BLOCK, SGU_BLOCK), 1) >> CHUNK_SHIFT
    causal = pi >= pj
    w_s = [jnp.where(causal, ws_ref[gi], jnp.zeros((SGU_BLOCK, SGU_BLOCK), BF16)) for gi in range(C_GROUPS)]
    lane = lax.broadcasted_iota(I32, (SGU_BLOCK, LANES), 1)
    sgb = sb_ref[...]
    rows_out = []
    for r in range(tm // SGU_BLOCK):
        rs = slice(r * SGU_BLOCK, (r + 1) * SGU_BLOCK)
        slabs = []
        for pr in range(cwid // LANES):
            vs = v[rs, pr * LANES:(pr + 1) * LANES]
            lo = _dot(w_s[2 * pr], vs)
            hi = _dot(w_s[2 * pr + 1], vs)
            slabs.append(jnp.where(lane < HEAD_DIM, lo, hi))
        rows_out.append(jnp.concatenate(slabs, axis=1) + sgb)
    c_out = (u * jnp.concatenate(rows_out, axis=0)).astype(BF16)

    zd = _dot(xn, wd_ref[...])
    dwid = zd.shape[1] // 2
    hd = zd[:, :dwid] * jax.nn.sigmoid(zd[:, dwid:])

    @pl.when(pl.program_id(0) % tiles_per_seq == 0)
    def _():
        buf_sc[0:CONV_HALO, :] = jnp.zeros((CONV_HALO, dwid), F32)

    buf_sc[CONV_HALO:CONV_HALO + tm, :] = hd
    buf_sc[CONV_HALO + tm:, :] = jnp.zeros((SUBLANES, dwid), F32)
    conv = jnp.zeros((tm, dwid), F32) + cb_ref[...]
    base = CONV_HALO - (CONV_W - 1)
    for r in range(SUBLANES):
        group = None
        for w in range(CONV_W):
            if (base + w) % SUBLANES == r:
                a = (base + w) // SUBLANES * SUBLANES
                term = buf_sc[a:a + tm + SUBLANES, :] * cw_ref[w:w + 1, :]
                group = term if group is None else group + term
        conv = conv + group[r:r + tm]
    buf_sc[0:CONV_HALO, :] = buf_sc[tm:tm + CONV_HALO, :]
    d_out = _layer_norm(conv, dlg_ref[...], dlb_ref[...])
    d_out = (d_out * jax.nn.sigmoid(d_out)).astype(BF16)

    o_ref[...] = h + _dot(c_out, woc_ref[...]) + _dot(d_out, wod_ref[...])


def _odd_call(h, seq_len, g, w_in, w_out, c_ln_g, c_ln_b, c_w_s, c_b_s, d_conv_w, d_conv_b, d_ln_g, d_ln_b):
    n, d = h.shape
    tm = min(TM_ODD, seq_len)
    cwid = c_ln_g.shape[0]
    dwid = d_ln_g.shape[0]
    row = lambda x: x.reshape(1, -1).astype(F32)
    wc = w_in[:, :2 * cwid].astype(BF16)
    wd = w_in[:, 2 * cwid:].astype(BF16)
    sgb = jnp.repeat(c_b_s.T, cwid // C_GROUPS, axis=1).astype(F32)
    consts = (row(g), wc, wd, row(c_ln_g), row(c_ln_b), c_w_s.astype(BF16), sgb,
              d_conv_w.astype(F32), row(d_conv_b), row(d_ln_g), row(d_ln_b),
              w_out[:cwid].astype(BF16), w_out[cwid:].astype(BF16))
    tok = lambda i: (i, 0)
    return pl.pallas_call(
        functools.partial(_odd_kernel, tiles_per_seq=seq_len // tm),
        grid=(n // tm,),
        in_specs=[pl.BlockSpec((tm, d), tok)] + [_const_spec(c.shape) for c in consts],
        out_specs=pl.BlockSpec((tm, d), tok),
        out_shape=jax.ShapeDtypeStruct((n, d), F32),
        scratch_shapes=[pltpu.VMEM((tm + CONV_HALO + SUBLANES, dwid), F32)],
        compiler_params=_params("arbitrary"),
        name="odd_mixer",
    )(h, *consts)


def _logit_bound(gq, gk):
    unit = HEAD_DIM * HEAD_DIM ** -0.5 * math.log2(math.e) * BF16_SLACK
    return (unit * jnp.max(jnp.abs(gq)) * jnp.max(jnp.abs(gk))).astype(F32).reshape(1)


def _bounded_or_online(bound, call, *args):
    return lax.cond(bound[0] <= MAX_SHIFT,
                    lambda: call(bound, *args, bounded=True),
                    lambda: call(jnp.zeros_like(bound), *args, bounded=False))
def kernel(x, p, pos, ffn1_g, ffn1_wg, ffn1_wu, ffn1_wd, mix_g, ffn2_g, ffn2_wg, ffn2_wu, ffn2_wd, ple_g, ple_wgate, ple_wproj, ev_w_in, ev_w_out, a_q_g, a_k_g, a_w_uv, b_q_g, b_k_g, b_lam_q1, b_lam_k1, b_lam_q2, b_lam_k2, b_subln_g, od_w_in, od_w_out, c_ln_g, c_ln_b, c_w_s, c_b_s, d_conv_w, d_conv_b, d_ln_g, d_ln_b):
    bsz, s_len, d = x.shape
    n = bsz * s_len
    depth = p.shape[0]
    row = lambda v: v.reshape(1, -1).astype(F32)
    h = x.reshape(n, d)
    for layer in range(depth):
        ffn1 = (row(ffn1_g[layer]), ffn1_wg[layer].astype(BF16), ffn1_wu[layer].astype(BF16),
                ffn1_wd[layer].astype(BF16))
        ffn2 = (row(ffn2_g[layer]), ffn2_wg[layer].astype(BF16), ffn2_wu[layer].astype(BF16),
                ffn2_wd[layer].astype(BF16))
        ple = (p[layer].reshape(n, -1), row(ple_g[layer]), ple_wgate[layer].astype(BF16),
               ple_wproj[layer].astype(BF16))
        h = _ffn_call(h, *ffn1)
        if layer % 2 == 0:
            e = layer // 2
            lam_init = 0.8 - 0.6 * math.exp(-0.3 * layer)
            aq, akd, bq, bk, iq, ikd, av, bv, iwt = _even_proj_call(
                h, pos, row(mix_g[layer]), ev_w_in[e], a_q_g[e], a_k_g[e], b_q_g[e], b_k_g[e])
            r3 = lambda a: a.reshape(bsz, s_len, a.shape[-1])
            a_out = _bounded_or_online(
                _logit_bound(a_q_g[e], a_k_g[e]), _dsa_call,
                r3(aq), r3(iq), iwt, r3(akd), r3(av), r3(ikd), a_w_uv[e])
            lam_vecs = jnp.stack([b_lam_q1[e], b_lam_k1[e], b_lam_q2[e], b_lam_k2[e]]).astype(F32)
            b_out = _bounded_or_online(
                _logit_bound(b_q_g[e], b_k_g[e]), _diff_call,
                r3(bq), r3(bk), r3(bv), lam_vecs, row(b_subln_g[e]), lam_init)
            a_w = A_HEADS * HEAD_DIM
            mix = (a_out.reshape(n, -1), b_out.reshape(n, -1),
                   ev_w_out[e][:a_w].astype(BF16), ev_w_out[e][a_w:].astype(BF16))
            h = _ffn_call(h, *ffn2, mix=mix, ple=ple)
        else:
            o = layer // 2
            h = _odd_call(h, s_len, mix_g[layer], od_w_in[o], od_w_out[o], c_ln_g[o], c_ln_b[o],
                          c_w_s[o], c_b_s[o], d_conv_w[o], d_conv_b[o], d_ln_g[o], d_ln_b[o])
            h = _ffn_call(h, *ffn2, ple=ple)
    return h.reshape(bsz, s_len, d)
```

```python
import functools
import math

import jax
import jax.numpy as jnp
from jax import lax
from jax.experimental import pallas as pl
from jax.experimental.pallas import tpu as pltpu

F32 = jnp.float32
BF16 = jnp.bfloat16
I32 = jnp.int32

EPS = 1e-6
HEAD_DIM = 64
CHUNK = 64
CHUNK_SHIFT = 6
ROPE_THETA = 10000.0
A_HEADS = 8
A_LAT = 128
IDX_HEADS = 4
TOPK_MAX = 256
B_HEADS = 4
C_GROUPS = 8
SGU_BLOCK = 128
CONV_W = 31
CONV_HALO = 32

LANES = 128
SUBLANES = 8
MXU_WIDTH = 256
INT_MIN = -(2 ** 31)
NEG_BIAS = -1e30
M_INIT = -1e29
MAX_SHIFT = 60.0
BF16_SLACK = 1.02

FF_CHUNK = 256
TM_FFN = 512
TM_PROJ = 512
TM_ODD = 256
TQ_DSA = 128
TK_DSA = 512
T_DIFF = 512
TIE_BLOCK = 256
WORD_BITS = 32
ATT_UNROLL = 4

VMEM_LIMIT_BYTES = 56 * 1024 * 1024


def _params(*sem):
    return pltpu.CompilerParams(dimension_semantics=sem, vmem_limit_bytes=VMEM_LIMIT_BYTES)


def _const_spec(shape):
    zeros = (0,) * len(shape)
    return pl.BlockSpec(shape, lambda *_: zeros)


def _rms(x, g):
    ms = jnp.mean(x * x, axis=-1, keepdims=True)
    return x * lax.rsqrt(ms + EPS) * g


def _layer_norm(x, g, b):
    mu = jnp.mean(x, axis=-1, keepdims=True)
    xc = x - mu
    var = jnp.mean(xc * xc, axis=-1, keepdims=True)
    return xc * lax.rsqrt(var + EPS) * g + b


def _grouped_loop(n, group_step, carry):
    done = 0
    unroll = ATT_UNROLL
    while unroll >= 1:
        def group(g, c, unroll=unroll, done=done):
            first = done + g * unroll
            return group_step([first + u for u in range(unroll)], c)

        n_groups = lax.shift_right_logical(n - done, unroll.bit_length() - 1)
        carry = lax.fori_loop(0, n_groups, group, carry)
        done = done + n_groups * unroll
        unroll //= 2
    return carry


def _unrolled_loop(n, step, carry):
    def group_step(ks, c):
        for k in ks:
            c = step(k, c)
        return c

    return _grouped_loop(n, group_step, carry)


def _dot(a, b):
    return jnp.dot(a, b, preferred_element_type=F32)


def _dot_nt(a, b):
    return lax.dot_general(a, b, (((1,), (1,)), ((), ())), preferred_element_type=F32)


def _ffn_kernel(*refs, has_mix, has_ple, n_chunks):
    refs = list(refs)
    h_ref = refs.pop(0)
    if has_mix:
        a_ref, b_ref, woa_ref, wob_ref = refs[:4]
        refs = refs[4:]
    g_ref, wg_ref, wu_ref, wd_ref = refs[:4]
    refs = refs[4:]
    if has_ple:
        p_ref, pg_ref, wgate_ref, wproj_ref = refs[:4]
        refs = refs[4:]
    (o_ref,) = refs

    h = h_ref[...]
    if has_mix:
        h = h + _dot(a_ref[...], woa_ref[...]) + _dot(b_ref[...], wob_ref[...])
    xn = _rms(h, g_ref[...]).astype(BF16)
    acc = jnp.zeros(h.shape, F32)
    for c in range(n_chunks):
        gt = _dot(xn, wg_ref[:, c * FF_CHUNK:(c + 1) * FF_CHUNK])
        up = _dot(xn, wu_ref[:, c * FF_CHUNK:(c + 1) * FF_CHUNK])
        act = (gt * jax.nn.sigmoid(gt) * up).astype(BF16)
        acc = acc + _dot(act, wd_ref[c * FF_CHUNK:(c + 1) * FF_CHUNK, :])
    h = h + 0.5 * acc
    if has_ple:
        xg = _rms(h, pg_ref[...]).astype(BF16)
        gate = jax.nn.sigmoid(_dot(xg, wgate_ref[...]))
        h = h + gate * _dot(p_ref[...].astype(BF16), wproj_ref[...])
    o_ref[...] = h


def _ffn_call(h, g, wg, wu, wd, mix=None, ple=None):
    n, d = h.shape
    tm = min(TM_FFN, n)
    d_ff = wd.shape[0]
    n_chunks = d_ff // FF_CHUNK
    tok = lambda i: (i, 0)
    args = [h]
    specs = [pl.BlockSpec((tm, d), tok)]
    if mix is not None:
        a, b, woa, wob = mix
        args += [a, b, woa, wob]
        specs += [pl.BlockSpec((tm, a.shape[1]), tok), pl.BlockSpec((tm, b.shape[1]), tok),
                  _const_spec(woa.shape), _const_spec(wob.shape)]
    args += [g, wg, wu, wd]
    specs += [_const_spec(g.shape), _const_spec(wg.shape), _const_spec(wu.shape), _const_spec(wd.shape)]
    if ple is not None:
        p, pg, wgate, wproj = ple
        args += [p, pg, wgate, wproj]
        specs += [pl.BlockSpec((tm, p.shape[1]), tok), _const_spec(pg.shape),
                  _const_spec(wgate.shape), _const_spec(wproj.shape)]
    kern = functools.partial(_ffn_kernel, has_mix=mix is not None, has_ple=ple is not None,
                             n_chunks=n_chunks)
    return pl.pallas_call(
        kern,
        grid=(n // tm,),
        in_specs=specs,
        out_specs=pl.BlockSpec((tm, d), tok),
        out_shape=jax.ShapeDtypeStruct((n, d), F32),
        compiler_params=_params("parallel"),
        name="ffn",
    )(*args)


_NR_WIDTHS = (A_HEADS * HEAD_DIM, 2 * HEAD_DIM, B_HEADS * 2 * HEAD_DIM, B_HEADS * 2 * HEAD_DIM)
_R_WIDTHS = (IDX_HEADS * HEAD_DIM, 2 * HEAD_DIM)
_P_WIDTHS = (A_LAT, B_HEADS * 2 * HEAD_DIM)
IW_ROWS = 8


def _even_proj_kernel(h_ref, pos_ref, g_ref, inv_ref, sgn_ref, wnr_ref, wr_ref, wp_ref, wiwt_ref,
                      gn_ref, gnr_ref, gmat_ref,
                      aq_ref, akd_ref, bq_ref, bk_ref, iq_ref, ikd_ref, av_ref, bv_ref, iwt_ref):
    xn = _rms(h_ref[...], g_ref[...]).astype(BF16)
    ang = pos_ref[...].astype(F32) * inv_ref[...]
    cos = jnp.cos(ang)
    sin = jnp.sin(ang) * sgn_ref[...]
    gmat = gmat_ref[...]
    first_half = lax.broadcasted_iota(I32, cos.shape, 1) % HEAD_DIM < HEAD_DIM // 2

    def swap_halves(xs):
        return jnp.where(first_half, pltpu.roll(xs, LANES - HEAD_DIM // 2, 1), pltpu.roll(xs, HEAD_DIM // 2, 1))

    off = 0
    for width, out_ref in zip(_NR_WIDTHS, (aq_ref, akd_ref, bq_ref, bk_ref)):
        x = _dot(xn, wnr_ref[:, off:off + width])
        mw = min(width, MXU_WIDTH)
        ms = jnp.concatenate(
            [_dot(jnp.square(x[:, c:c + mw]).astype(BF16), gmat[:mw, :mw]) for c in range(0, width, mw)], axis=1)
        for s in range(width // LANES):
            sl = slice(s * LANES, (s + 1) * LANES)
            gsl = slice(off + s * LANES, off + (s + 1) * LANES)
            xs = x[:, sl]
            y = (xs * (gn_ref[:, gsl] * cos) + swap_halves(xs) * (gnr_ref[:, gsl] * sin)) * lax.rsqrt(ms[:, sl] + EPS)
            out_ref[:, sl] = y.astype(out_ref.dtype)
        off += width

    off = 0
    for width, out_ref in zip(_R_WIDTHS, (iq_ref, ikd_ref)):
        x = _dot(xn, wr_ref[:, off:off + width])
        for s in range(width // LANES):
            xs = x[:, s * LANES:(s + 1) * LANES]
            out_ref[:, s * LANES:(s + 1) * LANES] = (xs * cos + swap_halves(xs) * sin).astype(out_ref.dtype)
        off += width

    off = 0
    for width, out_ref in zip(_P_WIDTHS, (av_ref, bv_ref)):
        out_ref[...] = _dot(xn, wp_ref[:, off:off + width]).astype(out_ref.dtype)
        off += width
    iwt_ref[...] = _dot_nt(wiwt_ref[...], xn)


def _swap_halves(w):
    lead = w.shape[:-1]
    w4 = w.reshape(*lead, -1, 2, HEAD_DIM // 2)
    return w4[..., ::-1, :].reshape(*lead, -1)


def _even_proj_call(h, pos, g, w_in, a_q_g, a_k_g, b_q_g, b_k_g):
    n, d = h.shape
    tm = min(TM_PROJ, n)
    hd = HEAD_DIM
    sizes = (A_HEADS * hd, hd, A_LAT, IDX_HEADS * hd, hd, IDX_HEADS,
             B_HEADS * 2 * hd, B_HEADS * 2 * hd, B_HEADS * 2 * hd)
    offs = [0]
    for s in sizes:
        offs.append(offs[-1] + s)
    w_aq, w_ak, w_av, w_iq, w_ik, w_iw, w_bq, w_bk, w_bv = (
        w_in[:, offs[i]:offs[i + 1]] for i in range(9))
    dup = lambda w: jnp.concatenate([w, w], axis=1)
    wnr = jnp.concatenate([w_aq, dup(w_ak), w_bq, w_bk], axis=1).astype(BF16)
    wr = jnp.concatenate([w_iq, dup(w_ik)], axis=1).astype(BF16)
    wp = jnp.concatenate([w_av, w_bv], axis=1).astype(BF16)
    wiwt = jnp.pad(w_iw.T, ((0, IW_ROWS - IDX_HEADS), (0, 0))).astype(BF16)

    scale = hd ** -0.5 * math.log2(math.e)
    gn =jnp.concatenate([jnp.tile(a_q_g, A_HEADS) * scale, jnp.tile(a_k_g, 2),
                          jnp.tile(b_q_g, 2 * B_HEADS) * scale, jnp.tile(b_k_g, 2 * B_HEADS)])
    gnr = _swap_halves(gn)
    gn = gn.reshape(1, -1).astype(F32)
    gnr = gnr.reshape(1, -1).astype(F32)

    inv = ROPE_THETA ** (-jnp.arange(0, hd, 2, dtype=F32) / hd)
    inv128 = jnp.tile(inv, LANES // (hd // 2)).reshape(1, LANES)
    sgn = jnp.tile(jnp.concatenate([-jnp.ones(hd // 2, F32), jnp.ones(hd // 2, F32)]),
                   LANES // hd).reshape(1, LANES)
    head_of_lane = jnp.arange(MXU_WIDTH) // hd
    gmat = ((head_of_lane[:, None] == head_of_lane[None, :]).astype(F32) / hd).astype(BF16)

    tok = lambda i: (i, 0)
    out_widths = (A_HEADS * hd, 2 * hd, B_HEADS * 2 * hd, B_HEADS * 2 * hd,
                  IDX_HEADS * hd, 2 * hd, A_LAT, B_HEADS * 2 * hd)
    consts = (g, inv128, sgn, wnr, wr, wp, wiwt, gn, gnr, gmat)
    return pl.pallas_call(
        _even_proj_kernel,
        grid=(n // tm,),
        in_specs=[pl.BlockSpec((tm, d), tok), pl.BlockSpec((tm, 1), tok)]
                 + [_const_spec(c.shape) for c in consts],
        out_specs=[pl.BlockSpec((tm, w), tok) for w in out_widths]
                  + [pl.BlockSpec((IW_ROWS, tm), lambda i: (0, i))],
        out_shape=[jax.ShapeDtypeStruct((n, w), BF16) for w in out_widths]
                  + [jax.ShapeDtypeStruct((IW_ROWS, n), F32)],
        compiler_params=_params("parallel"),
        name="even_proj",
    )(h, pos.reshape(n, 1), *consts)


def _bit_planes(words):
    a = list(words)
    j, m = WORD_BITS // 2, 0x0000FFFF
    while j:
        k = 0
        while k < WORD_BITS:
            t = (a[k] ^ lax.shift_right_logical(a[k + j], j)) & m
            a[k] = a[k] ^ t
            a[k + j] = a[k + j] ^ lax.shift_left(t, j)
            k = (k + j + 1) & ~j
        j >>= 1
        m ^= m << j
    return a


def _stack_heads(x, n_heads):
    lane = lax.broadcasted_iota(I32, (x.shape[0], LANES), 1)
    parts = []
    for hh in range(n_heads):
        slab = x[:, (hh // 2) * LANES:(hh // 2 + 1) * LANES]
        keep = (lane < HEAD_DIM) if hh % 2 == 0 else (lane >= HEAD_DIM)
        parts.append(jnp.where(keep, slab, jnp.zeros_like(slab)))
    return jnp.concatenate(parts, axis=0)


def _dsa_kernel(shift_ref, aq_ref, iq_ref, iwt_ref, k_ref, v_ref, ik_ref, tri_ref, wuv_ref, o_ref,
                key_sc, bias_sc, plane_sc, alive_sc, acc_sc, *, top_k, bounded):
    tq = aq_ref.shape[0]
    tk = k_ref.shape[1]
    q0 = pl.program_id(1) * tq
    n_kb = (q0 + tq + tk - 1) // tk

    iq_all = _stack_heads(iq_ref[...], IDX_HEADS)
    iwt = iwt_ref[...]
    qchunk = (q0 + lax.broadcasted_iota(I32, (tk, tq), 1)) >> CHUNK_SHIFT
    krow = lax.broadcasted_iota(I32, (tk, tq), 0)
    groups = tk // (WORD_BITS * SUBLANES)
    n_blocks = key_sc.shape[0]

    def score_block(kb, masked):
        dots = jnp.maximum(_dot_nt(ik_ref[kb], iq_all), 0.0)
        sc = dots[:, 0:tq] * iwt[0:1, :]
        for hh in range(1, IDX_HEADS):
            sc = sc + dots[:, hh * tq:(hh + 1) * tq] * iwt[hh:hh + 1, :]
        bits = pltpu.bitcast(sc, I32)
        key = jnp.where(bits < 0, INT_MIN - bits, bits)
        if masked:
            key = jnp.where(((kb * tk + krow) >> CHUNK_SHIFT) <= qchunk, key, INT_MIN)
        key_sc[kb] = key
        for g in range(groups):
            base = g * WORD_BITS * SUBLANES
            planes = _bit_planes([key[base + SUBLANES * j:base + SUBLANES * (j + 1)] for j in range(WORD_BITS)])
            planes[0] = ~planes[0]
            for p in range(WORD_BITS):
                plane_sc[kb, p, SUBLANES * g:SUBLANES * (g + 1), :] = planes[p]
        alive_sc[kb] = jnp.full((SUBLANES * groups, tq), -1, I32)

    def score_step(kb, carry):
        score_block(kb, False)
        return carry

    _unrolled_loop(n_kb - 1, score_step, 0)
    score_block(n_kb - 1, True)

    def clear_body(kb, carry):
        plane_sc[kb] = jnp.zeros(plane_sc.shape[1:], I32)
        alive_sc[kb] = jnp.zeros(alive_sc.shape[1:], I32)
        return carry

    lax.fori_loop(n_kb, n_blocks, clear_body, 0)

    def radix_body(b, carry):
        t_u, k_rem, keep = carry
        first = jnp.where(b == 0, jnp.int32(-1), jnp.int32(0))
        b_prev = jnp.maximum(b - 1, 0)

        acc = jnp.zeros((SUBLANES * groups, tq), I32)
        for kb in range(n_blocks):
            a = alive_sc[kb] & ((plane_sc[kb, b_prev] ^ keep) | first)
            alive_sc[kb] = a
            acc = acc + lax.population_count(a & plane_sc[kb, b])
        cnt = jnp.sum(acc.astype(F32), axis=0, keepdims=True)
        take = cnt >= k_rem
        t_u = t_u | jnp.where(take, lax.shift_left(jnp.int32(1), 31 - b), 0)
        return t_u, jnp.where(take, k_rem, k_rem - cnt), jnp.where(take, 0, -1)

    init = (jnp.zeros((1, tq), I32), jnp.full((1, tq), float(top_k), F32), jnp.zeros((1, tq), I32))
    t_u, k_rem, keep = lax.fori_loop(0, WORD_BITS, radix_body, init)

    n_eq = jnp.zeros((SUBLANES * groups, tq), I32)
    for kb in range(n_blocks):
        n_eq = n_eq + lax.population_count(alive_sc[kb] & (plane_sc[kb, WORD_BITS - 1] ^ keep))
    n_eq = jnp.sum(n_eq.astype(F32), axis=0, keepdims=True)
    t = t_u ^ INT_MIN
    few = t == INT_MIN
    need = jnp.where(few, 0.0, k_rem)
    n_tied = jnp.sum(jnp.where(few | (n_eq == need), 0.0, 1.0))

    sel_bias = -shift_ref[0]

    def plain_bias():
        t_min = jnp.where(few, INT_MIN + 1, t)

        def body(kb, carry):
            bias_sc[kb] = jnp.where(key_sc[kb] >= t_min, sel_bias, NEG_BIAS).astype(BF16)
            return carry

        lax.fori_loop(0, n_kb, body, 0)

    def ranked_bias():
        tri = tri_ref[...]

        def body(kb, carry):
            kk_all = key_sc[kb]
            for c in range(tk // TIE_BLOCK):
                kk = kk_all[c * TIE_BLOCK:(c + 1) * TIE_BLOCK]
                eq = kk == t
                rank = _dot(tri, jnp.where(eq, 1.0, 0.0).astype(BF16)) + carry
                tie_bias = jnp.where(rank <= need, sel_bias, NEG_BIAS)
                bias = jnp.where(kk > t, sel_bias, jnp.where(eq, tie_bias, NEG_BIAS))
                bias_sc[kb, c * TIE_BLOCK:(c + 1) * TIE_BLOCK, :] = bias.astype(BF16)
                carry = rank[TIE_BLOCK - 1:TIE_BLOCK, :]
            return carry

        lax.fori_loop(0, n_kb, body, jnp.zeros((1, tq), F32))

    lax.cond(n_tied == 0.0, plain_bias, ranked_bias)

    rows = A_HEADS * tq
    eye = lax.broadcasted_iota(I32, (tq, tq), 0) == lax.broadcasted_iota(I32, (tq, tq), 1)
    eye = jnp.where(eye, 1.0, 0.0).astype(BF16)
    qe = jnp.concatenate([_stack_heads(aq_ref[...], A_HEADS),
                          jnp.concatenate([eye] * A_HEADS, axis=0)], axis=1)
    ones = jnp.ones((tk, LANES), BF16)

    def logits(kb):
        return _dot_nt(qe, jnp.concatenate([k_ref[kb], bias_sc[kb]], axis=1))

    def values(kb):
        return jnp.concatenate([v_ref[kb], ones], axis=1)

    acc0 = jnp.zeros((rows, A_LAT + LANES), F32)
    if bounded:
        def att_group(kbs, carry):
            p = jnp.concatenate([jnp.exp2(logits(kb)).astype(BF16) for kb in kbs], axis=1)
            acc_sc[...] += _dot(p, jnp.concatenate([values(kb) for kb in kbs], axis=0))
            return carry

        acc_sc[...] = acc0
        _grouped_loop(n_kb, att_group, 0)
        acc = acc_sc[...]
    else:
        def softmax_step(kb, m, acc, s):
            m_new = jnp.maximum(m, jnp.max(s, axis=1, keepdims=True))
            p = jnp.exp2(s - m_new).astype(BF16)
            return m_new, jnp.exp2(m - m_new) * acc + _dot(p, values(kb))

        def att_body(kb, carry):
            m, acc, s = carry
            s_next = logits(kb + 1)
            m, acc = softmax_step(kb, m, acc, s)
            return m, acc, s_next

        m, acc, s = lax.fori_loop(0, n_kb - 1, att_body, (jnp.full((rows, 1), M_INIT, F32), acc0, logits(0)))
        _, acc = softmax_step(n_kb - 1, m, acc, s)
    o_lat = (acc[:, :A_LAT] / acc[:, A_LAT:]).astype(BF16)
    o_all = jnp.concatenate([o_lat[hh * tq:(hh + 1) * tq] for hh in range(A_HEADS)], axis=1)
    o_ref[...] = _dot(o_all, wuv_ref[...]).astype(o_ref.dtype)


def _dsa_call(shift, aq, iq, iwt, akd, av, ikd, w_uv, *, bounded):
    b, s, _ = aq.shape
    tq = min(TQ_DSA, s)
    tk = min(TK_DSA, s)
    assert tq == LANES, "the one-hot mask columns pair one query tile with 128 contraction lanes"
    top_k = min(TOPK_MAX, s // 4)
    n_kb = s // tk
    nq = s // tq
    blk = lambda x: x.reshape(b, n_kb, tk, x.shape[-1])
    tri = (jnp.arange(TIE_BLOCK)[:, None] >= jnp.arange(TIE_BLOCK)[None, :]).astype(BF16)
    eye = jnp.eye(A_HEADS, dtype=w_uv.dtype)
    wuv_bd = (w_uv[:, :, None, :] * eye[:, None, :, None]).reshape(A_HEADS * A_LAT, A_HEADS * HEAD_DIM)
    wuv_bd = wuv_bd.astype(BF16)

    qmap = lambda bi, i: (bi, i, 0)
    kvmap = lambda bi, i: (bi, 0, 0, 0)
    kv_spec = pl.BlockSpec((None, n_kb, tk, LANES), kvmap)
    return pl.pallas_call(
        functools.partial(_dsa_kernel, top_k=top_k, bounded=bounded),
        grid=(b, nq),
        in_specs=[pl.BlockSpec(memory_space=pltpu.SMEM),
                  pl.BlockSpec((None, tq, aq.shape[-1]), qmap),
                  pl.BlockSpec((None, tq, iq.shape[-1]), qmap),
                  pl.BlockSpec((iwt.shape[0], tq), lambda bi, i: (0, bi * nq + i)),
                  kv_spec, kv_spec, kv_spec,
                  _const_spec(tri.shape), _const_spec(wuv_bd.shape)],
        out_specs=pl.BlockSpec((None, tq, A_HEADS * HEAD_DIM), qmap),
        out_shape=jax.ShapeDtypeStruct((b, s, A_HEADS * HEAD_DIM), BF16),
        scratch_shapes=[pltpu.VMEM((n_kb, tk, tq), I32), pltpu.VMEM((n_kb, tk, tq), BF16),
                        pltpu.VMEM((n_kb, WORD_BITS, tk // WORD_BITS, tq), I32),
                        pltpu.VMEM((n_kb, tk // WORD_BITS, tq), I32),
                        pltpu.VMEM((A_HEADS * tq, A_LAT + LANES), F32)],
        compiler_params=_params("parallel", "arbitrary"),
        name="dsa",
    )(shift, aq, iq, iwt, blk(akd), blk(av), blk(ikd), tri, wuv_bd)


def _diff_kernel(shift_ref, lam_ref, q_ref, k_ref, v_ref, g_ref, o_ref, acc_sc, *, lam_init, bounded):
    t = q_ref.shape[0]
    i = pl.program_id(2)
    lam_v = lam_ref[...]
    lam = (jnp.exp(jnp.sum(lam_v[0:1] * lam_v[1:2], axis=1, keepdims=True))
           - jnp.exp(jnp.sum(lam_v[2:3] * lam_v[3:4], axis=1, keepdims=True)) + lam_init)

    q = q_ref[...]
    lane = lax.broadcasted_iota(I32, q.shape, 1)
    zero = jnp.zeros_like(q)
    q2 = jnp.concatenate([jnp.where(lane < HEAD_DIM, q, zero),
                          jnp.where(lane >= HEAD_DIM, q, zero)], axis=0)

    vdim = v_ref.shape[-1]
    ones = jnp.ones((t, LANES), BF16)

    def values(j):
        return jnp.concatenate([v_ref[j], ones], axis=1)

    acc0 = jnp.zeros((2 * t, vdim + LANES), F32)

    if bounded:
        assert t // CHUNK < LANES
        q_lane = lax.broadcasted_iota(I32, (2 * t, LANES), 1)
        q_chunk = lax.broadcasted_iota(I32, (2 * t, LANES), 0)
        q_chunk = jnp.where(q_chunk >= t, q_chunk - t, q_chunk) >> CHUNK_SHIFT
        q_extra = jnp.where(q_lane == 0, 1.0, jnp.where(q_lane - 1 > q_chunk, NEG_BIAS, 0.0))
        q2 = jnp.concatenate([q2, q_extra.astype(BF16)], axis=1)
        k_lane = lax.broadcasted_iota(I32, (t, LANES), 1)
        k_chunk = lax.broadcasted_iota(I32, (t, LANES), 0) >> CHUNK_SHIFT
        k_plain = jnp.where(k_lane == 0, -shift_ref[0], 0.0)
        k_diag = jnp.where(k_lane - 1 == k_chunk, 1.0, k_plain).astype(BF16)
        k_plain = k_plain.astype(BF16)

        def probs(j):
            k_extra = jnp.where(j == i, k_diag, k_plain)
            return jnp.exp2(_dot_nt(q2, jnp.concatenate([k_ref[j], k_extra], axis=1))).astype(BF16)

        def group(js, carry):
            p = jnp.concatenate([probs(j) for j in js], axis=1)
            acc_sc[...] += _dot(p, jnp.concatenate([values(j) for j in js], axis=0))
            return carry

        acc_sc[...] = acc0
        _grouped_loop(i + 1, group, 0)
        acc = acc_sc[...]
    else:
        qc = lax.broadcasted_iota(I32, (2 * t, t), 0)
        qc = jnp.where(qc >= t, qc - t, qc) >> CHUNK_SHIFT
        kc = lax.broadcasted_iota(I32, (2 * t, t), 1) >> CHUNK_SHIFT

        def logits(j):
            return _dot_nt(q2, k_ref[j])

        def softmax_step(j, m, acc, s):
            m_new = jnp.maximum(m, jnp.max(s, axis=1, keepdims=True))
            p = jnp.exp2(s - m_new).astype(BF16)
            return m_new, jnp.exp2(m - m_new) * acc + _dot(p, values(j))

        def body(j, carry):
            m, acc, s = carry
            s_next = logits(j + 1)
            m, acc = softmax_step(j, m, acc, s)
            return m, acc, s_next

        m, acc, s = lax.fori_loop(0, i, body, (jnp.full((2 * t, 1), M_INIT, F32), acc0, logits(0)))
        _, acc = softmax_step(i, m, acc, jnp.where(kc <= qc, s, NEG_BIAS))
    o = acc[:, :vdim] / acc[:, vdim:]
    o = o[:t] - lam * o[t:]
    o_ref[...] = (_rms(o, g_ref[...]) * (1.0 - lam_init)).astype(o_ref.dtype)


def _diff_call(shift, bq, bk, bv, lam_vecs, subln_g, lam_init, *, bounded):
    b, s, w = bq.shape
    t = min(T_DIFF, s)
    nb = s // t
    k4 = bk.reshape(b, nb, t, w)
    v4 = bv.reshape(b, nb, t, w)
    qmap = lambda bi, hh, i: (bi, i, hh)
    kvmap = lambda bi, hh, i: (bi, 0, 0, hh)
    return pl.pallas_call(
        functools.partial(_diff_kernel, lam_init=lam_init, bounded=bounded),
        grid=(b, B_HEADS, nb),
        in_specs=[pl.BlockSpec(memory_space=pltpu.SMEM),
                  _const_spec(lam_vecs.shape),
                  pl.BlockSpec((None, t, LANES), qmap),
                  pl.BlockSpec((None, nb, t, LANES), kvmap),
                  pl.BlockSpec((None, nb, t, LANES), kvmap),
                  _const_spec(subln_g.shape)],
        out_specs=pl.BlockSpec((None, t, LANES), qmap),
        out_shape=jax.ShapeDtypeStruct((b, s, w), BF16),
        scratch_shapes=[pltpu.VMEM((2 * t, 2 * LANES), F32)],
        compiler_params=_params("parallel", "parallel", "arbitrary"),
        name="diff_attn",
    )(shift, lam_vecs, bq, k4, v4, subln_g)


def _odd_kernel(h_ref, g_ref, wc_ref, wd_ref, clg_ref, clb_ref, ws_ref, sb_ref,
                cw_ref, cb_ref, dlg_ref, dlb_ref, woc_ref, wod_ref, o_ref, buf_sc, *, tiles_per_seq):
    tm = h_ref.shape[0]
    cwid = clg_ref.shape[1]

    h = h_ref[...]
    xn = _rms(h, g_ref[...]).astype(BF16)

    zc = jax.nn.gelu(_dot(xn, wc_ref[...]), approximate=True)
    u = zc[:, :cwid]
    v = _layer_norm(zc[:, cwid:], clg_ref[...], clb_ref[...]).astype(BF16)
    pi = lax.broadcasted_iota(I32, (SGU_BLOCK, SGU_BLOCK), 0) >> CHUNK_SHIFT
    pj = lax.broadcasted_iota(I32, (SGU_BLOCK, SGU_BLOCK), 1) >> CHUNK_SHIFT
    causal = pi >= pj
    w_s = [jnp.where(causal, ws_ref[gi], jnp.zeros((SGU_BLOCK, SGU_BLOCK), BF16)) for gi in range(C_GROUPS)]
    lane = lax.broadcasted_iota(I32, (SGU_BLOCK, LANES), 1)
    sgb = sb_ref[...]
    rows_out = []
    for r in range(tm // SGU_BLOCK):
        rs = slice(r * SGU_BLOCK, (r + 1) * SGU_BLOCK)
        slabs = []
        for pr in range(cwid // LANES):
            vs = v[rs, pr * LANES:(pr + 1) * LANES]
            lo = _dot(w_s[2 * pr], vs)
            hi = _dot(w_s[2 * pr + 1], vs)
            slabs.append(jnp.where(lane < HEAD_DIM, lo, hi))
        rows_out.append(jnp.concatenate(slabs, axis=1) + sgb)
    c_out = (u * jnp.concatenate(rows_out, axis=0)).astype(BF16)

    zd = _dot(xn, wd_ref[...])
    dwid = zd.shape[1] // 2
    hd = zd[:, :dwid] * jax.nn.sigmoid(zd[:, dwid:])

    @pl.when(pl.program_id(0) % tiles_per_seq == 0)
    def _():
        buf_sc[0:CONV_HALO, :] = jnp.zeros((CONV_HALO, dwid), F32)

    buf_sc[CONV_HALO:CONV_HALO + tm, :] = hd
    buf_sc[CONV_HALO + tm:, :] = jnp.zeros((SUBLANES, dwid), F32)
    conv = jnp.zeros((tm, dwid), F32) + cb_ref[...]
    base = CONV_HALO - (CONV_W - 1)
    for r in range(SUBLANES):
        group = None
        for w in range(CONV_W):
            if (base + w) % SUBLANES == r:
                a = (base + w) // SUBLANES * SUBLANES
                term = buf_sc[a:a + tm + SUBLANES, :] * cw_ref[w:w + 1, :]
                group = term if group is None else group + term
        conv = conv + group[r:r + tm]
    buf_sc[0:CONV_HALO, :] = buf_sc[tm:tm + CONV_HALO, :]
    d_out = _layer_norm(conv, dlg_ref[...], dlb_ref[...])
    d_out = (d_out * jax.nn.sigmoid(d_out)).astype(BF16)

    o_ref[...] = h + _dot(c_out, woc_ref[...]) + _dot(d_out, wod_ref[...])


def _odd_call(h, seq_len, g, w_in, w_out, c_ln_g, c_ln_b, c_w_s, c_b_s, d_conv_w, d_conv_b, d_ln_g, d_ln_b):
    n, d = h.shape
    tm = min(TM_ODD, seq_len)
    cwid = c_ln_g.shape[0]
    dwid = d_ln_g.shape[0]
    row = lambda x: x.reshape(1, -1).astype(F32)
    wc = w_in[:, :2 * cwid].astype(BF16)
    wd = w_in[:, 2 * cwid:].astype(BF16)
    sgb = jnp.repeat(c_b_s.T, cwid // C_GROUPS, axis=1).astype(F32)
    consts = (row(g), wc, wd, row(c_ln_g), row(c_ln_b), c_w_s.astype(BF16), sgb,
              d_conv_w.astype(F32), row(d_conv_b), row(d_ln_g), row(d_ln_b),
              w_out[:cwid].astype(BF16), w_out[cwid:].astype(BF16))
    tok = lambda i: (i, 0)
    return pl.pallas_call(
        functools.partial(_odd_kernel, tiles_per_seq=seq_len // tm),
        grid=(n // tm,),
        in_specs=[pl.BlockSpec((tm, d), tok)] + [_const_spec(c.shape) for c in consts],
        out_specs=pl.BlockSpec((tm, d), tok),
        out_shape=jax.ShapeDtypeStruct((n, d), F32),
        scratch_shapes=[pltpu.VMEM((tm + CONV_HALO + SUBLANES, dwid), F32)],
        compiler_params=_params("arbitrary"),
        name="odd_mixer",
    )(h, *consts)


def _logit_bound(gq, gk):
    unit = HEAD_DIM * HEAD_DIM ** -0.5 * math.log2(math.e) * BF16_SLACK
    return (unit * jnp.max(jnp.abs(gq)) * jnp.max(jnp.abs(gk))).astype(F32).reshape(1)


def _bounded_or_online(bound, call, *args):
    return lax.cond(bound[0] <= MAX_SHIFT,
                    lambda: call(bound, *args, bounded=True),
                    lambda: call(jnp.zeros_like(bound), *args, bounded=False))
def kernel(x, p, pos, ffn1_g, ffn1_wg, ffn1_wu, ffn1_wd, mix_g, ffn2_g, ffn2_wg, ffn2_wu, ffn2_wd, ple_g, ple_wgate, ple_wproj, ev_w_in, ev_w_out, a_q_g, a_k_g, a_w_uv, b_q_g, b_k_g, b_lam_q1, b_lam_k1, b_lam_q2, b_lam_k2, b_subln_g, od_w_in, od_w_out, c_ln_g, c_ln_b, c_w_s, c_b_s, d_conv_w, d_conv_b, d_ln_g, d_ln_b):
    bsz, s_len, d = x.shape
    n = bsz * s_len
    depth = p.shape[0]
    row = lambda v: v.reshape(1, -1).astype(F32)
    h = x.reshape(n, d)
    for layer in range(depth):
        ffn1 = (row(ffn1_g[layer]), ffn1_wg[layer].astype(BF16), ffn1_wu[layer].astype(BF16),
                ffn1_wd[layer].astype(BF16))
        ffn2 = (row(ffn2_g[layer]), ffn2_wg[layer].astype(BF16), ffn2_wu[layer].astype(BF16),
                ffn2_wd[layer].astype(BF16))
        ple = (p[layer].reshape(n, -1), row(ple_g[layer]), ple_wgate[layer].astype(BF16),
               ple_wproj[layer].astype(BF16))
        h = _ffn_call(h, *ffn1)
        if layer % 2 == 0:
            e = layer // 2
            lam_init = 0.8 - 0.6 * math.exp(-0.3 * layer)
            aq, akd, bq, bk, iq, ikd, av, bv, iwt = _even_proj_call(
                h, pos, row(mix_g[layer]), ev_w_in[e], a_q_g[e], a_k_g[e], b_q_g[e], b_k_g[e])
            r3 = lambda a: a.reshape(bsz, s_len, a.shape[-1])
            a_out = _bounded_or_online(
                _logit_bound(a_q_g[e], a_k_g[e]), _dsa_call,
                r3(aq), r3(iq), iwt, r3(akd), r3(av), r3(ikd), a_w_uv[e])
            lam_vecs = jnp.stack([b_lam_q1[e], b_lam_k1[e], b_lam_q2[e], b_lam_k2[e]]).astype(F32)
            b_out = _bounded_or_online(
                _logit_bound(b_q_g[e], b_k_g[e]), _diff_call,
                r3(bq), r3(bk), r3(bv), lam_vecs, row(b_subln_g[e]), lam_init)
            a_w = A_HEADS * HEAD_DIM
            mix = (a_out.reshape(n, -1), b_out.reshape(n, -1),
                   ev_w_out[e][:a_w].astype(BF16), ev_w_out[e][a_w:].astype(BF16))
            h = _ffn_call(h, *ffn2, mix=mix, ple=ple)
        else:
            o = layer // 2
            h = _odd_call(h, s_len, mix_g[layer], od_w_in[o], od_w_out[o], c_ln_g[o], c_ln_b[o],
                          c_w_s[o], c_b_s[o], d_conv_w[o], d_conv_b[o], d_ln_g[o], d_ln_b[o])
            h = _ffn_call(h, *ffn2, ple=ple)
    return h.reshape(bsz, s_len, d)
```

```python
import functools
import math

import jax
import jax.numpy as jnp
from jax import lax
from jax.experimental import pallas as pl
from jax.experimental.pallas import tpu as pltpu

F32 = jnp.float32
BF16 = jnp.bfloat16
I32 = jnp.int32

EPS = 1e-6
HEAD_DIM = 64
CHUNK = 64
CHUNK_SHIFT = 6
ROPE_THETA = 10000.0
A_HEADS = 8
A_LAT = 128
IDX_HEADS = 4
TOPK_MAX = 256
B_HEADS = 4
C_GROUPS = 8
SGU_BLOCK = 128
CONV_W = 31
CONV_HALO = 32

LANES = 128
SUBLANES = 8
MXU_WIDTH = 256
INT_MIN = -(2 ** 31)
NEG_BIAS = -1e30
M_INIT = -1e29
MAX_SHIFT = 60.0
BF16_SLACK = 1.02

FF_CHUNK = 256
TM_FFN = 512
TM_PROJ = 512
TM_ODD = 256
TQ_DSA = 128
TK_DSA = 512
T_DIFF = 512
TIE_BLOCK = 256
WORD_BITS = 32
ATT_UNROLL = 4

VMEM_LIMIT_BYTES = 56 * 1024 * 1024


def _params(*sem):
    return pltpu.CompilerParams(dimension_semantics=sem, vmem_limit_bytes=VMEM_LIMIT_BYTES)


def _const_spec(shape):
    zeros = (0,) * len(shape)
    return pl.BlockSpec(shape, lambda *_: zeros)


def _rms(x, g):
    ms = jnp.mean(x * x, axis=-1, keepdims=True)
    return x * lax.rsqrt(ms + EPS) * g


def _layer_norm(x, g, b):
    mu = jnp.mean(x, axis=-1, keepdims=True)
    xc = x - mu
    var = jnp.mean(xc * xc, axis=-1, keepdims=True)
    return xc * lax.rsqrt(var + EPS) * g + b


def _grouped_loop(n, group_step, carry):
    done = 0
    unroll = ATT_UNROLL
    while unroll >= 1:
        def group(g, c, unroll=unroll, done=done):
            first = done + g * unroll
            return group_step([first + u for u in range(unroll)], c)

        n_groups = lax.shift_right_logical(n - done, unroll.bit_length() - 1)
        carry = lax.fori_loop(0, n_groups, group, carry)
        done = done + n_groups * unroll
        unroll //= 2
    return carry


def _unrolled_loop(n, step, carry):
    def group_step(ks, c):
        for k in ks:
            c = step(k, c)
        return c

    return _grouped_loop(n, group_step, carry)


def _dot(a, b):
    return jnp.dot(a, b, preferred_element_type=F32)


def _dot_nt(a, b):
    return lax.dot_general(a, b, (((1,), (1,)), ((), ())), preferred_element_type=F32)


def _ffn_kernel(*refs, has_mix, has_ple, n_chunks):
    refs = list(refs)
    h_ref = refs.pop(0)
    if has_mix:
        a_ref, b_ref, woa_ref, wob_ref = refs[:4]
        refs = refs[4:]
    g_ref, wg_ref, wu_ref, wd_ref = refs[:4]
    refs = refs[4:]
    if has_ple:
        p_ref, pg_ref, wgate_ref, wproj_ref = refs[:4]
        refs = refs[4:]
    (o_ref,) = refs

    h = h_ref[...]
    if has_mix:
        h = h + _dot(a_ref[...], woa_ref[...]) + _dot(b_ref[...], wob_ref[...])
    xn = _rms(h, g_ref[...]).astype(BF16)
    acc = jnp.zeros(h.shape, F32)
    for c in range(n_chunks):
        gt = _dot(xn, wg_ref[:, c * FF_CHUNK:(c + 1) * FF_CHUNK])
        up = _dot(xn, wu_ref[:, c * FF_CHUNK:(c + 1) * FF_CHUNK])
        act = (gt * jax.nn.sigmoid(gt) * up).astype(BF16)
        acc = acc + _dot(act, wd_ref[c * FF_CHUNK:(c + 1) * FF_CHUNK, :])
    h = h + 0.5 * acc
    if has_ple:
        xg = _rms(h, pg_ref[...]).astype(BF16)
        gate = jax.nn.sigmoid(_dot(xg, wgate_ref[...]))
        h = h + gate * _dot(p_ref[...].astype(BF16), wproj_ref[...])
    o_ref[...] = h


def _ffn_call(h, g, wg, wu, wd, mix=None, ple=None):
    n, d = h.shape
    tm = min(TM_FFN, n)
    d_ff = wd.shape[0]
    n_chunks = d_ff // FF_CHUNK
    tok = lambda i: (i, 0)
    args = [h]
    specs = [pl.BlockSpec((tm, d), tok)]
    if mix is not None:
        a, b, woa, wob = mix
        args += [a, b, woa, wob]
        specs += [pl.BlockSpec((tm, a.shape[1]), tok), pl.BlockSpec((tm, b.shape[1]), tok),
                  _const_spec(woa.shape), _const_spec(wob.shape)]
    args += [g, wg, wu, wd]
    specs += [_const_spec(g.shape), _const_spec(wg.shape), _const_spec(wu.shape), _const_spec(wd.shape)]
    if ple is not None:
        p, pg, wgate, wproj = ple
        args += [p, pg, wgate, wproj]
        specs += [pl.BlockSpec((tm, p.shape[1]), tok), _const_spec(pg.shape),
                  _const_spec(wgate.shape), _const_spec(wproj.shape)]
    kern = functools.partial(_ffn_kernel, has_mix=mix is not None, has_ple=ple is not None,
                             n_chunks=n_chunks)
    return pl.pallas_call(
        kern,
        grid=(n // tm,),
        in_specs=specs,
        out_specs=pl.BlockSpec((tm, d), tok),
        out_shape=jax.ShapeDtypeStruct((n, d), F32),
        compiler_params=_params("parallel"),
        name="ffn",
    )(*args)


_NR_WIDTHS = (A_HEADS * HEAD_DIM, 2 * HEAD_DIM, B_HEADS * 2 * HEAD_DIM, B_HEADS * 2 * HEAD_DIM)
_R_WIDTHS = (IDX_HEADS * HEAD_DIM, 2 * HEAD_DIM)
_P_WIDTHS = (A_LAT, B_HEADS * 2 * HEAD_DIM)
IW_ROWS = 8


def _even_proj_kernel(h_ref, pos_ref, g_ref, inv_ref, sgn_ref, wnr_ref, wr_ref, wp_ref, wiwt_ref,
                      gn_ref, gnr_ref, gmat_ref,
                      aq_ref, akd_ref, bq_ref, bk_ref, iq_ref, ikd_ref, av_ref, bv_ref, iwt_ref):
    xn = _rms(h_ref[...], g_ref[...]).astype(BF16)
    ang = pos_ref[...].astype(F32) * inv_ref[...]
    cos = jnp.cos(ang)
    sin = jnp.sin(ang) * sgn_ref[...]
    gmat = gmat_ref[...]
    first_half = lax.broadcasted_iota(I32, cos.shape, 1) % HEAD_DIM < HEAD_DIM // 2

    def swap_halves(xs):
        return jnp.where(first_half, pltpu.roll(xs, LANES - HEAD_DIM // 2, 1), pltpu.roll(xs, HEAD_DIM // 2, 1))

    off = 0
    for width, out_ref in zip(_NR_WIDTHS, (aq_ref, akd_ref, bq_ref, bk_ref)):
        x = _dot(xn, wnr_ref[:, off:off + width])
        mw = min(width, MXU_WIDTH)
        ms = jnp.concatenate(
            [_dot(jnp.square(x[:, c:c + mw]).astype(BF16), gmat[:mw, :mw]) for c in range(0, width, mw)], axis=1)
        for s in range(width // LANES):
            sl = slice(s * LANES, (s + 1) * LANES)
            gsl = slice(off + s * LANES, off + (s + 1) * LANES)
            xs = x[:, sl]
            y = (xs * (gn_ref[:, gsl] * cos) + swap_halves(xs) * (gnr_ref[:, gsl] * sin)) * lax.rsqrt(ms[:, sl] + EPS)
            out_ref[:, sl] = y.astype(out_ref.dtype)
        off += width

    off = 0
    for width, out_ref in zip(_R_WIDTHS, (iq_ref, ikd_ref)):
        x = _dot(xn, wr_ref[:, off:off + width])
        for s in range(width // LANES):
            xs = x[:, s * LANES:(s + 1) * LANES]
            out_ref[:, s * LANES:(s + 1) * LANES] = (xs * cos + swap_halves(xs) * sin).astype(out_ref.dtype)
        off += width

    off = 0
    for width, out_ref in zip(_P_WIDTHS, (av_ref, bv_ref)):
        out_ref[...] = _dot(xn, wp_ref[:, off:off + width]).astype(out_ref.dtype)
        off += width
    iwt_ref[...] = _dot_nt(wiwt_ref[...], xn)


def _swap_halves(w):
    lead = w.shape[:-1]
    w4 = w.reshape(*lead, -1, 2, HEAD_DIM // 2)
    return w4[..., ::-1, :].reshape(*lead, -1)


def _even_proj_call(h, pos, g, w_in, a_q_g, a_k_g, b_q_g, b_k_g):
    n, d = h.shape
    tm = min(TM_PROJ, n)
    hd = HEAD_DIM
    sizes = (A_HEADS * hd, hd, A_LAT, IDX_HEADS * hd, hd, IDX_HEADS,
             B_HEADS * 2 * hd, B_HEADS * 2 * hd, B_HEADS * 2 * hd)
    offs = [0]
    for s in sizes:
        offs.append(offs[-1] + s)
    w_aq, w_ak, w_av, w_iq, w_ik, w_iw, w_bq, w_bk, w_bv = (
        w_in[:, offs[i]:offs[i + 1]] for i in range(9))
    dup = lambda w: jnp.concatenate([w, w], axis=1)
    wnr = jnp.concatenate([w_aq, dup(w_ak), w_bq, w_bk], axis=1).astype(BF16)
    wr = jnp.concatenate([w_iq, dup(w_ik)], axis=1).astype(BF16)
    wp = jnp.concatenate([w_av, w_bv], axis=1).astype(BF16)
    wiwt = jnp.pad(w_iw.T, ((0, IW_ROWS - IDX_HEADS), (0, 0))).astype(BF16)

    scale = hd ** -0.5 * math.log2(math.e)
    gn =jnp.concatenate([jnp.tile(a_q_g, A_HEADS) * scale, jnp.tile(a_k_g, 2),
                          jnp.tile(b_q_g, 2 * B_HEADS) * scale, jnp.tile(b_k_g, 2 * B_HEADS)])
    gnr = _swap_halves(gn)
    gn = gn.reshape(1, -1).astype(F32)
    gnr = gnr.reshape(1, -1).astype(F32)

    inv = ROPE_THETA ** (-jnp.arange(0, hd, 2, dtype=F32) / hd)
    inv128 = jnp.tile(inv, LANES // (hd // 2)).reshape(1, LANES)
    sgn = jnp.tile(jnp.concatenate([-jnp.ones(hd // 2, F32), jnp.ones(hd // 2, F32)]),
                   LANES // hd).reshape(1, LANES)
    head_of_lane = jnp.arange(MXU_WIDTH) // hd
    gmat = ((head_of_lane[:, None] == head_of_lane[None, :]).astype(F32) / hd).astype(BF16)

    tok = lambda i: (i, 0)
    out_widths = (A_HEADS * hd, 2 * hd, B_HEADS * 2 * hd, B_HEADS * 2 * hd,
                  IDX_HEADS * hd, 2 * hd, A_LAT, B_HEADS * 2 * hd)
    consts = (g, inv128, sgn, wnr, wr, wp, wiwt, gn, gnr, gmat)
    return pl.pallas_call(
        _even_proj_kernel,
        grid=(n // tm,),
        in_specs=[pl.BlockSpec((tm, d), tok), pl.BlockSpec((tm, 1), tok)]
                 + [_const_spec(c.shape) for c in consts],
        out_specs=[pl.BlockSpec((tm, w), tok) for w in out_widths]
                  + [pl.BlockSpec((IW_ROWS, tm), lambda i: (0, i))],
        out_shape=[jax.ShapeDtypeStruct((n, w), BF16) for w in out_widths]
                  + [jax.ShapeDtypeStruct((IW_ROWS, n), F32)],
        compiler_params=_params("parallel"),
        name="even_proj",
    )(h, pos.reshape(n, 1), *consts)


def _bit_planes(words):
    a = list(words)
    j, m = WORD_BITS // 2, 0x0000FFFF
    while j:
        k = 0
        while k < WORD_BITS:
            t = (a[k] ^ lax.shift_right_logical(a[k + j], j)) & m
            a[k] = a[k] ^ t
            a[k + j] = a[k + j] ^ lax.shift_left(t, j)
            k = (k + j + 1) & ~j
        j >>= 1
        m ^= m << j
    return a


def _stack_heads(x, n_heads):
    lane = lax.broadcasted_iota(I32, (x.shape[0], LANES), 1)
    parts = []
    for hh in range(n_heads):
        slab = x[:, (hh // 2) * LANES:(hh // 2 + 1) * LANES]
        keep = (lane < HEAD_DIM) if hh % 2 == 0 else (lane >= HEAD_DIM)
        parts.append(jnp.where(keep, slab, jnp.zeros_like(slab)))
    return jnp.concatenate(parts, axis=0)


def _dsa_kernel(*refs, top_k, bounded, diff_lam_init=None):
    fused = diff_lam_init is not None
    refs = list(refs)
    shift_ref = refs.pop(0)
    if fused:
        assert bounded
        shiftb_ref, lam_ref, bq_ref, bk_ref, bv_ref, bg_ref = refs[:6]
        refs = refs[6:]
    aq_ref, iq_ref, iwt_ref, k_ref, v_ref, ik_ref, tri_ref, wuv_ref, o_ref = refs[:9]
    refs = refs[9:]
    if fused:
        ob_ref = refs.pop(0)
    key_sc, bias_sc, plane_sc, alive_sc, acc_sc = refs[:5]
    dacc_sc = refs[5] if fused else None
    tq = aq_ref.shape[0]
    tk = k_ref.shape[1]
    q0 = pl.program_id(1) * tq
    n_kb = (q0 + tq + tk - 1) // tk

    iq_all = _stack_heads(iq_ref[...], IDX_HEADS)
    iwt = iwt_ref[...]
    qchunk = (q0 + lax.broadcasted_iota(I32, (1, tq), 1)) >> CHUNK_SHIFT
    krow_chunk = lax.broadcasted_iota(I32, (tk, tq), 0) >> CHUNK_SHIFT
    groups = tk // (WORD_BITS * SUBLANES)
    n_blocks = key_sc.shape[0]

    def score_block(kb, masked):
        dots = jnp.maximum(_dot_nt(ik_ref[kb], iq_all), 0.0)
        sc = dots[:, 0:tq] * iwt[0:1, :]
        for hh in range(1, IDX_HEADS):
            sc = sc + dots[:, hh * tq:(hh + 1) * tq] * iwt[hh:hh + 1, :]
        bits = pltpu.bitcast(sc, I32)
        key = jnp.where(bits < 0, INT_MIN - bits, bits)
        if masked:
            key = jnp.where(krow_chunk <= qchunk - kb * (tk >> CHUNK_SHIFT), key, INT_MIN)
        key_sc[kb] = key
        for g in range(groups):
            base = g * WORD_BITS * SUBLANES
            planes = _bit_planes([key[base + SUBLANES * j:base + SUBLANES * (j + 1)] for j in range(WORD_BITS)])
            planes[0] = ~planes[0]
            for p in range(WORD_BITS):
                plane_sc[kb, p, SUBLANES * g:SUBLANES * (g + 1), :] = planes[p]
        alive_sc[kb] = jnp.full((SUBLANES * groups, tq), -1, I32)

    if fused:
        diff_group = _diff_bounded_group(shiftb_ref[0], bq_ref, bk_ref, bv_ref, dacc_sc, n_kb - 1)

        def fused_group(kbs, carry):
            diff_group(kbs)
            for kb in kbs:
                score_block(kb, True)
            return carry

        _grouped_loop(n_kb, fused_group, 0)
        _diff_finish(dacc_sc[...], lam_ref, bg_ref, ob_ref, diff_lam_init)
    else:
        def score_step(kb, carry):
            score_block(kb, False)
            return carry

        _unrolled_loop(n_kb - 1, score_step, 0)
        score_block(n_kb - 1, True)

    def clear_body(kb, carry):
        plane_sc[kb] = jnp.zeros(plane_sc.shape[1:], I32)
        alive_sc[kb] = jnp.zeros(alive_sc.shape[1:], I32)
        return carry

    lax.fori_loop(n_kb, n_blocks, clear_body, 0)

    def radix_body(b, carry):
        t_u, k_rem, keep = carry
        first = jnp.where(b == 0, jnp.int32(-1), jnp.int32(0))
        b_prev = jnp.maximum(b - 1, 0)

        acc = jnp.zeros((SUBLANES * groups, tq), I32)
        for kb in range(n_blocks):
            a = alive_sc[kb] & ((plane_sc[kb, b_prev] ^ keep) | first)
            alive_sc[kb] = a
            acc = acc + lax.population_count(a & plane_sc[kb, b])
        cnt = jnp.sum(acc.astype(F32), axis=0, keepdims=True)
        take = cnt >= k_rem
        t_u = t_u | jnp.where(take, lax.shift_left(jnp.int32(1), 31 - b), 0)
        return t_u, jnp.where(take, k_rem, k_rem - cnt), jnp.where(take, 0, -1)

    init = (jnp.zeros((1, tq), I32), jnp.full((1, tq), float(top_k), F32), jnp.zeros((1, tq), I32))
    t_u, k_rem, keep = lax.fori_loop(0, WORD_BITS, radix_body, init)

    n_eq = jnp.zeros((SUBLANES * groups, tq), I32)
    for kb in range(n_blocks):
        n_eq = n_eq + lax.population_count(alive_sc[kb] & (plane_sc[kb, WORD_BITS - 1] ^ keep))
    n_eq = jnp.sum(n_eq.astype(F32), axis=0, keepdims=True)
    t = t_u ^ INT_MIN
    few = t == INT_MIN
    need = jnp.where(few, 0.0, k_rem)
    n_tied = jnp.sum(jnp.where(few | (n_eq == need), 0.0, 1.0))

    sel_bias = -shift_ref[0]

    def plain_bias():
        t_min = jnp.where(few, INT_MIN + 1, t)

        def body(kb, carry):
            bias_sc[kb] = jnp.where(key_sc[kb] >= t_min, sel_bias, NEG_BIAS).astype(BF16)
            return carry

        lax.fori_loop(0, n_kb, body, 0)

    def ranked_bias():
        tri = tri_ref[...]

        def body(kb, carry):
            kk_all = key_sc[kb]
            for c in range(tk // TIE_BLOCK):
                kk = kk_all[c * TIE_BLOCK:(c + 1) * TIE_BLOCK]
                eq = kk == t
                rank = _dot(tri, jnp.where(eq, 1.0, 0.0).astype(BF16)) + carry
                tie_bias = jnp.where(rank <= need, sel_bias, NEG_BIAS)
                bias = jnp.where(kk > t, sel_bias, jnp.where(eq, tie_bias, NEG_BIAS))
                bias_sc[kb, c * TIE_BLOCK:(c + 1) * TIE_BLOCK, :] = bias.astype(BF16)
                carry = rank[TIE_BLOCK - 1:TIE_BLOCK, :]
            return carry

        lax.fori_loop(0, n_kb, body, jnp.zeros((1, tq), F32))

    lax.cond(n_tied == 0.0, plain_bias, ranked_bias)

    rows = A_HEADS * tq
    eye = lax.broadcasted_iota(I32, (tq, tq), 0) == lax.broadcasted_iota(I32, (tq, tq), 1)
    eye = jnp.where(eye, 1.0, 0.0).astype(BF16)
    qe = jnp.concatenate([_stack_heads(aq_ref[...], A_HEADS),
                          jnp.concatenate([eye] * A_HEADS, axis=0)], axis=1)
    ones = jnp.ones((tk, LANES), BF16)

    def logits(kb):
        return _dot_nt(qe, jnp.concatenate([k_ref[kb], bias_sc[kb]], axis=1))

    def values(kb):
        return jnp.concatenate([v_ref[kb], ones], axis=1)

    acc0 = jnp.zeros((rows, A_LAT + LANES), F32)
    if bounded:
        def att_group(kbs, carry):
            p = jnp.concatenate([jnp.exp2(logits(kb)).astype(BF16) for kb in kbs], axis=1)
            acc_sc[...] += _dot(p, jnp.concatenate([values(kb) for kb in kbs], axis=0))
            return carry

        acc_sc[...] = acc0
        _grouped_loop(n_kb, att_group, 0)
        acc = acc_sc[...]
    else:
        def softmax_step(kb, m, acc, s):
            m_new = jnp.maximum(m, jnp.max(s, axis=1, keepdims=True))
            p = jnp.exp2(s - m_new).astype(BF16)
            return m_new, jnp.exp2(m - m_new) * acc + _dot(p, values(kb))

        def att_body(kb, carry):
            m, acc, s = carry
            s_next = logits(kb + 1)
            m, acc = softmax_step(kb, m, acc, s)
            return m, acc, s_next

        m, acc, s = lax.fori_loop(0, n_kb - 1, att_body, (jnp.full((rows, 1), M_INIT, F32), acc0, logits(0)))
        _, acc = softmax_step(n_kb - 1, m, acc, s)
    o_lat = (acc[:, :A_LAT] / acc[:, A_LAT:]).astype(BF16)
    o_all = jnp.concatenate([o_lat[hh * tq:(hh + 1) * tq] for hh in range(A_HEADS)], axis=1)
    o_ref[...] = _dot(o_all, wuv_ref[...]).astype(o_ref.dtype)


def _dsa_call(shift, aq, iq, iwt, akd, av, ikd, w_uv, *, bounded, diff=None):
    b, s, _ = aq.shape
    tq = min(TQ_DSA, s)
    tk = min(TK_DSA, s)
    assert tq == LANES, "the one-hot mask columns pair one query tile with 128 contraction lanes"
    top_k = min(TOPK_MAX, s // 4)
    n_kb = s // tk
    nq = s // tq
    blk = lambda x: x.reshape(b, n_kb, tk, x.shape[-1])
    tri = (jnp.arange(TIE_BLOCK)[:, None] >= jnp.arange(TIE_BLOCK)[None, :]).astype(BF16)
    eye = jnp.eye(A_HEADS, dtype=w_uv.dtype)
    wuv_bd = (w_uv[:, :, None, :] * eye[:, None, :, None]).reshape(A_HEADS * A_LAT, A_HEADS * HEAD_DIM)
    wuv_bd = wuv_bd.astype(BF16)

    qmap = lambda bi, i: (bi, i, 0)
    kvmap = lambda bi, i: (bi, 0, 0, 0)
    kv_spec = pl.BlockSpec((None, n_kb, tk, LANES), kvmap)
    args = [shift]
    in_specs = [pl.BlockSpec(memory_space=pltpu.SMEM)]
    out_specs = [pl.BlockSpec((None, tq, A_HEADS * HEAD_DIM), qmap)]
    out_shape = [jax.ShapeDtypeStruct((b, s, A_HEADS * HEAD_DIM), BF16)]
    scratch = [pltpu.VMEM((n_kb, tk, tq), I32), pltpu.VMEM((n_kb, tk, tq), BF16),
               pltpu.VMEM((n_kb, WORD_BITS, tk // WORD_BITS, tq), I32),
               pltpu.VMEM((n_kb, tk // WORD_BITS, tq), I32),
               pltpu.VMEM((A_HEADS * tq, A_LAT + LANES), F32)]
    lam_init = None
    if diff is not None:
        shift_b, bq, bk, bv, lam_vecs, subln_g, lam_init = diff
        assert B_HEADS * tq == tk and bq.shape[-1] == B_HEADS * LANES
        dqmap = lambda bi, i: (bi, i // B_HEADS, i % B_HEADS)
        dkvmap = lambda bi, i: (bi, 0, 0, i % B_HEADS)
        args += [shift_b, lam_vecs, bq, bk.reshape(b, n_kb, tk, -1), bv.reshape(b, n_kb, tk, -1), subln_g]
        in_specs += [pl.BlockSpec(memory_space=pltpu.SMEM), _const_spec(lam_vecs.shape),
                     pl.BlockSpec((None, tk, LANES), dqmap),
                     pl.BlockSpec((None, n_kb, tk, LANES), dkvmap),
                     pl.BlockSpec((None, n_kb, tk, LANES), dkvmap),
                     _const_spec(subln_g.shape)]
        out_specs.append(pl.BlockSpec((None, tk, LANES), dqmap))
        out_shape.append(jax.ShapeDtypeStruct(bq.shape, BF16))
        scratch.append(pltpu.VMEM((2 * tk, 2 * LANES), F32))
    args += [aq, iq, iwt, blk(akd), blk(av), blk(ikd), tri, wuv_bd]
    in_specs += [pl.BlockSpec((None, tq, aq.shape[-1]), qmap),
                 pl.BlockSpec((None, tq, iq.shape[-1]), qmap),
                 pl.BlockSpec((iwt.shape[0], tq), lambda bi, i: (0, bi * nq + i)),
                 kv_spec, kv_spec, kv_spec,
                 _const_spec(tri.shape), _const_spec(wuv_bd.shape)]
    outs = pl.pallas_call(
        functools.partial(_dsa_kernel, top_k=top_k, bounded=bounded, diff_lam_init=lam_init),
        grid=(b, nq),
        in_specs=in_specs,
        out_specs=out_specs,
        out_shape=out_shape,
        scratch_shapes=scratch,
        compiler_params=_params("parallel", "arbitrary"),
        name="dsa" if diff is None else "dsa_diff",
    )(*args)
    return outs[0] if diff is None else tuple(outs)


def _diff_split_maps(q):
    lane = lax.broadcasted_iota(I32, q.shape, 1)
    zero = jnp.zeros_like(q)
    return jnp.concatenate([jnp.where(lane < HEAD_DIM, q, zero), jnp.where(lane >= HEAD_DIM, q, zero)], axis=0)


def _diff_bounded_group(shift, q_ref, k_ref, v_ref, acc_sc, i):
    t = q_ref.shape[0]
    assert t // CHUNK < LANES
    q_lane = lax.broadcasted_iota(I32, (2 * t, LANES), 1)
    q_chunk = lax.broadcasted_iota(I32, (2 * t, LANES), 0)
    q_chunk = jnp.where(q_chunk >= t, q_chunk - t, q_chunk) >> CHUNK_SHIFT
    q_extra = jnp.where(q_lane == 0, 1.0, jnp.where(q_lane - 1 > q_chunk, NEG_BIAS, 0.0))
    q2 = jnp.concatenate([_diff_split_maps(q_ref[...]), q_extra.astype(BF16)], axis=1)
    k_lane = lax.broadcasted_iota(I32, (t, LANES), 1)
    k_chunk = lax.broadcasted_iota(I32, (t, LANES), 0) >> CHUNK_SHIFT
    k_plain = jnp.where(k_lane == 0, -shift, 0.0)
    k_diag = jnp.where(k_lane - 1 == k_chunk, 1.0, k_plain).astype(BF16)
    k_plain = k_plain.astype(BF16)
    ones = jnp.ones((t, LANES), BF16)

    def probs(j):
        k_extra = jnp.where(j == i, k_diag, k_plain)
        return jnp.exp2(_dot_nt(q2, jnp.concatenate([k_ref[j], k_extra], axis=1))).astype(BF16)

    def group(js):
        p = jnp.concatenate([probs(j) for j in js], axis=1)
        acc_sc[...] += _dot(p, jnp.concatenate([jnp.concatenate([v_ref[j], ones], axis=1) for j in js], axis=0))

    acc_sc[...] = jnp.zeros(acc_sc.shape, F32)
    return group


def _diff_finish(acc, lam_ref, g_ref, o_ref, lam_init):
    t = acc.shape[0] // 2
    vdim = o_ref.shape[-1]
    lam_v = lam_ref[...]
    lam = (jnp.exp(jnp.sum(lam_v[0:1] * lam_v[1:2], axis=1, keepdims=True))
           - jnp.exp(jnp.sum(lam_v[2:3] * lam_v[3:4], axis=1, keepdims=True)) + lam_init)
    o = acc[:, :vdim] / acc[:, vdim:]
    o = o[:t] - lam * o[t:]
    o_ref[...] = (_rms(o, g_ref[...]) * (1.0 - lam_init)).astype(o_ref.dtype)


def _diff_kernel(shift_ref, lam_ref, q_ref, k_ref, v_ref, g_ref, o_ref, acc_sc, *, lam_init, bounded):
    t = q_ref.shape[0]
    i = pl.program_id(2)
    vdim = v_ref.shape[-1]

    if bounded:
        group = _diff_bounded_group(shift_ref[0], q_ref, k_ref, v_ref, acc_sc, i)

        def group_step(js, carry):
            group(js)
            return carry

        _grouped_loop(i + 1, group_step, 0)
        acc = acc_sc[...]
    else:
        q2 = _diff_split_maps(q_ref[...])
        ones = jnp.ones((t, LANES), BF16)

        def values(j):
            return jnp.concatenate([v_ref[j], ones], axis=1)

        acc0 = jnp.zeros((2 * t, vdim + LANES), F32)
        qc = lax.broadcasted_iota(I32, (2 * t, t), 0)
        qc = jnp.where(qc >= t, qc - t, qc) >> CHUNK_SHIFT
        kc = lax.broadcasted_iota(I32, (2 * t, t), 1) >> CHUNK_SHIFT

        def logits(j):
            return _dot_nt(q2, k_ref[j])

        def softmax_step(j, m, acc, s):
            m_new = jnp.maximum(m, jnp.max(s, axis=1, keepdims=True))
            p = jnp.exp2(s - m_new).astype(BF16)
            return m_new, jnp.exp2(m - m_new) * acc + _dot(p, values(j))

        def body(j, carry):
            m, acc, s = carry
            s_next = logits(j + 1)
            m, acc = softmax_step(j, m, acc, s)
            return m, acc, s_next

        m, acc, s = lax.fori_loop(0, i, body, (jnp.full((2 * t, 1), M_INIT, F32), acc0, logits(0)))
        _, acc = softmax_step(i, m, acc, jnp.where(kc <= qc, s, NEG_BIAS))
    _diff_finish(acc, lam_ref, g_ref, o_ref, lam_init)


def _diff_call(shift, bq, bk, bv, lam_vecs, subln_g, lam_init, *, bounded):
    b, s, w = bq.shape
    t = min(T_DIFF, s)
    nb = s // t
    k4 = bk.reshape(b, nb, t, w)
    v4 = bv.reshape(b, nb, t, w)
    qmap = lambda bi, hh, i: (bi, i, hh)
    kvmap = lambda bi, hh, i: (bi, 0, 0, hh)
    return pl.pallas_call(
        functools.partial(_diff_kernel, lam_init=lam_init, bounded=bounded),
        grid=(b, B_HEADS, nb),
        in_specs=[pl.BlockSpec(memory_space=pltpu.SMEM),
                  _const_spec(lam_vecs.shape),
                  pl.BlockSpec((None, t, LANES), qmap),
                  pl.BlockSpec((None, nb, t, LANES), kvmap),
                  pl.BlockSpec((None, nb, t, LANES), kvmap),
                  _const_spec(subln_g.shape)],
        out_specs=pl.BlockSpec((None, t, LANES), qmap),
        out_shape=jax.ShapeDtypeStruct((b, s, w), BF16),
        scratch_shapes=[pltpu.VMEM((2 * t, 2 * LANES), F32)],
        compiler_params=_params("parallel", "parallel", "arbitrary"),
        name="diff_attn",
    )(shift, lam_vecs, bq, k4, v4, subln_g)


def _odd_kernel(h_ref, g_ref, wc_ref, wd_ref, clg_ref, clb_ref, ws_ref, sb_ref,
                cw_ref, cb_ref, dlg_ref, dlb_ref, woc_ref, wod_ref, o_ref, buf_sc, *, tiles_per_seq):
    tm = h_ref.shape[0]
    cwid = clg_ref.shape[1]

    h = h_ref[...]
    xn = _rms(h, g_ref[...]).astype(BF16)

    zc = jax.nn.gelu(_dot(xn, wc_ref[...]), approximate=True)
    u = zc[:, :cwid]
    v = _layer_norm(zc[:, cwid:], clg_ref[...], clb_ref[...]).astype(BF16)
    pi = lax.broadcasted_iota(I32, (SGU_BLOCK, SGU_BLOCK), 0) >> CHUNK_SHIFT
    pj = lax.broadcasted_iota(I32, (SGU_BLOCK, SGU_BLOCK), 1) >> CHUNK_SHIFT
    causal = pi >= pj
    w_s = [jnp.where(causal, ws_ref[gi], jnp.zeros((SGU_BLOCK, SGU_BLOCK), BF16)) for gi in range(C_GROUPS)]
    lane = lax.broadcasted_iota(I32, (SGU_BLOCK, LANES), 1)
    sgb = sb_ref[...]
    rows_out = []
    for r in range(tm // SGU_BLOCK):
        rs = slice(r * SGU_BLOCK, (r + 1) * SGU_BLOCK)
        slabs = []
        for pr in range(cwid // LANES):
            vs = v[rs, pr * LANES:(pr + 1) * LANES]
            lo = _dot(w_s[2 * pr], vs)
            hi = _dot(w_s[2 * pr + 1], vs)
            slabs.append(jnp.where(lane < HEAD_DIM, lo, hi))
        rows_out.append(jnp.concatenate(slabs, axis=1) + sgb)
    c_out = (u * jnp.concatenate(rows_out, axis=0)).astype(BF16)

    zd = _dot(xn, wd_ref[...])
    dwid = zd.shape[1] // 2
    hd = zd[:, :dwid] * jax.nn.sigmoid(zd[:, dwid:])

    @pl.when(pl.program_id(0) % tiles_per_seq == 0)
    def _():
        buf_sc[0:CONV_HALO, :] = jnp.zeros((CONV_HALO, dwid), F32)

    buf_sc[CONV_HALO:CONV_HALO + tm, :] = hd
    buf_sc[CONV_HALO + tm:, :] = jnp.zeros((SUBLANES, dwid), F32)
    conv = jnp.zeros((tm, dwid), F32) + cb_ref[...]
    base = CONV_HALO - (CONV_W - 1)
    for r in range(SUBLANES):
        group = None
        for w in range(CONV_W):
            if (base + w) % SUBLANES == r:
                a = (base + w) // SUBLANES * SUBLANES
                term = buf_sc[a:a + tm + SUBLANES, :] * cw_ref[w:w + 1, :]
                group = term if group is None else group + term
        conv = conv + group[r:r + tm]
    buf_sc[0:CONV_HALO, :] = buf_sc[tm:tm + CONV_HALO, :]
    d_out = _layer_norm(conv, dlg_ref[...], dlb_ref[...])
    d_out = (d_out * jax.nn.sigmoid(d_out)).astype(BF16)

    o_ref[...] = h + _dot(c_out, woc_ref[...]) + _dot(d_out, wod_ref[...])


def _odd_call(h, seq_len, g, w_in, w_out, c_ln_g, c_ln_b, c_w_s, c_b_s, d_conv_w, d_conv_b, d_ln_g, d_ln_b):
    n, d = h.shape
    tm = min(TM_ODD, seq_len)
    cwid = c_ln_g.shape[0]
    dwid = d_ln_g.shape[0]
    row = lambda x: x.reshape(1, -1).astype(F32)
    wc = w_in[:, :2 * cwid].astype(BF16)
    wd = w_in[:, 2 * cwid:].astype(BF16)
    sgb = jnp.repeat(c_b_s.T, cwid // C_GROUPS, axis=1).astype(F32)
    consts = (row(g), wc, wd, row(c_ln_g), row(c_ln_b), c_w_s.astype(BF16), sgb,
              d_conv_w.astype(F32), row(d_conv_b), row(d_ln_g), row(d_ln_b),
              w_out[:cwid].astype(BF16), w_out[cwid:].astype(BF16))
    tok = lambda i: (i, 0)
    return pl.pallas_call(
        functools.partial(_odd_kernel, tiles_per_seq=seq_len // tm),
        grid=(n // tm,),
        in_specs=[pl.BlockSpec((tm, d), tok)] + [_const_spec(c.shape) for c in consts],
        out_specs=pl.BlockSpec((tm, d), tok),
        out_shape=jax.ShapeDtypeStruct((n, d), F32),
        scratch_shapes=[pltpu.VMEM((tm + CONV_HALO + SUBLANES, dwid), F32)],
        compiler_params=_params("arbitrary"),
        name="odd_mixer",
    )(h, *consts)


def _logit_bound(gq, gk):
    unit = HEAD_DIM * HEAD_DIM ** -0.5 * math.log2(math.e) * BF16_SLACK
    return (unit * jnp.max(jnp.abs(gq)) * jnp.max(jnp.abs(gk))).astype(F32).reshape(1)


def _bounded_or_online(bound, call, *args):
    return lax.cond(bound[0] <= MAX_SHIFT,
                    lambda: call(bound, *args, bounded=True),
                    lambda: call(jnp.zeros_like(bound), *args, bounded=False))
def kernel(x, p, pos, ffn1_g, ffn1_wg, ffn1_wu, ffn1_wd, mix_g, ffn2_g, ffn2_wg, ffn2_wu, ffn2_wd, ple_g, ple_wgate, ple_wproj, ev_w_in, ev_w_out, a_q_g, a_k_g, a_w_uv, b_q_g, b_k_g, b_lam_q1, b_lam_k1, b_lam_q2, b_lam_k2, b_subln_g, od_w_in, od_w_out, c_ln_g, c_ln_b, c_w_s, c_b_s, d_conv_w, d_conv_b, d_ln_g, d_ln_b):
    bsz, s_len, d = x.shape
    n = bsz * s_len
    depth = p.shape[0]
    row = lambda v: v.reshape(1, -1).astype(F32)
    h = x.reshape(n, d)
    for layer in range(depth):
        ffn1 = (row(ffn1_g[layer]), ffn1_wg[layer].astype(BF16), ffn1_wu[layer].astype(BF16),
                ffn1_wd[layer].astype(BF16))
        ffn2 = (row(ffn2_g[layer]), ffn2_wg[layer].astype(BF16), ffn2_wu[layer].astype(BF16),
                ffn2_wd[layer].astype(BF16))
        ple = (p[layer].reshape(n, -1), row(ple_g[layer]), ple_wgate[layer].astype(BF16),
               ple_wproj[layer].astype(BF16))
        h = _ffn_call(h, *ffn1)
        if layer % 2 == 0:
            e = layer // 2
            lam_init = 0.8 - 0.6 * math.exp(-0.3 * layer)
            aq, akd, bq, bk, iq, ikd, av, bv, iwt = _even_proj_call(
                h, pos, row(mix_g[layer]), ev_w_in[e], a_q_g[e], a_k_g[e], b_q_g[e], b_k_g[e])
            r3 = lambda a: a.reshape(bsz, s_len, a.shape[-1])
            lam_vecs = jnp.stack([b_lam_q1[e], b_lam_k1[e], b_lam_q2[e], b_lam_k2[e]]).astype(F32)
            a_args = (r3(aq), r3(iq), iwt, r3(akd), r3(av), r3(ikd), a_w_uv[e])
            b_args = (r3(bq), r3(bk), r3(bv), lam_vecs, row(b_subln_g[e]), lam_init)
            bound_a = _logit_bound(a_q_g[e], a_k_g[e])
            bound_b = _logit_bound(b_q_g[e], b_k_g[e])
            a_out, b_out = lax.cond(
                jnp.maximum(bound_a[0], bound_b[0]) <= MAX_SHIFT,
                lambda: _dsa_call(bound_a, *a_args, bounded=True, diff=(bound_b,) + b_args),
                lambda: (_bounded_or_online(bound_a, _dsa_call, *a_args),
                         _bounded_or_online(bound_b, _diff_call, *b_args)))
            a_w = A_HEADS * HEAD_DIM
            mix = (a_out.reshape(n, -1), b_out.reshape(n, -1),
                   ev_w_out[e][:a_w].astype(BF16), ev_w_out[e][a_w:].astype(BF16))
            h = _ffn_call(h, *ffn2, mix=mix, ple=ple)
        else:
            o = layer // 2
            h = _odd_call(h, s_len, mix_g[layer], od_w_in[o], od_w_out[o], c_ln_g[o], c_ln_b[o],
                          c_w_s[o], c_b_s[o], d_conv_w[o], d_conv_b[o], d_ln_g[o], d_ln_b[o])
            h = _ffn_call(h, *ffn2, ple=ple)
    return h.reshape(bsz, s_len, d)
```

```python
import functools
import math

import jax
import jax.numpy as jnp
from jax import lax
from jax.experimental import pallas as pl
from jax.experimental.pallas import tpu as pltpu

F32 = jnp.float32
BF16 = jnp.bfloat16
I32 = jnp.int32

EPS = 1e-6
HEAD_DIM = 64
CHUNK = 64
CHUNK_SHIFT = 6
ROPE_THETA = 10000.0
A_HEADS = 8
A_LAT = 128
IDX_HEADS = 4
TOPK_MAX = 256
B_HEADS = 4
C_GROUPS = 8
SGU_BLOCK = 128
CONV_W = 31
CONV_HALO = 32

LANES = 128
SUBLANES = 8
MXU_WIDTH = 256
INT_MIN = -(2 ** 31)
NEG_BIAS = -1e30
M_INIT = -1e29
MAX_SHIFT = 60.0
BF16_SLACK = 1.02

FF_CHUNK = 256
TM_FFN = 512
TM_PROJ = 512
TM_ODD = 256
TQ_DSA = 128
TK_DSA = 512
T_DIFF = 512
TIE_BLOCK = 256
WORD_BITS = 32
ATT_UNROLL = 4

VMEM_LIMIT_BYTES = 56 * 1024 * 1024


def _params(*sem):
    return pltpu.CompilerParams(dimension_semantics=sem, vmem_limit_bytes=VMEM_LIMIT_BYTES)


def _const_spec(shape):
    zeros = (0,) * len(shape)
    return pl.BlockSpec(shape, lambda *_: zeros)


def _rms(x, g):
    ms = jnp.mean(x * x, axis=-1, keepdims=True)
    return x * lax.rsqrt(ms + EPS) * g


def _layer_norm(x, g, b):
    mu = jnp.mean(x, axis=-1, keepdims=True)
    xc = x - mu
    var = jnp.mean(xc * xc, axis=-1, keepdims=True)
    return xc * lax.rsqrt(var + EPS) * g + b


def _grouped_loop(n, group_step, carry):
    done = 0
    unroll = ATT_UNROLL
    while unroll >= 1:
        def group(g, c, unroll=unroll, done=done):
            first = done + g * unroll
            return group_step([first + u for u in range(unroll)], c)

        n_groups = lax.shift_right_logical(n - done, unroll.bit_length() - 1)
        carry = lax.fori_loop(0, n_groups, group, carry)
        done = done + n_groups * unroll
        unroll //= 2
    return carry


def _unrolled_loop(n, step, carry):
    def group_step(ks, c):
        for k in ks:
            c = step(k, c)
        return c

    return _grouped_loop(n, group_step, carry)


def _tree_sum(xs):
    xs = list(xs)
    while len(xs) > 1:
        xs = [xs[i] + xs[i + 1] for i in range(0, len(xs) - 1, 2)] + ([xs[-1]] if len(xs) % 2 else [])
    return xs[0]


def _dot(a, b):
    return jnp.dot(a, b, preferred_element_type=F32)


def _dot_nt(a, b):
    return lax.dot_general(a, b, (((1,), (1,)), ((), ())), preferred_element_type=F32)


def _ffn_kernel(*refs, has_mix, has_ple, n_chunks):
    refs = list(refs)
    h_ref = refs.pop(0)
    if has_mix:
        a_ref, b_ref, woa_ref, wob_ref = refs[:4]
        refs = refs[4:]
    g_ref, wg_ref, wu_ref, wd_ref = refs[:4]
    refs = refs[4:]
    if has_ple:
        p_ref, pg_ref, wgate_ref, wproj_ref = refs[:4]
        refs = refs[4:]
    (o_ref,) = refs

    h = h_ref[...]
    if has_mix:
        h = h + _dot(a_ref[...], woa_ref[...]) + _dot(b_ref[...], wob_ref[...])
    xn = _rms(h, g_ref[...]).astype(BF16)
    acc = jnp.zeros(h.shape, F32)
    for c in range(n_chunks):
        gt = _dot(xn, wg_ref[:, c * FF_CHUNK:(c + 1) * FF_CHUNK])
        up = _dot(xn, wu_ref[:, c * FF_CHUNK:(c + 1) * FF_CHUNK])
        act = (gt * jax.nn.sigmoid(gt) * up).astype(BF16)
        acc = acc + _dot(act, wd_ref[c * FF_CHUNK:(c + 1) * FF_CHUNK, :])
    h = h + 0.5 * acc
    if has_ple:
        xg = _rms(h, pg_ref[...]).astype(BF16)
        gate = jax.nn.sigmoid(_dot(xg, wgate_ref[...]))
        h = h + gate * _dot(p_ref[...].astype(BF16), wproj_ref[...])
    o_ref[...] = h


def _ffn_call(h, g, wg, wu, wd, mix=None, ple=None):
    n, d = h.shape
    tm = min(TM_FFN, n)
    d_ff = wd.shape[0]
    n_chunks = d_ff // FF_CHUNK
    tok = lambda i: (i, 0)
    args = [h]
    specs = [pl.BlockSpec((tm, d), tok)]
    if mix is not None:
        a, b, woa, wob = mix
        args += [a, b, woa, wob]
        specs += [pl.BlockSpec((tm, a.shape[1]), tok), pl.BlockSpec((tm, b.shape[1]), tok),
                  _const_spec(woa.shape), _const_spec(wob.shape)]
    args += [g, wg, wu, wd]
    specs += [_const_spec(g.shape), _const_spec(wg.shape), _const_spec(wu.shape), _const_spec(wd.shape)]
    if ple is not None:
        p, layer, pg, wgate, wproj = ple
        args += [p, pg, wgate, wproj]
        specs += [pl.BlockSpec((None, tm, p.shape[2]), lambda i: (layer, i, 0)), _const_spec(pg.shape),
                  _const_spec(wgate.shape), _const_spec(wproj.shape)]
    kern = functools.partial(_ffn_kernel, has_mix=mix is not None, has_ple=ple is not None,
                             n_chunks=n_chunks)
    return pl.pallas_call(
        kern,
        grid=(n // tm,),
        in_specs=specs,
        out_specs=pl.BlockSpec((tm, d), tok),
        out_shape=jax.ShapeDtypeStruct((n, d), F32),
        compiler_params=_params("parallel"),
        name="ffn",
    )(*args)


_NR_WIDTHS = (A_HEADS * HEAD_DIM, 2 * HEAD_DIM, B_HEADS * 2 * HEAD_DIM, B_HEADS * 2 * HEAD_DIM)
_R_WIDTHS = (IDX_HEADS * HEAD_DIM, 2 * HEAD_DIM)
_P_WIDTHS = (A_LAT, B_HEADS * 2 * HEAD_DIM)
IW_ROWS = 8


def _even_proj_kernel(h_ref, pos_ref, g_ref, inv_ref, sgn_ref, wnr_ref, wr_ref, wp_ref, wiwt_ref,
                      gn_ref, gnr_ref, gmat_ref,
                      aq_ref, akd_ref, bq_ref, bk_ref, iq_ref, ikd_ref, av_ref, bv_ref, iwt_ref):
    xn = _rms(h_ref[...], g_ref[...]).astype(BF16)
    ang = pos_ref[...].astype(F32) * inv_ref[...]
    cos = jnp.cos(ang)
    sin = jnp.sin(ang) * sgn_ref[...]
    gmat = gmat_ref[...]
    first_half = lax.broadcasted_iota(I32, cos.shape, 1) % HEAD_DIM < HEAD_DIM // 2

    def swap_halves(xs):
        return jnp.where(first_half, pltpu.roll(xs, LANES - HEAD_DIM // 2, 1), pltpu.roll(xs, HEAD_DIM // 2, 1))

    off = 0
    for width, out_ref in zip(_NR_WIDTHS, (aq_ref, akd_ref, bq_ref, bk_ref)):
        x = _dot(xn, wnr_ref[:, off:off + width])
        mw = min(width, MXU_WIDTH)
        ms = jnp.concatenate(
            [_dot(jnp.square(x[:, c:c + mw]).astype(BF16), gmat[:mw, :mw]) for c in range(0, width, mw)], axis=1)
        for s in range(width // LANES):
            sl = slice(s * LANES, (s + 1) * LANES)
            gsl = slice(off + s * LANES, off + (s + 1) * LANES)
            xs = x[:, sl]
            y = (xs * (gn_ref[:, gsl] * cos) + swap_halves(xs) * (gnr_ref[:, gsl] * sin)) * lax.rsqrt(ms[:, sl] + EPS)
            out_ref[:, sl] = y.astype(out_ref.dtype)
        off += width

    off = 0
    for width, out_ref in zip(_R_WIDTHS, (iq_ref, ikd_ref)):
        x = _dot(xn, wr_ref[:, off:off + width])
        for s in range(width // LANES):
            xs = x[:, s * LANES:(s + 1) * LANES]
            out_ref[:, s * LANES:(s + 1) * LANES] = (xs * cos + swap_halves(xs) * sin).astype(out_ref.dtype)
        off += width

    off = 0
    for width, out_ref in zip(_P_WIDTHS, (av_ref, bv_ref)):
        out_ref[...] = _dot(xn, wp_ref[:, off:off + width]).astype(out_ref.dtype)
        off += width
    iwt_ref[...] = _dot_nt(wiwt_ref[...], xn)


def _swap_halves(w):
    lead = w.shape[:-1]
    w4 = w.reshape(*lead, -1, 2, HEAD_DIM // 2)
    return w4[..., ::-1, :].reshape(*lead, -1)


def _even_proj_call(h, pos, g, w_in, a_q_g, a_k_g, b_q_g, b_k_g):
    n, d = h.shape
    tm = min(TM_PROJ, n)
    hd = HEAD_DIM
    sizes = (A_HEADS * hd, hd, A_LAT, IDX_HEADS * hd, hd, IDX_HEADS,
             B_HEADS * 2 * hd, B_HEADS * 2 * hd, B_HEADS * 2 * hd)
    offs = [0]
    for s in sizes:
        offs.append(offs[-1] + s)
    w_aq, w_ak, w_av, w_iq, w_ik, w_iw, w_bq, w_bk, w_bv = (
        w_in[:, offs[i]:offs[i + 1]] for i in range(9))
    dup = lambda w: jnp.concatenate([w, w], axis=1)
    wnr = jnp.concatenate([w_aq, dup(w_ak), w_bq, w_bk], axis=1).astype(BF16)
    wr = jnp.concatenate([w_iq, dup(w_ik)], axis=1).astype(BF16)
    wp = jnp.concatenate([w_av, w_bv], axis=1).astype(BF16)
    wiwt = jnp.pad(w_iw.T, ((0, IW_ROWS - IDX_HEADS), (0, 0))).astype(BF16)

    scale = hd ** -0.5 * math.log2(math.e)
    gn =jnp.concatenate([jnp.tile(a_q_g, A_HEADS) * scale, jnp.tile(a_k_g, 2),
                          jnp.tile(b_q_g, 2 * B_HEADS) * scale, jnp.tile(b_k_g, 2 * B_HEADS)])
    gnr = _swap_halves(gn)
    gn = gn.reshape(1, -1).astype(F32)
    gnr = gnr.reshape(1, -1).astype(F32)

    inv = ROPE_THETA ** (-jnp.arange(0, hd, 2, dtype=F32) / hd)
    inv128 = jnp.tile(inv, LANES // (hd // 2)).reshape(1, LANES)
    sgn = jnp.tile(jnp.concatenate([-jnp.ones(hd // 2, F32), jnp.ones(hd // 2, F32)]),
                   LANES // hd).reshape(1, LANES)
    head_of_lane = jnp.arange(MXU_WIDTH) // hd
    gmat = ((head_of_lane[:, None] == head_of_lane[None, :]).astype(F32) / hd).astype(BF16)

    tok = lambda i: (i, 0)
    out_widths = (A_HEADS * hd, 2 * hd, B_HEADS * 2 * hd, B_HEADS * 2 * hd,
                  IDX_HEADS * hd, 2 * hd, A_LAT, B_HEADS * 2 * hd)
    consts = (g, inv128, sgn, wnr, wr, wp, wiwt, gn, gnr, gmat)
    return pl.pallas_call(
        _even_proj_kernel,
        grid=(n // tm,),
        in_specs=[pl.BlockSpec((tm, d), tok), pl.BlockSpec((tm, 1), tok)]
                 + [_const_spec(c.shape) for c in consts],
        out_specs=[pl.BlockSpec((tm, w), tok) for w in out_widths]
                  + [pl.BlockSpec((IW_ROWS, tm), lambda i: (0, i))],
        out_shape=[jax.ShapeDtypeStruct((n, w), BF16) for w in out_widths]
                  + [jax.ShapeDtypeStruct((IW_ROWS, n), F32)],
        compiler_params=_params("parallel"),
        name="even_proj",
    )(h, pos.reshape(n, 1), *consts)


def _bit_planes(words):
    a = list(words)
    j, m = WORD_BITS // 2, 0x0000FFFF
    while j:
        k = 0
        while k < WORD_BITS:
            t = (a[k] ^ lax.shift_right_logical(a[k + j], j)) & m
            a[k] = a[k] ^ t
            a[k + j] = a[k + j] ^ lax.shift_left(t, j)
            k = (k + j + 1) & ~j
        j >>= 1
        m ^= m << j
    return a


def _stack_heads(x, n_heads):
    lane = lax.broadcasted_iota(I32, (x.shape[0], LANES), 1)
    parts = []
    for hh in range(n_heads):
        slab = x[:, (hh // 2) * LANES:(hh // 2 + 1) * LANES]
        keep = (lane < HEAD_DIM) if hh % 2 == 0 else (lane >= HEAD_DIM)
        parts.append(jnp.where(keep, slab, jnp.zeros_like(slab)))
    return jnp.concatenate(parts, axis=0)


def _dsa_kernel(*refs, top_k, bounded, diff_lam_init=None):
    fused = diff_lam_init is not None
    refs = list(refs)
    shift_ref = refs.pop(0)
    if fused:
        assert bounded
        shiftb_ref, lam_ref, bq_ref, bk_ref, bv_ref, bg_ref = refs[:6]
        refs = refs[6:]
    aq_ref, iq_ref, iwt_ref, k_ref, v_ref, ik_ref, tri_ref, wuv_ref, o_ref = refs[:9]
    refs = refs[9:]
    if fused:
        ob_ref = refs.pop(0)
    key_sc, bias_sc, plane_sc, alive_sc, acc_sc = refs[:5]
    dacc_sc = refs[5] if fused else None
    tq = aq_ref.shape[0]
    tk = k_ref.shape[1]
    q0 = pl.program_id(1) * tq
    n_kb = (q0 + tq + tk - 1) // tk

    iq_all = _stack_heads(iq_ref[...], IDX_HEADS)
    iwt = iwt_ref[...]
    qchunk = (q0 + lax.broadcasted_iota(I32, (1, tq), 1)) >> CHUNK_SHIFT
    krow_chunk = lax.broadcasted_iota(I32, (tk, tq), 0) >> CHUNK_SHIFT
    groups = tk // (WORD_BITS * SUBLANES)
    n_blocks = key_sc.shape[0]

    def score_block(kb, masked):
        dots = jnp.maximum(_dot_nt(ik_ref[kb], iq_all), 0.0)
        sc = dots[:, 0:tq] * iwt[0:1, :]
        for hh in range(1, IDX_HEADS):
            sc = sc + dots[:, hh * tq:(hh + 1) * tq] * iwt[hh:hh + 1, :]
        bits = pltpu.bitcast(sc, I32)
        key = jnp.where(bits < 0, INT_MIN - bits, bits)
        if masked:
            key = jnp.where(krow_chunk <= qchunk - kb * (tk >> CHUNK_SHIFT), key, INT_MIN)
        key_sc[kb] = key
        for g in range(groups):
            base = g * WORD_BITS * SUBLANES
            planes = _bit_planes([key[base + SUBLANES * j:base + SUBLANES * (j + 1)] for j in range(WORD_BITS)])
            planes[0] = ~planes[0]
            for p in range(WORD_BITS):
                plane_sc[kb, p, SUBLANES * g:SUBLANES * (g + 1), :] = planes[p]
        alive_sc[kb] = jnp.full((SUBLANES * groups, tq), -1, I32)

    if fused:
        diff_group = _diff_bounded_group(shiftb_ref[0], bq_ref, bk_ref, bv_ref, dacc_sc, n_kb - 1)

        def fused_group(kbs, carry):
            diff_group(kbs)
            for kb in kbs:
                score_block(kb, False)
            return carry

        _grouped_loop(n_kb - 1, fused_group, 0)
    else:
        def score_step(kb, carry):
            score_block(kb, False)
            return carry

        _unrolled_loop(n_kb - 1, score_step, 0)
    score_block(n_kb - 1, True)

    def clear_body(kb, carry):
        plane_sc[kb] = jnp.zeros(plane_sc.shape[1:], I32)
        alive_sc[kb] = jnp.zeros(alive_sc.shape[1:], I32)
        return carry

    lax.fori_loop(n_kb, n_blocks, clear_body, 0)

    def radix_body(b, carry):
        t_u, k_rem, keep = carry
        first = jnp.where(b == 0, jnp.int32(-1), jnp.int32(0))
        b_prev = jnp.maximum(b - 1, 0)

        counts = []
        for kb in range(n_blocks):
            a = alive_sc[kb] & ((plane_sc[kb, b_prev] ^ keep) | first)
            alive_sc[kb] = a
            counts.append(lax.population_count(a & plane_sc[kb, b]))
        cnt = jnp.sum(_tree_sum(counts).astype(F32), axis=0, keepdims=True)
        take = cnt >= k_rem
        t_u = t_u | jnp.where(take, lax.shift_left(jnp.int32(1), 31 - b), 0)
        return t_u, jnp.where(take, k_rem, k_rem - cnt), jnp.where(take, 0, -1)

    init = (jnp.zeros((1, tq), I32), jnp.full((1, tq), float(top_k), F32), jnp.zeros((1, tq), I32))
    if fused:
        diff_group([n_kb - 1])
        carry = init
        for b in range(WORD_BITS):
            carry = radix_body(jnp.int32(b), carry)
        t_u, k_rem, keep = carry
        _diff_finish(dacc_sc[...], lam_ref, bg_ref, ob_ref, diff_lam_init)
    else:
        t_u, k_rem, keep = lax.fori_loop(0, WORD_BITS, radix_body, init)

    n_eq = jnp.zeros((SUBLANES * groups, tq), I32)
    for kb in range(n_blocks):
        n_eq = n_eq + lax.population_count(alive_sc[kb] & (plane_sc[kb, WORD_BITS - 1] ^ keep))
    n_eq = jnp.sum(n_eq.astype(F32), axis=0, keepdims=True)
    t = t_u ^ INT_MIN
    few = t == INT_MIN
    need = jnp.where(few, 0.0, k_rem)
    n_tied = jnp.sum(jnp.where(few | (n_eq == need), 0.0, 1.0))

    sel_bias = -shift_ref[0]

    def plain_bias():
        t_min = jnp.where(few, INT_MIN + 1, t)

        def body(kb, carry):
            bias_sc[kb] = jnp.where(key_sc[kb] >= t_min, sel_bias, NEG_BIAS).astype(BF16)
            return carry

        lax.fori_loop(0, n_kb, body, 0)

    def ranked_bias():
        tri = tri_ref[...]

        def body(kb, carry):
            kk_all = key_sc[kb]
            for c in range(tk // TIE_BLOCK):
                kk = kk_all[c * TIE_BLOCK:(c + 1) * TIE_BLOCK]
                eq = kk == t
                rank = _dot(tri, jnp.where(eq, 1.0, 0.0).astype(BF16)) + carry
                tie_bias = jnp.where(rank <= need, sel_bias, NEG_BIAS)
                bias = jnp.where(kk > t, sel_bias, jnp.where(eq, tie_bias, NEG_BIAS))
                bias_sc[kb, c * TIE_BLOCK:(c + 1) * TIE_BLOCK, :] = bias.astype(BF16)
                carry = rank[TIE_BLOCK - 1:TIE_BLOCK, :]
            return carry

        lax.fori_loop(0, n_kb, body, jnp.zeros((1, tq), F32))

    lax.cond(n_tied == 0.0, plain_bias, ranked_bias)

    rows = A_HEADS * tq
    eye = lax.broadcasted_iota(I32, (tq, tq), 0) == lax.broadcasted_iota(I32, (tq, tq), 1)
    eye = jnp.where(eye, 1.0, 0.0).astype(BF16)
    qe = jnp.concatenate([_stack_heads(aq_ref[...], A_HEADS),
                          jnp.concatenate([eye] * A_HEADS, axis=0)], axis=1)
    ones = jnp.ones((tk, LANES), BF16)

    def logits(kb):
        return _dot_nt(qe, jnp.concatenate([k_ref[kb], bias_sc[kb]], axis=1))

    def values(kb):
        return jnp.concatenate([v_ref[kb], ones], axis=1)

    acc0 = jnp.zeros((rows, A_LAT + LANES), F32)
    if bounded:
        def att_group(kbs, carry):
            p = jnp.concatenate([jnp.exp2(logits(kb)).astype(BF16) for kb in kbs], axis=1)
            acc_sc[...] += _dot(p, jnp.concatenate([values(kb) for kb in kbs], axis=0))
            return carry

        acc_sc[...] = acc0
        _grouped_loop(n_kb, att_group, 0)
        acc = acc_sc[...]
    else:
        def softmax_step(kb, m, acc, s):
            m_new = jnp.maximum(m, jnp.max(s, axis=1, keepdims=True))
            p = jnp.exp2(s - m_new).astype(BF16)
            return m_new, jnp.exp2(m - m_new) * acc + _dot(p, values(kb))

        def att_body(kb, carry):
            m, acc, s = carry
            s_next = logits(kb + 1)
            m, acc = softmax_step(kb, m, acc, s)
            return m, acc, s_next

        m, acc, s = lax.fori_loop(0, n_kb - 1, att_body, (jnp.full((rows, 1), M_INIT, F32), acc0, logits(0)))
        _, acc = softmax_step(n_kb - 1, m, acc, s)
    o_lat = (acc[:, :A_LAT] / acc[:, A_LAT:]).astype(BF16)
    o_all = jnp.concatenate([o_lat[hh * tq:(hh + 1) * tq] for hh in range(A_HEADS)], axis=1)
    o_ref[...] = _dot(o_all, wuv_ref[...]).astype(o_ref.dtype)


def _dsa_call(shift, aq, iq, iwt, akd, av, ikd, w_uv, *, bounded, diff=None):
    b, s, _ = aq.shape
    tq = min(TQ_DSA, s)
    tk = min(TK_DSA, s)
    assert tq == LANES, "the one-hot mask columns pair one query tile with 128 contraction lanes"
    top_k = min(TOPK_MAX, s // 4)
    n_kb = s // tk
    nq = s // tq
    blk = lambda x: x.reshape(b, n_kb, tk, x.shape[-1])
    tri = (jnp.arange(TIE_BLOCK)[:, None] >= jnp.arange(TIE_BLOCK)[None, :]).astype(BF16)
    eye = jnp.eye(A_HEADS, dtype=w_uv.dtype)
    wuv_bd = (w_uv[:, :, None, :] * eye[:, None, :, None]).reshape(A_HEADS * A_LAT, A_HEADS * HEAD_DIM)
    wuv_bd = wuv_bd.astype(BF16)

    qmap = lambda bi, i: (bi, i, 0)
    kvmap = lambda bi, i: (bi, 0, 0, 0)
    kv_spec = pl.BlockSpec((None, n_kb, tk, LANES), kvmap)
    args = [shift]
    in_specs = [pl.BlockSpec(memory_space=pltpu.SMEM)]
    out_specs = [pl.BlockSpec((None, tq, A_HEADS * HEAD_DIM), qmap)]
    out_shape = [jax.ShapeDtypeStruct((b, s, A_HEADS * HEAD_DIM), BF16)]
    scratch = [pltpu.VMEM((n_kb, tk, tq), I32), pltpu.VMEM((n_kb, tk, tq), BF16),
               pltpu.VMEM((n_kb, WORD_BITS, tk // WORD_BITS, tq), I32),
               pltpu.VMEM((n_kb, tk // WORD_BITS, tq), I32),
               pltpu.VMEM((A_HEADS * tq, A_LAT + LANES), F32)]
    lam_init = None
    if diff is not None:
        shift_b, bq, bk, bv, lam_vecs, subln_g, lam_init = diff
        assert B_HEADS * tq == tk and bq.shape[-1] == B_HEADS * LANES
        dqmap = lambda bi, i: (bi, i // B_HEADS, i % B_HEADS)
        dkvmap = lambda bi, i: (bi, 0, 0, i % B_HEADS)
        args += [shift_b, lam_vecs, bq, bk.reshape(b, n_kb, tk, -1), bv.reshape(b, n_kb, tk, -1), subln_g]
        in_specs += [pl.BlockSpec(memory_space=pltpu.SMEM), _const_spec(lam_vecs.shape),
                     pl.BlockSpec((None, tk, LANES), dqmap),
                     pl.BlockSpec((None, n_kb, tk, LANES), dkvmap),
                     pl.BlockSpec((None, n_kb, tk, LANES), dkvmap),
                     _const_spec(subln_g.shape)]
        out_specs.append(pl.BlockSpec((None, tk, LANES), dqmap))
        out_shape.append(jax.ShapeDtypeStruct(bq.shape, BF16))
        scratch.append(pltpu.VMEM((2 * tk, 2 * LANES), F32))
    args += [aq, iq, iwt, blk(akd), blk(av), blk(ikd), tri, wuv_bd]
    in_specs += [pl.BlockSpec((None, tq, aq.shape[-1]), qmap),
                 pl.BlockSpec((None, tq, iq.shape[-1]), qmap),
                 pl.BlockSpec((iwt.shape[0], tq), lambda bi, i: (0, bi * nq + i)),
                 kv_spec, kv_spec, kv_spec,
                 _const_spec(tri.shape), _const_spec(wuv_bd.shape)]
    outs = pl.pallas_call(
        functools.partial(_dsa_kernel, top_k=top_k, bounded=bounded, diff_lam_init=lam_init),
        grid=(b, nq),
        in_specs=in_specs,
        out_specs=out_specs,
        out_shape=out_shape,
        scratch_shapes=scratch,
        compiler_params=_params("parallel", "arbitrary"),
        name="dsa" if diff is None else "dsa_diff",
    )(*args)
    return outs[0] if diff is None else tuple(outs)


def _diff_split_maps(q):
    lane = lax.broadcasted_iota(I32, q.shape, 1)
    zero = jnp.zeros_like(q)
    return jnp.concatenate([jnp.where(lane < HEAD_DIM, q, zero), jnp.where(lane >= HEAD_DIM, q, zero)], axis=0)


def _diff_bounded_group(shift, q_ref, k_ref, v_ref, acc_sc, i):
    t = q_ref.shape[0]
    assert t // CHUNK < LANES
    q_lane = lax.broadcasted_iota(I32, (2 * t, LANES), 1)
    q_chunk = lax.broadcasted_iota(I32, (2 * t, LANES), 0)
    q_chunk = jnp.where(q_chunk >= t, q_chunk - t, q_chunk) >> CHUNK_SHIFT
    q_extra = jnp.where(q_lane == 0, 1.0, jnp.where(q_lane - 1 > q_chunk, NEG_BIAS, 0.0))
    q2 = jnp.concatenate([_diff_split_maps(q_ref[...]), q_extra.astype(BF16)], axis=1)
    k_lane = lax.broadcasted_iota(I32, (t, LANES), 1)
    k_chunk = lax.broadcasted_iota(I32, (t, LANES), 0) >> CHUNK_SHIFT
    k_plain = jnp.where(k_lane == 0, -shift, 0.0)
    k_diag = jnp.where(k_lane - 1 == k_chunk, 1.0, k_plain).astype(BF16)
    k_plain = k_plain.astype(BF16)
    ones = jnp.ones((t, LANES), BF16)

    def probs(j):
        k_extra = jnp.where(j == i, k_diag, k_plain)
        return jnp.exp2(_dot_nt(q2, jnp.concatenate([k_ref[j], k_extra], axis=1))).astype(BF16)

    def group(js):
        p = jnp.concatenate([probs(j) for j in js], axis=1)
        acc_sc[...] += _dot(p, jnp.concatenate([jnp.concatenate([v_ref[j], ones], axis=1) for j in js], axis=0))

    acc_sc[...] = jnp.zeros(acc_sc.shape, F32)
    return group


def _diff_finish(acc, lam_ref, g_ref, o_ref, lam_init):
    t = acc.shape[0] // 2
    vdim = o_ref.shape[-1]
    lam_v = lam_ref[...]
    lam = (jnp.exp(jnp.sum(lam_v[0:1] * lam_v[1:2], axis=1, keepdims=True))
           - jnp.exp(jnp.sum(lam_v[2:3] * lam_v[3:4], axis=1, keepdims=True)) + lam_init)
    o = acc[:, :vdim] / acc[:, vdim:]
    o = o[:t] - lam * o[t:]
    o_ref[...] = (_rms(o, g_ref[...]) * (1.0 - lam_init)).astype(o_ref.dtype)


def _diff_kernel(shift_ref, lam_ref, q_ref, k_ref, v_ref, g_ref, o_ref, acc_sc, *, lam_init, bounded):
    t = q_ref.shape[0]
    i = pl.program_id(2)
    vdim = v_ref.shape[-1]

    if bounded:
        group = _diff_bounded_group(shift_ref[0], q_ref, k_ref, v_ref, acc_sc, i)

        def group_step(js, carry):
            group(js)
            return carry

        _grouped_loop(i + 1, group_step, 0)
        acc = acc_sc[...]
    else:
        q2 = _diff_split_maps(q_ref[...])
        ones = jnp.ones((t, LANES), BF16)

        def values(j):
            return jnp.concatenate([v_ref[j], ones], axis=1)

        acc0 = jnp.zeros((2 * t, vdim + LANES), F32)
        qc = lax.broadcasted_iota(I32, (2 * t, t), 0)
        qc = jnp.where(qc >= t, qc - t, qc) >> CHUNK_SHIFT
        kc = lax.broadcasted_iota(I32, (2 * t, t), 1) >> CHUNK_SHIFT

        def logits(j):
            return _dot_nt(q2, k_ref[j])

        def softmax_step(j, m, acc, s):
            m_new = jnp.maximum(m, jnp.max(s, axis=1, keepdims=True))
            p = jnp.exp2(s - m_new).astype(BF16)
            return m_new, jnp.exp2(m - m_new) * acc + _dot(p, values(j))

        def body(j, carry):
            m, acc, s = carry
            s_next = logits(j + 1)
            m, acc = softmax_step(j, m, acc, s)
            return m, acc, s_next

        m, acc, s = lax.fori_loop(0, i, body, (jnp.full((2 * t, 1), M_INIT, F32), acc0, logits(0)))
        _, acc = softmax_step(i, m, acc, jnp.where(kc <= qc, s, NEG_BIAS))
    _diff_finish(acc, lam_ref, g_ref, o_ref, lam_init)


def _diff_call(shift, bq, bk, bv, lam_vecs, subln_g, lam_init, *, bounded):
    b, s, w = bq.shape
    t = min(T_DIFF, s)
    nb = s // t
    k4 = bk.reshape(b, nb, t, w)
    v4 = bv.reshape(b, nb, t, w)
    qmap = lambda bi, hh, i: (bi, i, hh)
    kvmap = lambda bi, hh, i: (bi, 0, 0, hh)
    return pl.pallas_call(
        functools.partial(_diff_kernel, lam_init=lam_init, bounded=bounded),
        grid=(b, B_HEADS, nb),
        in_specs=[pl.BlockSpec(memory_space=pltpu.SMEM),
                  _const_spec(lam_vecs.shape),
                  pl.BlockSpec((None, t, LANES), qmap),
                  pl.BlockSpec((None, nb, t, LANES), kvmap),
                  pl.BlockSpec((None, nb, t, LANES), kvmap),
                  _const_spec(subln_g.shape)],
        out_specs=pl.BlockSpec((None, t, LANES), qmap),
        out_shape=jax.ShapeDtypeStruct((b, s, w), BF16),
        scratch_shapes=[pltpu.VMEM((2 * t, 2 * LANES), F32)],
        compiler_params=_params("parallel", "parallel", "arbitrary"),
        name="diff_attn",
    )(shift, lam_vecs, bq, k4, v4, subln_g)


def _odd_kernel(h_ref, g_ref, wc_ref, wd_ref, clg_ref, clb_ref, ws_ref, sb_ref,
                cw_ref, cb_ref, dlg_ref, dlb_ref, woc_ref, wod_ref, o_ref, buf_sc, *, tiles_per_seq):
    tm = h_ref.shape[0]
    cwid = clg_ref.shape[1]

    h = h_ref[...]
    xn = _rms(h, g_ref[...]).astype(BF16)

    zc = jax.nn.gelu(_dot(xn, wc_ref[...]), approximate=True)
    u = zc[:, :cwid]
    v = _layer_norm(zc[:, cwid:], clg_ref[...], clb_ref[...]).astype(BF16)
    pi = lax.broadcasted_iota(I32, (SGU_BLOCK, SGU_BLOCK), 0) >> CHUNK_SHIFT
    pj = lax.broadcasted_iota(I32, (SGU_BLOCK, SGU_BLOCK), 1) >> CHUNK_SHIFT
    causal = pi >= pj
    w_s = [jnp.where(causal, ws_ref[gi], jnp.zeros((SGU_BLOCK, SGU_BLOCK), BF16)) for gi in range(C_GROUPS)]
    lane = lax.broadcasted_iota(I32, (SGU_BLOCK, LANES), 1)
    sgb = sb_ref[...]
    rows_out = []
    for r in range(tm // SGU_BLOCK):
        rs = slice(r * SGU_BLOCK, (r + 1) * SGU_BLOCK)
        slabs = []
        for pr in range(cwid // LANES):
            vs = v[rs, pr * LANES:(pr + 1) * LANES]
            lo = _dot(w_s[2 * pr], vs)
            hi = _dot(w_s[2 * pr + 1], vs)
            slabs.append(jnp.where(lane < HEAD_DIM, lo, hi))
        rows_out.append(jnp.concatenate(slabs, axis=1) + sgb)
    c_out = (u * jnp.concatenate(rows_out, axis=0)).astype(BF16)

    zd = _dot(xn, wd_ref[...])
    dwid = zd.shape[1] // 2
    hd = zd[:, :dwid] * jax.nn.sigmoid(zd[:, dwid:])

    @pl.when(pl.program_id(0) % tiles_per_seq == 0)
    def _():
        buf_sc[0:CONV_HALO, :] = jnp.zeros((CONV_HALO, dwid), F32)

    buf_sc[CONV_HALO:CONV_HALO + tm, :] = hd
    buf_sc[CONV_HALO + tm:, :] = jnp.zeros((SUBLANES, dwid), F32)
    conv = jnp.zeros((tm, dwid), F32) + cb_ref[...]
    base = CONV_HALO - (CONV_W - 1)
    for r in range(SUBLANES):
        group = None
        for w in range(CONV_W):
            if (base + w) % SUBLANES == r:
                a = (base + w) // SUBLANES * SUBLANES
                term = buf_sc[a:a + tm + SUBLANES, :] * cw_ref[w:w + 1, :]
                group = term if group is None else group + term
        conv = conv + group[r:r + tm]
    buf_sc[0:CONV_HALO, :] = buf_sc[tm:tm + CONV_HALO, :]
    d_out = _layer_norm(conv, dlg_ref[...], dlb_ref[...])
    d_out = (d_out * jax.nn.sigmoid(d_out)).astype(BF16)

    o_ref[...] = h + _dot(c_out, woc_ref[...]) + _dot(d_out, wod_ref[...])


def _odd_call(h, seq_len, g, w_in, w_out, c_ln_g, c_ln_b, c_w_s, c_b_s, d_conv_w, d_conv_b, d_ln_g, d_ln_b):
    n, d = h.shape
    tm = min(TM_ODD, seq_len)
    cwid = c_ln_g.shape[0]
    dwid = d_ln_g.shape[0]
    row = lambda x: x.reshape(1, -1).astype(F32)
    wc = w_in[:, :2 * cwid].astype(BF16)
    wd = w_in[:, 2 * cwid:].astype(BF16)
    sgb = jnp.repeat(c_b_s.T, cwid // C_GROUPS, axis=1).astype(F32)
    consts = (row(g), wc, wd, row(c_ln_g), row(c_ln_b), c_w_s.astype(BF16), sgb,
              d_conv_w.astype(F32), row(d_conv_b), row(d_ln_g), row(d_ln_b),
              w_out[:cwid].astype(BF16), w_out[cwid:].astype(BF16))
    tok = lambda i: (i, 0)
    return pl.pallas_call(
        functools.partial(_odd_kernel, tiles_per_seq=seq_len // tm),
        grid=(n // tm,),
        in_specs=[pl.BlockSpec((tm, d), tok)] + [_const_spec(c.shape) for c in consts],
        out_specs=pl.BlockSpec((tm, d), tok),
        out_shape=jax.ShapeDtypeStruct((n, d), F32),
        scratch_shapes=[pltpu.VMEM((tm + CONV_HALO + SUBLANES, dwid), F32)],
        compiler_params=_params("arbitrary"),
        name="odd_mixer",
    )(h, *consts)


def _logit_bound(gq, gk):
    unit = HEAD_DIM * HEAD_DIM ** -0.5 * math.log2(math.e) * BF16_SLACK
    return (unit * jnp.max(jnp.abs(gq)) * jnp.max(jnp.abs(gk))).astype(F32).reshape(1)


def _bounded_or_online(bound, call, *args):
    return lax.cond(bound[0] <= MAX_SHIFT,
                    lambda: call(bound, *args, bounded=True),
                    lambda: call(jnp.zeros_like(bound), *args, bounded=False))
def kernel(x, p, pos, ffn1_g, ffn1_wg, ffn1_wu, ffn1_wd, mix_g, ffn2_g, ffn2_wg, ffn2_wu, ffn2_wd, ple_g, ple_wgate, ple_wproj, ev_w_in, ev_w_out, a_q_g, a_k_g, a_w_uv, b_q_g, b_k_g, b_lam_q1, b_lam_k1, b_lam_q2, b_lam_k2, b_subln_g, od_w_in, od_w_out, c_ln_g, c_ln_b, c_w_s, c_b_s, d_conv_w, d_conv_b, d_ln_g, d_ln_b):
    bsz, s_len, d = x.shape
    n = bsz * s_len
    depth = p.shape[0]
    row = lambda v: v.reshape(1, -1).astype(F32)
    h = x.reshape(n, d)
    for layer in range(depth):
        ffn1 = (row(ffn1_g[layer]), ffn1_wg[layer].astype(BF16), ffn1_wu[layer].astype(BF16),
                ffn1_wd[layer].astype(BF16))
        ffn2 = (row(ffn2_g[layer]), ffn2_wg[layer].astype(BF16), ffn2_wu[layer].astype(BF16),
                ffn2_wd[layer].astype(BF16))
        ple = (p.reshape(depth, n, -1), layer, row(ple_g[layer]), ple_wgate[layer].astype(BF16),
               ple_wproj[layer].astype(BF16))
        h = _ffn_call(h, *ffn1)
        if layer % 2 == 0:
            e = layer // 2
            lam_init = 0.8 - 0.6 * math.exp(-0.3 * layer)
            aq, akd, bq, bk, iq, ikd, av, bv, iwt = _even_proj_call(
                h, pos, row(mix_g[layer]), ev_w_in[e], a_q_g[e], a_k_g[e], b_q_g[e], b_k_g[e])
            r3 = lambda a: a.reshape(bsz, s_len, a.shape[-1])
            lam_vecs = jnp.stack([b_lam_q1[e], b_lam_k1[e], b_lam_q2[e], b_lam_k2[e]]).astype(F32)
            a_args = (r3(aq), r3(iq), iwt, r3(akd), r3(av), r3(ikd), a_w_uv[e])
            b_args = (r3(bq), r3(bk), r3(bv), lam_vecs, row(b_subln_g[e]), lam_init)
            bound_a = _logit_bound(a_q_g[e], a_k_g[e])
            bound_b = _logit_bound(b_q_g[e], b_k_g[e])
            a_out, b_out = lax.cond(
                jnp.maximum(bound_a[0], bound_b[0]) <= MAX_SHIFT,
                lambda: _dsa_call(bound_a, *a_args, bounded=True, diff=(bound_b,) + b_args),
                lambda: (_bounded_or_online(bound_a, _dsa_call, *a_args),
                         _bounded_or_online(bound_b, _diff_call, *b_args)))
            a_w = A_HEADS * HEAD_DIM
            mix = (a_out.reshape(n, -1), b_out.reshape(n, -1),
                   ev_w_out[e][:a_w].astype(BF16), ev_w_out[e][a_w:].astype(BF16))
            h = _ffn_call(h, *ffn2, mix=mix, ple=ple)
        else:
            o = layer // 2
            h = _odd_call(h, s_len, mix_g[layer], od_w_in[o], od_w_out[o], c_ln_g[o], c_ln_b[o],
                          c_w_s[o], c_b_s[o], d_conv_w[o], d_conv_b[o], d_ln_g[o], d_ln_b[o])
            h = _ffn_call(h, *ffn2, ple=ple)
    return h.reshape(bsz, s_len, d)
```

```python
import functools
import math

import jax
import jax.numpy as jnp
from jax import lax
from jax.experimental import pallas as pl
from jax.experimental.pallas import tpu as pltpu

F32 = jnp.float32
BF16 = jnp.bfloat16
I32 = jnp.int32

EPS = 1e-6
HEAD_DIM = 64
CHUNK = 64
CHUNK_SHIFT = CHUNK.bit_length() - 1
ROPE_THETA = 10000.0
A_HEADS = 8
A_LAT = 128
IDX_HEADS = 4
TOPK_MAX = 256
B_HEADS = 4
C_GROUPS = 8
SGU_BLOCK = 128
CONV_W = 31
CONV_HALO = 32

LANES = 128
SUBLANES = 8
MXU_WIDTH = 256
INT_MIN = -(2 ** 31)
NEG_BIAS = -1e30
M_INIT = -1e29
MAX_SHIFT = 60.0
BF16_SLACK = 1.02

FF_CHUNK = 256
TM_FFN = 512
TM_PROJ = 512
TM_ODD = 256
TQ_DSA = 128
TK_DSA = 512
T_DIFF = 512
TIE_BLOCK = 256
WORD_BITS = 32
ATT_UNROLL = 4

VMEM_LIMIT_BYTES = 56 * 1024 * 1024


def _params(*sem):
    return pltpu.CompilerParams(dimension_semantics=sem, vmem_limit_bytes=VMEM_LIMIT_BYTES)


def _const_spec(shape):
    zeros = (0,) * len(shape)
    return pl.BlockSpec(shape, lambda *_: zeros)


def _rms(x, g):
    ms = jnp.mean(x * x, axis=-1, keepdims=True)
    return x * lax.rsqrt(ms + EPS) * g


def _layer_norm(x, g, b):
    mu = jnp.mean(x, axis=-1, keepdims=True)
    xc = x - mu
    var = jnp.mean(xc * xc, axis=-1, keepdims=True)
    return xc * lax.rsqrt(var + EPS) * g + b


def _grouped_loop(n, group_step, carry):
    done = 0
    unroll = ATT_UNROLL
    while unroll >= 1:
        def group(g, c, unroll=unroll, done=done):
            first = done + g * unroll
            return group_step([first + u for u in range(unroll)], c)

        n_groups = lax.shift_right_logical(n - done, unroll.bit_length() - 1)
        carry = lax.fori_loop(0, n_groups, group, carry)
        done = done + n_groups * unroll
        unroll //= 2
    return carry


def _unrolled_loop(n, step, carry):
    def group_step(ks, c):
        for k in ks:
            c = step(k, c)
        return c

    return _grouped_loop(n, group_step, carry)


def _tree_sum(xs):
    xs = list(xs)
    while len(xs) > 1:
        xs = [xs[i] + xs[i + 1] for i in range(0, len(xs) - 1, 2)] + ([xs[-1]] if len(xs) % 2 else [])
    return xs[0]


def _dot(a, b):
    return jnp.dot(a, b, preferred_element_type=F32)


def _dot_nt(a, b):
    return lax.dot_general(a, b, (((1,), (1,)), ((), ())), preferred_element_type=F32)


def _ffn_kernel(*refs, has_mix, has_ple, n_chunks):
    refs = list(refs)
    h_ref = refs.pop(0)
    if has_mix:
        a_ref, b_ref, woa_ref, wob_ref = refs[:4]
        refs = refs[4:]
    g_ref, wg_ref, wu_ref, wd_ref = refs[:4]
    refs = refs[4:]
    if has_ple:
        p_ref, pg_ref, wgate_ref, wproj_ref = refs[:4]
        refs = refs[4:]
    (o_ref,) = refs

    h = h_ref[...]
    if has_mix:
        h = h + _dot(a_ref[...], woa_ref[...]) + _dot(b_ref[...], wob_ref[...])
    xn = _rms(h, g_ref[...]).astype(BF16)
    acc = jnp.zeros(h.shape, F32)
    for c in range(n_chunks):
        gt = _dot(xn, wg_ref[:, c * FF_CHUNK:(c + 1) * FF_CHUNK])
        up = _dot(xn, wu_ref[:, c * FF_CHUNK:(c + 1) * FF_CHUNK])
        act = (gt * jax.nn.sigmoid(gt) * up).astype(BF16)
        acc = acc + _dot(act, wd_ref[c * FF_CHUNK:(c + 1) * FF_CHUNK, :])
    h = h + 0.5 * acc
    if has_ple:
        xg = _rms(h, pg_ref[...]).astype(BF16)
        gate = jax.nn.sigmoid(_dot(xg, wgate_ref[...]))
        h = h + gate * _dot(p_ref[...].astype(BF16), wproj_ref[...])
    o_ref[...] = h


def _ffn_call(h, g, wg, wu, wd, mix=None, ple=None):
    n, d = h.shape
    tm = min(TM_FFN, n)
    d_ff = wd.shape[0]
    n_chunks = d_ff // FF_CHUNK
    tok = lambda i: (i, 0)
    args = [h]
    specs = [pl.BlockSpec((tm, d), tok)]
    if mix is not None:
        a, b, woa, wob = mix
        args += [a, b, woa, wob]
        specs += [pl.BlockSpec((tm, a.shape[1]), tok), pl.BlockSpec((tm, b.shape[1]), tok),
                  _const_spec(woa.shape), _const_spec(wob.shape)]
    args += [g, wg, wu, wd]
    specs += [_const_spec(g.shape), _const_spec(wg.shape), _const_spec(wu.shape), _const_spec(wd.shape)]
    if ple is not None:
        p, layer, pg, wgate, wproj = ple
        args += [p, pg, wgate, wproj]
        specs += [pl.BlockSpec((None, tm, p.shape[2]), lambda i: (layer, i, 0)), _const_spec(pg.shape),
                  _const_spec(wgate.shape), _const_spec(wproj.shape)]
    kern = functools.partial(_ffn_kernel, has_mix=mix is not None, has_ple=ple is not None,
                             n_chunks=n_chunks)
    return pl.pallas_call(
        kern,
        grid=(n // tm,),
        in_specs=specs,
        out_specs=pl.BlockSpec((tm, d), tok),
        out_shape=jax.ShapeDtypeStruct((n, d), F32),
        compiler_params=_params("parallel"),
        name="ffn",
    )(*args)


_NR_WIDTHS = (A_HEADS * HEAD_DIM, 2 * HEAD_DIM, B_HEADS * 2 * HEAD_DIM, B_HEADS * 2 * HEAD_DIM)
_R_WIDTHS = (IDX_HEADS * HEAD_DIM, 2 * HEAD_DIM)
_P_WIDTHS = (A_LAT, B_HEADS * 2 * HEAD_DIM)
IW_ROWS = SUBLANES


def _even_proj_kernel(h_ref, pos_ref, g_ref, inv_ref, sgn_ref, wnr_ref, wr_ref, wp_ref, wiwt_ref,
                      gn_ref, gnr_ref, gmat_ref,
                      aq_ref, akd_ref, bq_ref, bk_ref, iq_ref, ikd_ref, av_ref, bv_ref, iwt_ref):
    xn = _rms(h_ref[...], g_ref[...]).astype(BF16)
    ang = pos_ref[...].astype(F32) * inv_ref[...]
    cos = jnp.cos(ang)
    sin = jnp.sin(ang) * sgn_ref[...]
    gmat = gmat_ref[...]
    first_half = lax.broadcasted_iota(I32, cos.shape, 1) % HEAD_DIM < HEAD_DIM // 2

    def swap_halves(xs):
        return jnp.where(first_half, pltpu.roll(xs, LANES - HEAD_DIM // 2, 1), pltpu.roll(xs, HEAD_DIM // 2, 1))

    off = 0
    for width, out_ref in zip(_NR_WIDTHS, (aq_ref, akd_ref, bq_ref, bk_ref)):
        x = _dot(xn, wnr_ref[:, off:off + width])
        mw = min(width, MXU_WIDTH)
        ms = jnp.concatenate(
            [_dot(jnp.square(x[:, c:c + mw]).astype(BF16), gmat[:mw, :mw]) for c in range(0, width, mw)], axis=1)
        for s in range(width // LANES):
            sl = slice(s * LANES, (s + 1) * LANES)
            gsl = slice(off + s * LANES, off + (s + 1) * LANES)
            xs = x[:, sl]
            y = (xs * (gn_ref[:, gsl] * cos) + swap_halves(xs) * (gnr_ref[:, gsl] * sin)) * lax.rsqrt(ms[:, sl] + EPS)
            out_ref[:, sl] = y.astype(out_ref.dtype)
        off += width

    off = 0
    for width, out_ref in zip(_R_WIDTHS, (iq_ref, ikd_ref)):
        x = _dot(xn, wr_ref[:, off:off + width])
        for s in range(width // LANES):
            xs = x[:, s * LANES:(s + 1) * LANES]
            out_ref[:, s * LANES:(s + 1) * LANES] = (xs * cos + swap_halves(xs) * sin).astype(out_ref.dtype)
        off += width

    off = 0
    for width, out_ref in zip(_P_WIDTHS, (av_ref, bv_ref)):
        out_ref[...] = _dot(xn, wp_ref[:, off:off + width]).astype(out_ref.dtype)
        off += width
    iwt_ref[...] = _dot_nt(wiwt_ref[...], xn)


def _swap_halves(w):
    lead = w.shape[:-1]
    w4 = w.reshape(*lead, -1, 2, HEAD_DIM // 2)
    return w4[..., ::-1, :].reshape(*lead, -1)


def _even_proj_call(h, pos, g, w_in, a_q_g, a_k_g, b_q_g, b_k_g):
    n, d = h.shape
    tm = min(TM_PROJ, n)
    hd = HEAD_DIM
    sizes = (A_HEADS * hd, hd, A_LAT, IDX_HEADS * hd, hd, IDX_HEADS,
             B_HEADS * 2 * hd, B_HEADS * 2 * hd, B_HEADS * 2 * hd)
    offs = [0]
    for s in sizes:
        offs.append(offs[-1] + s)
    w_aq, w_ak, w_av, w_iq, w_ik, w_iw, w_bq, w_bk, w_bv = (
        w_in[:, offs[i]:offs[i + 1]] for i in range(9))
    dup = lambda w: jnp.concatenate([w, w], axis=1)
    wnr = jnp.concatenate([w_aq, dup(w_ak), w_bq, w_bk], axis=1).astype(BF16)
    wr = jnp.concatenate([w_iq, dup(w_ik)], axis=1).astype(BF16)
    wp = jnp.concatenate([w_av, w_bv], axis=1).astype(BF16)
    wiwt = jnp.pad(w_iw.T, ((0, IW_ROWS - IDX_HEADS), (0, 0))).astype(BF16)

    scale = hd ** -0.5 * math.log2(math.e)
    gn = jnp.concatenate([jnp.tile(a_q_g, A_HEADS) * scale, jnp.tile(a_k_g, 2),
                          jnp.tile(b_q_g, 2 * B_HEADS) * scale, jnp.tile(b_k_g, 2 * B_HEADS)])
    gnr = _swap_halves(gn)
    gn = gn.reshape(1, -1).astype(F32)
    gnr = gnr.reshape(1, -1).astype(F32)

    inv = ROPE_THETA ** (-jnp.arange(0, hd, 2, dtype=F32) / hd)
    inv128 = jnp.tile(inv, LANES // (hd // 2)).reshape(1, LANES)
    sgn = jnp.tile(jnp.concatenate([-jnp.ones(hd // 2, F32), jnp.ones(hd // 2, F32)]),
                   LANES // hd).reshape(1, LANES)
    head_of_lane = jnp.arange(MXU_WIDTH) // hd
    gmat = ((head_of_lane[:, None] == head_of_lane[None, :]).astype(F32) / hd).astype(BF16)

    tok = lambda i: (i, 0)
    out_widths = (A_HEADS * hd, 2 * hd, B_HEADS * 2 * hd, B_HEADS * 2 * hd,
                  IDX_HEADS * hd, 2 * hd, A_LAT, B_HEADS * 2 * hd)
    consts = (g, inv128, sgn, wnr, wr, wp, wiwt, gn, gnr, gmat)
    return pl.pallas_call(
        _even_proj_kernel,
        grid=(n // tm,),
        in_specs=[pl.BlockSpec((tm, d), tok), pl.BlockSpec((tm, 1), tok)]
                 + [_const_spec(c.shape) for c in consts],
        out_specs=[pl.BlockSpec((tm, w), tok) for w in out_widths]
                  + [pl.BlockSpec((IW_ROWS, tm), lambda i: (0, i))],
        out_shape=[jax.ShapeDtypeStruct((n, w), BF16) for w in out_widths]
                  + [jax.ShapeDtypeStruct((IW_ROWS, n), F32)],
        compiler_params=_params("parallel"),
        name="even_proj",
    )(h, pos.reshape(n, 1), *consts)


def _bit_planes(words):
    a = list(words)
    j, m = WORD_BITS // 2, 0x0000FFFF
    while j:
        k = 0
        while k < WORD_BITS:
            t = (a[k] ^ lax.shift_right_logical(a[k + j], j)) & m
            a[k] = a[k] ^ t
            a[k + j] = a[k + j] ^ lax.shift_left(t, j)
            k = (k + j + 1) & ~j
        j >>= 1
        m ^= m << j
    return a


def _stack_heads(x, n_heads):
    lane = lax.broadcasted_iota(I32, (x.shape[0], LANES), 1)
    parts = []
    for hh in range(n_heads):
        slab = x[:, (hh // 2) * LANES:(hh // 2 + 1) * LANES]
        keep = (lane < HEAD_DIM) if hh % 2 == 0 else (lane >= HEAD_DIM)
        parts.append(jnp.where(keep, slab, jnp.zeros_like(slab)))
    return jnp.concatenate(parts, axis=0)


def _dsa_kernel(*refs, top_k, bounded, diff_lam_init=None):
    fused = diff_lam_init is not None
    refs = list(refs)
    shift_ref = refs.pop(0)
    if fused:
        assert bounded
        shiftb_ref, lam_ref, bq_ref, bk_ref, bv_ref, bg_ref = refs[:6]
        refs = refs[6:]
    aq_ref, iq_ref, iwt_ref, k_ref, v_ref, ik_ref, tri_ref, wuv_ref, o_ref = refs[:9]
    refs = refs[9:]
    if fused:
        ob_ref = refs.pop(0)
    key_sc, bias_sc, plane_sc, alive_sc, acc_sc = refs[:5]
    dacc_sc = refs[5] if fused else None
    tq = aq_ref.shape[0]
    tk = k_ref.shape[1]
    q0 = pl.program_id(1) * tq
    n_kb = (q0 + tq + tk - 1) // tk

    iq_all = _stack_heads(iq_ref[...], IDX_HEADS)
    iwt = iwt_ref[...]
    qchunk = (q0 + lax.broadcasted_iota(I32, (1, tq), 1)) >> CHUNK_SHIFT
    krow_chunk = lax.broadcasted_iota(I32, (tk, tq), 0) >> CHUNK_SHIFT
    groups = tk // (WORD_BITS * SUBLANES)
    n_blocks = key_sc.shape[0]

    def score_block(kb, masked):
        dots = jnp.maximum(_dot_nt(ik_ref[kb], iq_all), 0.0)
        sc = dots[:, 0:tq] * iwt[0:1, :]
        for hh in range(1, IDX_HEADS):
            sc = sc + dots[:, hh * tq:(hh + 1) * tq] * iwt[hh:hh + 1, :]
        bits = pltpu.bitcast(sc, I32)
        key = jnp.where(bits < 0, INT_MIN - bits, bits)
        if masked:
            key = jnp.where(krow_chunk <= qchunk - kb * (tk >> CHUNK_SHIFT), key, INT_MIN)
        key_sc[kb] = key
        for g in range(groups):
            base = g * WORD_BITS * SUBLANES
            planes = _bit_planes([key[base + SUBLANES * j:base + SUBLANES * (j + 1)] for j in range(WORD_BITS)])
            planes[0] = ~planes[0]
            for p in range(WORD_BITS):
                plane_sc[kb, p, SUBLANES * g:SUBLANES * (g + 1), :] = planes[p]
        alive_sc[kb] = jnp.full((SUBLANES * groups, tq), -1, I32)

    if fused:
        diff_group = _diff_bounded_group(shiftb_ref[0], bq_ref, bk_ref, bv_ref, dacc_sc, n_kb - 1)

        def fused_group(kbs, carry):
            diff_group(kbs)
            for kb in kbs:
                score_block(kb, False)
            return carry

        _grouped_loop(n_kb - 1, fused_group, 0)
    else:
        def score_step(kb, carry):
            score_block(kb, False)
            return carry

        _unrolled_loop(n_kb - 1, score_step, 0)
    score_block(n_kb - 1, True)

    def clear_body(kb, carry):
        plane_sc[kb] = jnp.zeros(plane_sc.shape[1:], I32)
        alive_sc[kb] = jnp.zeros(alive_sc.shape[1:], I32)
        return carry

    lax.fori_loop(n_kb, n_blocks, clear_body, 0)

    def radix_body(b, carry):
        t_u, k_rem, keep = carry
        first = jnp.where(b == 0, jnp.int32(-1), jnp.int32(0))
        b_prev = jnp.maximum(b - 1, 0)

        counts = []
        for kb in range(n_blocks):
            a = alive_sc[kb] & ((plane_sc[kb, b_prev] ^ keep) | first)
            alive_sc[kb] = a
            counts.append(lax.population_count(a & plane_sc[kb, b]))
        cnt = jnp.sum(_tree_sum(counts).astype(F32), axis=0, keepdims=True)
        take = cnt >= k_rem
        t_u = t_u | jnp.where(take, lax.shift_left(jnp.int32(1), 31 - b), 0)
        return t_u, jnp.where(take, k_rem, k_rem - cnt), jnp.where(take, 0, -1)

    init = (jnp.zeros((1, tq), I32), jnp.full((1, tq), float(top_k), F32), jnp.zeros((1, tq), I32))
    if fused:
        diff_group([n_kb - 1])
        carry = init
        for b in range(WORD_BITS):
            carry = radix_body(jnp.int32(b), carry)
        t_u, k_rem, keep = carry
        _diff_finish(dacc_sc[...], lam_ref, bg_ref, ob_ref, diff_lam_init)
    else:
        t_u, k_rem, keep = lax.fori_loop(0, WORD_BITS, radix_body, init)

    n_eq = jnp.zeros((SUBLANES * groups, tq), I32)
    for kb in range(n_blocks):
        n_eq = n_eq + lax.population_count(alive_sc[kb] & (plane_sc[kb, WORD_BITS - 1] ^ keep))
    n_eq = jnp.sum(n_eq.astype(F32), axis=0, keepdims=True)
    t = t_u ^ INT_MIN
    few = t == INT_MIN
    need = jnp.where(few, 0.0, k_rem)
    n_tied = jnp.sum(jnp.where(few | (n_eq == need), 0.0, 1.0))

    sel_bias = -shift_ref[0]

    def plain_bias():
        t_min = jnp.where(few, INT_MIN + 1, t)

        def body(kb, carry):
            bias_sc[kb] = jnp.where(key_sc[kb] >= t_min, sel_bias, NEG_BIAS).astype(BF16)
            return carry

        lax.fori_loop(0, n_kb, body, 0)

    def ranked_bias():
        tri = tri_ref[...]

        def body(kb, carry):
            kk_all = key_sc[kb]
            for c in range(tk // TIE_BLOCK):
                kk = kk_all[c * TIE_BLOCK:(c + 1) * TIE_BLOCK]
                eq = kk == t
                rank = _dot(tri, jnp.where(eq, 1.0, 0.0).astype(BF16)) + carry
                tie_bias = jnp.where(rank <= need, sel_bias, NEG_BIAS)
                bias = jnp.where(kk > t, sel_bias, jnp.where(eq, tie_bias, NEG_BIAS))
                bias_sc[kb, c * TIE_BLOCK:(c + 1) * TIE_BLOCK, :] = bias.astype(BF16)
                carry = rank[TIE_BLOCK - 1:TIE_BLOCK, :]
            return carry

        lax.fori_loop(0, n_kb, body, jnp.zeros((1, tq), F32))

    lax.cond(n_tied == 0.0, plain_bias, ranked_bias)

    rows = A_HEADS * tq
    eye = lax.broadcasted_iota(I32, (tq, tq), 0) == lax.broadcasted_iota(I32, (tq, tq), 1)
    eye = jnp.where(eye, 1.0, 0.0).astype(BF16)
    qe = jnp.concatenate([_stack_heads(aq_ref[...], A_HEADS),
                          jnp.concatenate([eye] * A_HEADS, axis=0)], axis=1)
    ones = jnp.ones((tk, LANES), BF16)

    def logits(kb):
        return _dot_nt(qe, jnp.concatenate([k_ref[kb], bias_sc[kb]], axis=1))

    def values(kb):
        return jnp.concatenate([v_ref[kb], ones], axis=1)

    acc0 = jnp.zeros((rows, A_LAT + LANES), F32)
    if bounded:
        def att_group(kbs, carry):
            p = jnp.concatenate([jnp.exp2(logits(kb)).astype(BF16) for kb in kbs], axis=1)
            acc_sc[...] += _dot(p, jnp.concatenate([values(kb) for kb in kbs], axis=0))
            return carry

        acc_sc[...] = acc0
        _grouped_loop(n_kb, att_group, 0)
        acc = acc_sc[...]
    else:
        def softmax_step(kb, m, acc, s):
            m_new = jnp.maximum(m, jnp.max(s, axis=1, keepdims=True))
            p = jnp.exp2(s - m_new).astype(BF16)
            return m_new, jnp.exp2(m - m_new) * acc + _dot(p, values(kb))

        def att_body(kb, carry):
            m, acc, s = carry
            s_next = logits(kb + 1)
            m, acc = softmax_step(kb, m, acc, s)
            return m, acc, s_next

        m, acc, s = lax.fori_loop(0, n_kb - 1, att_body, (jnp.full((rows, 1), M_INIT, F32), acc0, logits(0)))
        _, acc = softmax_step(n_kb - 1, m, acc, s)
    o_lat = (acc[:, :A_LAT] / acc[:, A_LAT:]).astype(BF16)
    o_all = jnp.concatenate([o_lat[hh * tq:(hh + 1) * tq] for hh in range(A_HEADS)], axis=1)
    o_ref[...] = _dot(o_all, wuv_ref[...]).astype(o_ref.dtype)


def _dsa_call(shift, aq, iq, iwt, akd, av, ikd, w_uv, *, bounded, diff=None):
    b, s, _ = aq.shape
    tq = min(TQ_DSA, s)
    tk = min(TK_DSA, s)
    assert tq == LANES, "the one-hot mask columns pair one query tile with 128 contraction lanes"
    top_k = min(TOPK_MAX, s // 4)
    n_kb = s // tk
    nq = s // tq
    blk = lambda x: x.reshape(b, n_kb, tk, x.shape[-1])
    tri = (jnp.arange(TIE_BLOCK)[:, None] >= jnp.arange(TIE_BLOCK)[None, :]).astype(BF16)
    eye = jnp.eye(A_HEADS, dtype=w_uv.dtype)
    wuv_bd = (w_uv[:, :, None, :] * eye[:, None, :, None]).reshape(A_HEADS * A_LAT, A_HEADS * HEAD_DIM)
    wuv_bd = wuv_bd.astype(BF16)

    qmap = lambda bi, i: (bi, i, 0)
    kvmap = lambda bi, i: (bi, 0, 0, 0)
    kv_spec = pl.BlockSpec((None, n_kb, tk, LANES), kvmap)
    args = [shift]
    in_specs = [pl.BlockSpec(memory_space=pltpu.SMEM)]
    out_specs = [pl.BlockSpec((None, tq, A_HEADS * HEAD_DIM), qmap)]
    out_shape = [jax.ShapeDtypeStruct((b, s, A_HEADS * HEAD_DIM), BF16)]
    scratch = [pltpu.VMEM((n_kb, tk, tq), I32), pltpu.VMEM((n_kb, tk, tq), BF16),
               pltpu.VMEM((n_kb, WORD_BITS, tk // WORD_BITS, tq), I32),
               pltpu.VMEM((n_kb, tk // WORD_BITS, tq), I32),
               pltpu.VMEM((A_HEADS * tq, A_LAT + LANES), F32)]
    lam_init = None
    if diff is not None:
        shift_b, bq, bk, bv, lam_vecs, subln_g, lam_init = diff
        assert B_HEADS * tq == tk and bq.shape[-1] == B_HEADS * LANES
        dqmap = lambda bi, i: (bi, i // B_HEADS, i % B_HEADS)
        dkvmap = lambda bi, i: (bi, 0, 0, i % B_HEADS)
        args += [shift_b, lam_vecs, bq, bk.reshape(b, n_kb, tk, -1), bv.reshape(b, n_kb, tk, -1), subln_g]
        in_specs += [pl.BlockSpec(memory_space=pltpu.SMEM), _const_spec(lam_vecs.shape),
                     pl.BlockSpec((None, tk, LANES), dqmap),
                     pl.BlockSpec((None, n_kb, tk, LANES), dkvmap),
                     pl.BlockSpec((None, n_kb, tk, LANES), dkvmap),
                     _const_spec(subln_g.shape)]
        out_specs.append(pl.BlockSpec((None, tk, LANES), dqmap))
        out_shape.append(jax.ShapeDtypeStruct(bq.shape, BF16))
        scratch.append(pltpu.VMEM((2 * tk, 2 * LANES), F32))
    args += [aq, iq, iwt, blk(akd), blk(av), blk(ikd), tri, wuv_bd]
    in_specs += [pl.BlockSpec((None, tq, aq.shape[-1]), qmap),
                 pl.BlockSpec((None, tq, iq.shape[-1]), qmap),
                 pl.BlockSpec((iwt.shape[0], tq), lambda bi, i: (0, bi * nq + i)),
                 kv_spec, kv_spec, kv_spec,
                 _const_spec(tri.shape), _const_spec(wuv_bd.shape)]
    outs = pl.pallas_call(
        functools.partial(_dsa_kernel, top_k=top_k, bounded=bounded, diff_lam_init=lam_init),
        grid=(b, nq),
        in_specs=in_specs,
        out_specs=out_specs,
        out_shape=out_shape,
        scratch_shapes=scratch,
        compiler_params=_params("parallel", "arbitrary"),
        name="dsa" if diff is None else "dsa_diff",
    )(*args)
    return outs[0] if diff is None else tuple(outs)


def _diff_split_maps(q):
    lane = lax.broadcasted_iota(I32, q.shape, 1)
    zero = jnp.zeros_like(q)
    return jnp.concatenate([jnp.where(lane < HEAD_DIM, q, zero), jnp.where(lane >= HEAD_DIM, q, zero)], axis=0)


def _diff_bounded_group(shift, q_ref, k_ref, v_ref, acc_sc, i):
    t = q_ref.shape[0]
    assert t // CHUNK < LANES
    q_lane = lax.broadcasted_iota(I32, (2 * t, LANES), 1)
    q_chunk = lax.broadcasted_iota(I32, (2 * t, LANES), 0)
    q_chunk = jnp.where(q_chunk >= t, q_chunk - t, q_chunk) >> CHUNK_SHIFT
    q_extra = jnp.where(q_lane == 0, 1.0, jnp.where(q_lane - 1 > q_chunk, NEG_BIAS, 0.0))
    q2 = jnp.concatenate([_diff_split_maps(q_ref[...]), q_extra.astype(BF16)], axis=1)
    k_lane = lax.broadcasted_iota(I32, (t, LANES), 1)
    k_chunk = lax.broadcasted_iota(I32, (t, LANES), 0) >> CHUNK_SHIFT
    k_plain = jnp.where(k_lane == 0, -shift, 0.0)
    k_diag = jnp.where(k_lane - 1 == k_chunk, 1.0, k_plain).astype(BF16)
    k_plain = k_plain.astype(BF16)
    ones = jnp.ones((t, LANES), BF16)

    def probs(j):
        k_extra = jnp.where(j == i, k_diag, k_plain)
        return jnp.exp2(_dot_nt(q2, jnp.concatenate([k_ref[j], k_extra], axis=1))).astype(BF16)

    def group(js):
        p = jnp.concatenate([probs(j) for j in js], axis=1)
        acc_sc[...] += _dot(p, jnp.concatenate([jnp.concatenate([v_ref[j], ones], axis=1) for j in js], axis=0))

    acc_sc[...] = jnp.zeros(acc_sc.shape, F32)
    return group


def _diff_finish(acc, lam_ref, g_ref, o_ref, lam_init):
    t = acc.shape[0] // 2
    vdim = o_ref.shape[-1]
    lam_v = lam_ref[...]
    lam = (jnp.exp(jnp.sum(lam_v[0:1] * lam_v[1:2], axis=1, keepdims=True))
           - jnp.exp(jnp.sum(lam_v[2:3] * lam_v[3:4], axis=1, keepdims=True)) + lam_init)
    o = acc[:, :vdim] / acc[:, vdim:]
    o = o[:t] - lam * o[t:]
    o_ref[...] = (_rms(o, g_ref[...]) * (1.0 - lam_init)).astype(o_ref.dtype)


def _diff_kernel(shift_ref, lam_ref, q_ref, k_ref, v_ref, g_ref, o_ref, acc_sc, *, lam_init, bounded):
    t = q_ref.shape[0]
    i = pl.program_id(2)
    vdim = v_ref.shape[-1]

    if bounded:
        group = _diff_bounded_group(shift_ref[0], q_ref, k_ref, v_ref, acc_sc, i)

        def group_step(js, carry):
            group(js)
            return carry

        _grouped_loop(i + 1, group_step, 0)
        acc = acc_sc[...]
    else:
        q2 = _diff_split_maps(q_ref[...])
        ones = jnp.ones((t, LANES), BF16)

        def values(j):
            return jnp.concatenate([v_ref[j], ones], axis=1)

        acc0 = jnp.zeros((2 * t, vdim + LANES), F32)
        qc = lax.broadcasted_iota(I32, (2 * t, t), 0)
        qc = jnp.where(qc >= t, qc - t, qc) >> CHUNK_SHIFT
        kc = lax.broadcasted_iota(I32, (2 * t, t), 1) >> CHUNK_SHIFT

        def logits(j):
            return _dot_nt(q2, k_ref[j])

        def softmax_step(j, m, acc, s):
            m_new = jnp.maximum(m, jnp.max(s, axis=1, keepdims=True))
            p = jnp.exp2(s - m_new).astype(BF16)
            return m_new, jnp.exp2(m - m_new) * acc + _dot(p, values(j))

        def body(j, carry):
            m, acc, s = carry
            s_next = logits(j + 1)
            m, acc = softmax_step(j, m, acc, s)
            return m, acc, s_next

        m, acc, s = lax.fori_loop(0, i, body, (jnp.full((2 * t, 1), M_INIT, F32), acc0, logits(0)))
        _, acc = softmax_step(i, m, acc, jnp.where(kc <= qc, s, NEG_BIAS))
    _diff_finish(acc, lam_ref, g_ref, o_ref, lam_init)


def _diff_call(shift, bq, bk, bv, lam_vecs, subln_g, lam_init, *, bounded):
    b, s, w = bq.shape
    t = min(T_DIFF, s)
    nb = s // t
    k4 = bk.reshape(b, nb, t, w)
    v4 = bv.reshape(b, nb, t, w)
    qmap = lambda bi, hh, i: (bi, i, hh)
    kvmap = lambda bi, hh, i: (bi, 0, 0, hh)
    return pl.pallas_call(
        functools.partial(_diff_kernel, lam_init=lam_init, bounded=bounded),
        grid=(b, B_HEADS, nb),
        in_specs=[pl.BlockSpec(memory_space=pltpu.SMEM),
                  _const_spec(lam_vecs.shape),
                  pl.BlockSpec((None, t, LANES), qmap),
                  pl.BlockSpec((None, nb, t, LANES), kvmap),
                  pl.BlockSpec((None, nb, t, LANES), kvmap),
                  _const_spec(subln_g.shape)],
        out_specs=pl.BlockSpec((None, t, LANES), qmap),
        out_shape=jax.ShapeDtypeStruct((b, s, w), BF16),
        scratch_shapes=[pltpu.VMEM((2 * t, 2 * LANES), F32)],
        compiler_params=_params("parallel", "parallel", "arbitrary"),
        name="diff_attn",
    )(shift, lam_vecs, bq, k4, v4, subln_g)


def _odd_kernel(h_ref, g_ref, wc_ref, wd_ref, clg_ref, clb_ref, ws_ref, sb_ref,
                cw_ref, cb_ref, dlg_ref, dlb_ref, woc_ref, wod_ref, o_ref, buf_sc, *, tiles_per_seq):
    tm = h_ref.shape[0]
    cwid = clg_ref.shape[1]

    h = h_ref[...]
    xn = _rms(h, g_ref[...]).astype(BF16)

    zc = jax.nn.gelu(_dot(xn, wc_ref[...]), approximate=True)
    u = zc[:, :cwid]
    v = _layer_norm(zc[:, cwid:], clg_ref[...], clb_ref[...]).astype(BF16)
    pi = lax.broadcasted_iota(I32, (SGU_BLOCK, SGU_BLOCK), 0) >> CHUNK_SHIFT
    pj = lax.broadcasted_iota(I32, (SGU_BLOCK, SGU_BLOCK), 1) >> CHUNK_SHIFT
    causal = pi >= pj
    w_s = [jnp.where(causal, ws_ref[gi], jnp.zeros((SGU_BLOCK, SGU_BLOCK), BF16)) for gi in range(C_GROUPS)]
    lane = lax.broadcasted_iota(I32, (SGU_BLOCK, LANES), 1)
    sgb = sb_ref[...]
    rows_out = []
    for r in range(tm // SGU_BLOCK):
        rs = slice(r * SGU_BLOCK, (r + 1) * SGU_BLOCK)
        slabs = []
        for pr in range(cwid // LANES):
            vs = v[rs, pr * LANES:(pr + 1) * LANES]
            lo = _dot(w_s[2 * pr], vs)
            hi = _dot(w_s[2 * pr + 1], vs)
            slabs.append(jnp.where(lane < HEAD_DIM, lo, hi))
        rows_out.append(jnp.concatenate(slabs, axis=1) + sgb)
    c_out = (u * jnp.concatenate(rows_out, axis=0)).astype(BF16)

    zd = _dot(xn, wd_ref[...])
    dwid = zd.shape[1] // 2
    hd = zd[:, :dwid] * jax.nn.sigmoid(zd[:, dwid:])

    @pl.when(pl.program_id(0) % tiles_per_seq == 0)
    def _():
        buf_sc[0:CONV_HALO, :] = jnp.zeros((CONV_HALO, dwid), F32)

    buf_sc[CONV_HALO:CONV_HALO + tm, :] = hd
    buf_sc[CONV_HALO + tm:, :] = jnp.zeros((SUBLANES, dwid), F32)
    conv = jnp.zeros((tm, dwid), F32) + cb_ref[...]
    base = CONV_HALO - (CONV_W - 1)
    for r in range(SUBLANES):
        group = None
        for w in range(CONV_W):
            if (base + w) % SUBLANES == r:
                a = (base + w) // SUBLANES * SUBLANES
                term = buf_sc[a:a + tm + SUBLANES, :] * cw_ref[w:w + 1, :]
                group = term if group is None else group + term
        conv = conv + group[r:r + tm]
    buf_sc[0:CONV_HALO, :] = buf_sc[tm:tm + CONV_HALO, :]
    d_out = _layer_norm(conv, dlg_ref[...], dlb_ref[...])
    d_out = (d_out * jax.nn.sigmoid(d_out)).astype(BF16)

    o_ref[...] = h + _dot(c_out, woc_ref[...]) + _dot(d_out, wod_ref[...])


def _odd_call(h, seq_len, g, w_in, w_out, c_ln_g, c_ln_b, c_w_s, c_b_s, d_conv_w, d_conv_b, d_ln_g, d_ln_b):
    n, d = h.shape
    tm = min(TM_ODD, seq_len)
    cwid = c_ln_g.shape[0]
    dwid = d_ln_g.shape[0]
    row = lambda x: x.reshape(1, -1).astype(F32)
    wc = w_in[:, :2 * cwid].astype(BF16)
    wd = w_in[:, 2 * cwid:].astype(BF16)
    sgb = jnp.repeat(c_b_s.T, cwid // C_GROUPS, axis=1).astype(F32)
    consts = (row(g), wc, wd, row(c_ln_g), row(c_ln_b), c_w_s.astype(BF16), sgb,
              d_conv_w.astype(F32), row(d_conv_b), row(d_ln_g), row(d_ln_b),
              w_out[:cwid].astype(BF16), w_out[cwid:].astype(BF16))
    tok = lambda i: (i, 0)
    return pl.pallas_call(
        functools.partial(_odd_kernel, tiles_per_seq=seq_len // tm),
        grid=(n // tm,),
        in_specs=[pl.BlockSpec((tm, d), tok)] + [_const_spec(c.shape) for c in consts],
        out_specs=pl.BlockSpec((tm, d), tok),
        out_shape=jax.ShapeDtypeStruct((n, d), F32),
        scratch_shapes=[pltpu.VMEM((tm + CONV_HALO + SUBLANES, dwid), F32)],
        compiler_params=_params("arbitrary"),
        name="odd_mixer",
    )(h, *consts)


def _logit_bound(gq, gk):
    unit = HEAD_DIM * HEAD_DIM ** -0.5 * math.log2(math.e) * BF16_SLACK
    return (unit * jnp.max(jnp.abs(gq)) * jnp.max(jnp.abs(gk))).astype(F32).reshape(1)


def _bounded_or_online(bound, call, *args):
    return lax.cond(bound[0] <= MAX_SHIFT,
                    lambda: call(bound, *args, bounded=True),
                    lambda: call(jnp.zeros_like(bound), *args, bounded=False))
def kernel(x, p, pos, ffn1_g, ffn1_wg, ffn1_wu, ffn1_wd, mix_g, ffn2_g, ffn2_wg, ffn2_wu, ffn2_wd, ple_g, ple_wgate, ple_wproj, ev_w_in, ev_w_out, a_q_g, a_k_g, a_w_uv, b_q_g, b_k_g, b_lam_q1, b_lam_k1, b_lam_q2, b_lam_k2, b_subln_g, od_w_in, od_w_out, c_ln_g, c_ln_b, c_w_s, c_b_s, d_conv_w, d_conv_b, d_ln_g, d_ln_b):
    bsz, s_len, d = x.shape
    n = bsz * s_len
    depth = p.shape[0]
    row = lambda v: v.reshape(1, -1).astype(F32)
    h = x.reshape(n, d)
    for layer in range(depth):
        ffn1 = (row(ffn1_g[layer]), ffn1_wg[layer].astype(BF16), ffn1_wu[layer].astype(BF16),
                ffn1_wd[layer].astype(BF16))
        ffn2 = (row(ffn2_g[layer]), ffn2_wg[layer].astype(BF16), ffn2_wu[layer].astype(BF16),
                ffn2_wd[layer].astype(BF16))
        ple = (p.reshape(depth, n, -1), layer, row(ple_g[layer]), ple_wgate[layer].astype(BF16),
               ple_wproj[layer].astype(BF16))
        h = _ffn_call(h, *ffn1)
        if layer % 2 == 0:
            e = layer // 2
            lam_init = 0.8 - 0.6 * math.exp(-0.3 * layer)
            aq, akd, bq, bk, iq, ikd, av, bv, iwt = _even_proj_call(
                h, pos, row(mix_g[layer]), ev_w_in[e], a_q_g[e], a_k_g[e], b_q_g[e], b_k_g[e])
            r3 = lambda a: a.reshape(bsz, s_len, a.shape[-1])
            lam_vecs = jnp.stack([b_lam_q1[e], b_lam_k1[e], b_lam_q2[e], b_lam_k2[e]]).astype(F32)
            a_args = (r3(aq), r3(iq), iwt, r3(akd), r3(av), r3(ikd), a_w_uv[e])
            b_args = (r3(bq), r3(bk), r3(bv), lam_vecs, row(b_subln_g[e]), lam_init)
            bound_a = _logit_bound(a_q_g[e], a_k_g[e])
            bound_b = _logit_bound(b_q_g[e], b_k_g[e])
            a_out, b_out = lax.cond(
                jnp.maximum(bound_a[0], bound_b[0]) <= MAX_SHIFT,
                lambda: _dsa_call(bound_a, *a_args, bounded=True, diff=(bound_b,) + b_args),
                lambda: (_bounded_or_online(bound_a, _dsa_call, *a_args),
                         _bounded_or_online(bound_b, _diff_call, *b_args)))
            a_w = A_HEADS * HEAD_DIM
            mix = (a_out.reshape(n, -1), b_out.reshape(n, -1),
                   ev_w_out[e][:a_w].astype(BF16), ev_w_out[e][a_w:].astype(BF16))
            h = _ffn_call(h, *ffn2, mix=mix, ple=ple)
        else:
            o = layer // 2
            h = _odd_call(h, s_len, mix_g[layer], od_w_in[o], od_w_out[o], c_ln_g[o], c_ln_b[o],
                          c_w_s[o], c_b_s[o], d_conv_w[o], d_conv_b[o], d_ln_g[o], d_ln_b[o])
            h = _ffn_call(h, *ffn2, ple=ple)
    return h.reshape(bsz, s_len, d)
```

```python
import functools
import math

import jax
import jax.numpy as jnp
from jax import lax
from jax.experimental import pallas as pl
from jax.experimental.pallas import tpu as pltpu

F32 = jnp.float32
BF16 = jnp.bfloat16
I32 = jnp.int32

EPS = 1e-6
HEAD_DIM = 64
CHUNK = 64
CHUNK_SHIFT = CHUNK.bit_length() - 1
ROPE_THETA = 10000.0
A_HEADS = 8
A_LAT = 128
IDX_HEADS = 4
TOPK_MAX = 256
B_HEADS = 4
C_GROUPS = 8
SGU_BLOCK = 128
CONV_W = 31
CONV_HALO = 32

LANES = 128
SUBLANES = 8
MXU_WIDTH = 256
INT_MIN = -(2 ** 31)
NEG_BIAS = -1e30
M_INIT = -1e29
MAX_SHIFT = 60.0
BF16_SLACK = 1.02

FF_CHUNK = 256
TM_FFN = 512
TM_PROJ = 512
TM_ODD = 512
TQ_DSA = 128
TK_DSA = 512
T_DIFF = 512
TIE_BLOCK = 256
WORD_BITS = 32
ATT_UNROLL = 4

VMEM_LIMIT_BYTES = 56 * 1024 * 1024


def _params(*sem):
    return pltpu.CompilerParams(dimension_semantics=sem, vmem_limit_bytes=VMEM_LIMIT_BYTES)


def _const_spec(shape):
    zeros = (0,) * len(shape)
    return pl.BlockSpec(shape, lambda *_: zeros)


def _rms(x, g):
    ms = jnp.mean(x * x, axis=-1, keepdims=True)
    return x * lax.rsqrt(ms + EPS) * g


def _layer_norm(x, g, b):
    mu = jnp.mean(x, axis=-1, keepdims=True)
    xc = x - mu
    var = jnp.mean(xc * xc, axis=-1, keepdims=True)
    return xc * lax.rsqrt(var + EPS) * g + b


def _grouped_loop(n, group_step, carry):
    done = 0
    unroll = ATT_UNROLL
    while unroll >= 1:
        def group(g, c, unroll=unroll, done=done):
            first = done + g * unroll
            return group_step([first + u for u in range(unroll)], c)

        n_groups = lax.shift_right_logical(n - done, unroll.bit_length() - 1)
        carry = lax.fori_loop(0, n_groups, group, carry)
        done = done + n_groups * unroll
        unroll //= 2
    return carry


def _unrolled_loop(n, step, carry):
    def group_step(ks, c):
        for k in ks:
            c = step(k, c)
        return c

    return _grouped_loop(n, group_step, carry)


def _tree_sum(xs):
    xs = list(xs)
    while len(xs) > 1:
        xs = [xs[i] + xs[i + 1] for i in range(0, len(xs) - 1, 2)] + ([xs[-1]] if len(xs) % 2 else [])
    return xs[0]


def _dot(a, b):
    return jnp.dot(a, b, preferred_element_type=F32)


def _dot_nt(a, b):
    return lax.dot_general(a, b, (((1,), (1,)), ((), ())), preferred_element_type=F32)


def _ffn_kernel(*refs, has_mix, has_ple, n_chunks):
    refs = list(refs)
    h_ref = refs.pop(0)
    if has_mix:
        a_ref, b_ref, woa_ref, wob_ref = refs[:4]
        refs = refs[4:]
    g_ref, wg_ref, wu_ref, wd_ref = refs[:4]
    refs = refs[4:]
    if has_ple:
        p_ref, pg_ref, wgate_ref, wproj_ref = refs[:4]
        refs = refs[4:]
    (o_ref,) = refs

    h = h_ref[...]
    if has_mix:
        h = h + _dot(a_ref[...], woa_ref[...]) + _dot(b_ref[...], wob_ref[...])
    xn = _rms(h, g_ref[...]).astype(BF16)
    acc = jnp.zeros(h.shape, F32)
    for c in range(n_chunks):
        gt = _dot(xn, wg_ref[:, c * FF_CHUNK:(c + 1) * FF_CHUNK])
        up = _dot(xn, wu_ref[:, c * FF_CHUNK:(c + 1) * FF_CHUNK])
        act = (gt * jax.nn.sigmoid(gt) * up).astype(BF16)
        acc = acc + _dot(act, wd_ref[c * FF_CHUNK:(c + 1) * FF_CHUNK, :])
    h = h + 0.5 * acc
    if has_ple:
        xg = _rms(h, pg_ref[...]).astype(BF16)
        gate = jax.nn.sigmoid(_dot(xg, wgate_ref[...]))
        h = h + gate * _dot(p_ref[...].astype(BF16), wproj_ref[...])
    o_ref[...] = h


def _ffn_call(h, g, wg, wu, wd, mix=None, ple=None):
    n, d = h.shape
    tm = min(TM_FFN, n)
    d_ff = wd.shape[0]
    n_chunks = d_ff // FF_CHUNK
    tok = lambda i: (i, 0)
    args = [h]
    specs = [pl.BlockSpec((tm, d), tok)]
    if mix is not None:
        a, b, woa, wob = mix
        args += [a, b, woa, wob]
        specs += [pl.BlockSpec((tm, a.shape[1]), tok), pl.BlockSpec((tm, b.shape[1]), tok),
                  _const_spec(woa.shape), _const_spec(wob.shape)]
    args += [g, wg, wu, wd]
    specs += [_const_spec(g.shape), _const_spec(wg.shape), _const_spec(wu.shape), _const_spec(wd.shape)]
    if ple is not None:
        p, layer, pg, wgate, wproj = ple
        args += [p, pg, wgate, wproj]
        specs += [pl.BlockSpec((None, tm, p.shape[2]), lambda i: (layer, i, 0)), _const_spec(pg.shape),
                  _const_spec(wgate.shape), _const_spec(wproj.shape)]
    kern = functools.partial(_ffn_kernel, has_mix=mix is not None, has_ple=ple is not None,
                             n_chunks=n_chunks)
    return pl.pallas_call(
        kern,
        grid=(n // tm,),
        in_specs=specs,
        out_specs=pl.BlockSpec((tm, d), tok),
        out_shape=jax.ShapeDtypeStruct((n, d), F32),
        compiler_params=_params("parallel"),
        name="ffn",
    )(*args)


_NR_WIDTHS = (A_HEADS * HEAD_DIM, 2 * HEAD_DIM, B_HEADS * 2 * HEAD_DIM, B_HEADS * 2 * HEAD_DIM)
_R_WIDTHS = (IDX_HEADS * HEAD_DIM, 2 * HEAD_DIM)
_P_WIDTHS = (A_LAT, B_HEADS * 2 * HEAD_DIM)
IW_ROWS = SUBLANES


def _even_proj_kernel(h_ref, pos_ref, g_ref, inv_ref, sgn_ref, wnr_ref, wr_ref, wp_ref, wiwt_ref,
                      gn_ref, gnr_ref, gmat_ref,
                      aq_ref, akd_ref, bq_ref, bk_ref, iq_ref, ikd_ref, av_ref, bv_ref, iwt_ref):
    xn = _rms(h_ref[...], g_ref[...]).astype(BF16)
    ang = pos_ref[...].astype(F32) * inv_ref[...]
    cos = jnp.cos(ang)
    sin = jnp.sin(ang) * sgn_ref[...]
    gmat = gmat_ref[...]
    first_half = lax.broadcasted_iota(I32, cos.shape, 1) % HEAD_DIM < HEAD_DIM // 2

    def swap_halves(xs):
        return jnp.where(first_half, pltpu.roll(xs, LANES - HEAD_DIM // 2, 1), pltpu.roll(xs, HEAD_DIM // 2, 1))

    off = 0
    for width, out_ref in zip(_NR_WIDTHS, (aq_ref, akd_ref, bq_ref, bk_ref)):
        x = _dot(xn, wnr_ref[:, off:off + width])
        mw = min(width, MXU_WIDTH)
        ms = jnp.concatenate(
            [_dot(jnp.square(x[:, c:c + mw]).astype(BF16), gmat[:mw, :mw]) for c in range(0, width, mw)], axis=1)
        for s in range(width // LANES):
            sl = slice(s * LANES, (s + 1) * LANES)
            gsl = slice(off + s * LANES, off + (s + 1) * LANES)
            xs = x[:, sl]
            y = (xs * (gn_ref[:, gsl] * cos) + swap_halves(xs) * (gnr_ref[:, gsl] * sin)) * lax.rsqrt(ms[:, sl] + EPS)
            out_ref[:, sl] = y.astype(out_ref.dtype)
        off += width

    off = 0
    for width, out_ref in zip(_R_WIDTHS, (iq_ref, ikd_ref)):
        x = _dot(xn, wr_ref[:, off:off + width])
        for s in range(width // LANES):
            xs = x[:, s * LANES:(s + 1) * LANES]
            out_ref[:, s * LANES:(s + 1) * LANES] = (xs * cos + swap_halves(xs) * sin).astype(out_ref.dtype)
        off += width

    off = 0
    for width, out_ref in zip(_P_WIDTHS, (av_ref, bv_ref)):
        out_ref[...] = _dot(xn, wp_ref[:, off:off + width]).astype(out_ref.dtype)
        off += width
    iwt_ref[...] = _dot_nt(wiwt_ref[...], xn)


def _swap_halves(w):
    lead = w.shape[:-1]
    w4 = w.reshape(*lead, -1, 2, HEAD_DIM // 2)
    return w4[..., ::-1, :].reshape(*lead, -1)


def _even_proj_call(h, pos, g, w_in, a_q_g, a_k_g, b_q_g, b_k_g):
    n, d = h.shape
    tm = min(TM_PROJ, n)
    hd = HEAD_DIM
    sizes = (A_HEADS * hd, hd, A_LAT, IDX_HEADS * hd, hd, IDX_HEADS,
             B_HEADS * 2 * hd, B_HEADS * 2 * hd, B_HEADS * 2 * hd)
    offs = [0]
    for s in sizes:
        offs.append(offs[-1] + s)
    w_aq, w_ak, w_av, w_iq, w_ik, w_iw, w_bq, w_bk, w_bv = (
        w_in[:, offs[i]:offs[i + 1]] for i in range(9))
    dup = lambda w: jnp.concatenate([w, w], axis=1)
    wnr = jnp.concatenate([w_aq, dup(w_ak), w_bq, w_bk], axis=1).astype(BF16)
    wr = jnp.concatenate([w_iq, dup(w_ik)], axis=1).astype(BF16)
    wp = jnp.concatenate([w_av, w_bv], axis=1).astype(BF16)
    wiwt = jnp.pad(w_iw.T, ((0, IW_ROWS - IDX_HEADS), (0, 0))).astype(BF16)

    scale = hd ** -0.5 * math.log2(math.e)
    gn = jnp.concatenate([jnp.tile(a_q_g, A_HEADS) * scale, jnp.tile(a_k_g, 2),
                          jnp.tile(b_q_g, 2 * B_HEADS) * scale, jnp.tile(b_k_g, 2 * B_HEADS)])
    gnr = _swap_halves(gn)
    gn = gn.reshape(1, -1).astype(F32)
    gnr = gnr.reshape(1, -1).astype(F32)

    inv = ROPE_THETA ** (-jnp.arange(0, hd, 2, dtype=F32) / hd)
    inv128 = jnp.tile(inv, LANES // (hd // 2)).reshape(1, LANES)
    sgn = jnp.tile(jnp.concatenate([-jnp.ones(hd // 2, F32), jnp.ones(hd // 2, F32)]),
                   LANES // hd).reshape(1, LANES)
    head_of_lane = jnp.arange(MXU_WIDTH) // hd
    gmat = ((head_of_lane[:, None] == head_of_lane[None, :]).astype(F32) / hd).astype(BF16)

    tok = lambda i: (i, 0)
    out_widths = (A_HEADS * hd, 2 * hd, B_HEADS * 2 * hd, B_HEADS * 2 * hd,
                  IDX_HEADS * hd, 2 * hd, A_LAT, B_HEADS * 2 * hd)
    consts = (g, inv128, sgn, wnr, wr, wp, wiwt, gn, gnr, gmat)
    return pl.pallas_call(
        _even_proj_kernel,
        grid=(n // tm,),
        in_specs=[pl.BlockSpec((tm, d), tok), pl.BlockSpec((tm, 1), tok)]
                 + [_const_spec(c.shape) for c in consts],
        out_specs=[pl.BlockSpec((tm, w), tok) for w in out_widths]
                  + [pl.BlockSpec((IW_ROWS, tm), lambda i: (0, i))],
        out_shape=[jax.ShapeDtypeStruct((n, w), BF16) for w in out_widths]
                  + [jax.ShapeDtypeStruct((IW_ROWS, n), F32)],
        compiler_params=_params("parallel"),
        name="even_proj",
    )(h, pos.reshape(n, 1), *consts)


def _bit_planes(words):
    a = list(words)
    j, m = WORD_BITS // 2, 0x0000FFFF
    while j:
        k = 0
        while k < WORD_BITS:
            t = (a[k] ^ lax.shift_right_logical(a[k + j], j)) & m
            a[k] = a[k] ^ t
            a[k + j] = a[k + j] ^ lax.shift_left(t, j)
            k = (k + j + 1) & ~j
        j >>= 1
        m ^= m << j
    return a


def _stack_heads(x, n_heads):
    lane = lax.broadcasted_iota(I32, (x.shape[0], LANES), 1)
    parts = []
    for hh in range(n_heads):
        slab = x[:, (hh // 2) * LANES:(hh // 2 + 1) * LANES]
        keep = (lane < HEAD_DIM) if hh % 2 == 0 else (lane >= HEAD_DIM)
        parts.append(jnp.where(keep, slab, jnp.zeros_like(slab)))
    return jnp.concatenate(parts, axis=0)


def _dsa_kernel(*refs, top_k, bounded, diff_lam_init=None):
    fused = diff_lam_init is not None
    refs = list(refs)
    shift_ref = refs.pop(0)
    if fused:
        assert bounded
        shiftb_ref, lam_ref, bq_ref, bk_ref, bv_ref, bg_ref = refs[:6]
        refs = refs[6:]
    aq_ref, iq_ref, iwt_ref, k_ref, v_ref, ik_ref, tri_ref, wuv_ref, o_ref = refs[:9]
    refs = refs[9:]
    if fused:
        ob_ref = refs.pop(0)
    key_sc, bias_sc, plane_sc, alive_sc, acc_sc = refs[:5]
    dacc_sc = refs[5] if fused else None
    tq = aq_ref.shape[0]
    tk = k_ref.shape[1]
    q0 = pl.program_id(1) * tq
    n_kb = (q0 + tq + tk - 1) // tk

    iq_all = _stack_heads(iq_ref[...], IDX_HEADS)
    iwt = iwt_ref[...]
    qchunk = (q0 + lax.broadcasted_iota(I32, (1, tq), 1)) >> CHUNK_SHIFT
    krow_chunk = lax.broadcasted_iota(I32, (tk, tq), 0) >> CHUNK_SHIFT
    groups = tk // (WORD_BITS * SUBLANES)
    n_blocks = key_sc.shape[0]

    def score_block(kb, masked):
        dots = jnp.maximum(_dot_nt(ik_ref[kb], iq_all), 0.0)
        sc = dots[:, 0:tq] * iwt[0:1, :]
        for hh in range(1, IDX_HEADS):
            sc = sc + dots[:, hh * tq:(hh + 1) * tq] * iwt[hh:hh + 1, :]
        bits = pltpu.bitcast(sc, I32)
        key = jnp.where(bits < 0, INT_MIN - bits, bits)
        if masked:
            key = jnp.where(krow_chunk <= qchunk - kb * (tk >> CHUNK_SHIFT), key, INT_MIN)
        key_sc[kb] = key
        for g in range(groups):
            base = g * WORD_BITS * SUBLANES
            planes = _bit_planes([key[base + SUBLANES * j:base + SUBLANES * (j + 1)] for j in range(WORD_BITS)])
            planes[0] = ~planes[0]
            for p in range(WORD_BITS):
                plane_sc[kb, p, SUBLANES * g:SUBLANES * (g + 1), :] = planes[p]
        alive_sc[kb] = jnp.full((SUBLANES * groups, tq), -1, I32)

    if fused:
        diff_group = _diff_bounded_group(shiftb_ref[0], bq_ref, bk_ref, bv_ref, dacc_sc, n_kb - 1)

        def fused_group(kbs, carry):
            diff_group(kbs)
            for kb in kbs:
                score_block(kb, False)
            return carry

        _grouped_loop(n_kb - 1, fused_group, 0)
    else:
        def score_step(kb, carry):
            score_block(kb, False)
            return carry

        _unrolled_loop(n_kb - 1, score_step, 0)
    score_block(n_kb - 1, True)

    def clear_body(kb, carry):
        plane_sc[kb] = jnp.zeros(plane_sc.shape[1:], I32)
        alive_sc[kb] = jnp.zeros(alive_sc.shape[1:], I32)
        return carry

    lax.fori_loop(n_kb, n_blocks, clear_body, 0)

    def radix_body(b, carry):
        t_u, k_rem, keep = carry
        first = jnp.where(b == 0, jnp.int32(-1), jnp.int32(0))
        b_prev = jnp.maximum(b - 1, 0)

        counts = []
        for kb in range(n_blocks):
            a = alive_sc[kb] & ((plane_sc[kb, b_prev] ^ keep) | first)
            alive_sc[kb] = a
            counts.append(lax.population_count(a & plane_sc[kb, b]))
        cnt = jnp.sum(_tree_sum(counts).astype(F32), axis=0, keepdims=True)
        take = cnt >= k_rem
        t_u = t_u | jnp.where(take, lax.shift_left(jnp.int32(1), 31 - b), 0)
        return t_u, jnp.where(take, k_rem, k_rem - cnt), jnp.where(take, 0, -1)

    init = (jnp.zeros((1, tq), I32), jnp.full((1, tq), float(top_k), F32), jnp.zeros((1, tq), I32))
    if fused:
        diff_group([n_kb - 1])
        carry = init
        for b in range(WORD_BITS):
            carry = radix_body(jnp.int32(b), carry)
        t_u, k_rem, keep = carry
        _diff_finish(dacc_sc[...], lam_ref, bg_ref, ob_ref, diff_lam_init)
    else:
        t_u, k_rem, keep = lax.fori_loop(0, WORD_BITS, radix_body, init)

    n_eq = jnp.zeros((SUBLANES * groups, tq), I32)
    for kb in range(n_blocks):
        n_eq = n_eq + lax.population_count(alive_sc[kb] & (plane_sc[kb, WORD_BITS - 1] ^ keep))
    n_eq = jnp.sum(n_eq.astype(F32), axis=0, keepdims=True)
    t = t_u ^ INT_MIN
    few = t == INT_MIN
    need = jnp.where(few, 0.0, k_rem)
    n_tied = jnp.sum(jnp.where(few | (n_eq == need), 0.0, 1.0))

    sel_bias = -shift_ref[0]

    def plain_bias():
        t_min = jnp.where(few, INT_MIN + 1, t)

        def body(kb, carry):
            bias_sc[kb] = jnp.where(key_sc[kb] >= t_min, sel_bias, NEG_BIAS).astype(BF16)
            return carry

        lax.fori_loop(0, n_kb, body, 0)

    def ranked_bias():
        tri = tri_ref[...]

        def body(kb, carry):
            kk_all = key_sc[kb]
            for c in range(tk // TIE_BLOCK):
                kk = kk_all[c * TIE_BLOCK:(c + 1) * TIE_BLOCK]
                eq = kk == t
                rank = _dot(tri, jnp.where(eq, 1.0, 0.0).astype(BF16)) + carry
                tie_bias = jnp.where(rank <= need, sel_bias, NEG_BIAS)
                bias = jnp.where(kk > t, sel_bias, jnp.where(eq, tie_bias, NEG_BIAS))
                bias_sc[kb, c * TIE_BLOCK:(c + 1) * TIE_BLOCK, :] = bias.astype(BF16)
                carry = rank[TIE_BLOCK - 1:TIE_BLOCK, :]
            return carry

        lax.fori_loop(0, n_kb, body, jnp.zeros((1, tq), F32))

    lax.cond(n_tied == 0.0, plain_bias, ranked_bias)

    rows = A_HEADS * tq
    eye = lax.broadcasted_iota(I32, (tq, tq), 0) == lax.broadcasted_iota(I32, (tq, tq), 1)
    eye = jnp.where(eye, 1.0, 0.0).astype(BF16)
    qe = jnp.concatenate([_stack_heads(aq_ref[...], A_HEADS),
                          jnp.concatenate([eye] * A_HEADS, axis=0)], axis=1)
    ones = jnp.ones((tk, LANES), BF16)

    def logits(kb):
        return _dot_nt(qe, jnp.concatenate([k_ref[kb], bias_sc[kb]], axis=1))

    def values(kb):
        return jnp.concatenate([v_ref[kb], ones], axis=1)

    acc0 = jnp.zeros((rows, A_LAT + LANES), F32)
    if bounded:
        def att_group(kbs, carry):
            p = jnp.concatenate([jnp.exp2(logits(kb)).astype(BF16) for kb in kbs], axis=1)
            acc_sc[...] += _dot(p, jnp.concatenate([values(kb) for kb in kbs], axis=0))
            return carry

        acc_sc[...] = acc0
        _grouped_loop(n_kb, att_group, 0)
        acc = acc_sc[...]
    else:
        def softmax_step(kb, m, acc, s):
            m_new = jnp.maximum(m, jnp.max(s, axis=1, keepdims=True))
            p = jnp.exp2(s - m_new).astype(BF16)
            return m_new, jnp.exp2(m - m_new) * acc + _dot(p, values(kb))

        def att_body(kb, carry):
            m, acc, s = carry
            s_next = logits(kb + 1)
            m, acc = softmax_step(kb, m, acc, s)
            return m, acc, s_next

        m, acc, s = lax.fori_loop(0, n_kb - 1, att_body, (jnp.full((rows, 1), M_INIT, F32), acc0, logits(0)))
        _, acc = softmax_step(n_kb - 1, m, acc, s)
    o_lat = (acc[:, :A_LAT] / acc[:, A_LAT:]).astype(BF16)
    o_all = jnp.concatenate([o_lat[hh * tq:(hh + 1) * tq] for hh in range(A_HEADS)], axis=1)
    o_ref[...] = _dot(o_all, wuv_ref[...]).astype(o_ref.dtype)


def _dsa_call(shift, aq, iq, iwt, akd, av, ikd, w_uv, *, bounded, diff=None):
    b, s, _ = aq.shape
    tq = min(TQ_DSA, s)
    tk = min(TK_DSA, s)
    assert tq == LANES, "the one-hot mask columns pair one query tile with 128 contraction lanes"
    top_k = min(TOPK_MAX, s // 4)
    n_kb = s // tk
    nq = s // tq
    blk = lambda x: x.reshape(b, n_kb, tk, x.shape[-1])
    tri = (jnp.arange(TIE_BLOCK)[:, None] >= jnp.arange(TIE_BLOCK)[None, :]).astype(BF16)
    eye = jnp.eye(A_HEADS, dtype=w_uv.dtype)
    wuv_bd = (w_uv[:, :, None, :] * eye[:, None, :, None]).reshape(A_HEADS * A_LAT, A_HEADS * HEAD_DIM)
    wuv_bd = wuv_bd.astype(BF16)

    qmap = lambda bi, i: (bi, i, 0)
    kvmap = lambda bi, i: (bi, 0, 0, 0)
    kv_spec = pl.BlockSpec((None, n_kb, tk, LANES), kvmap)
    args = [shift]
    in_specs = [pl.BlockSpec(memory_space=pltpu.SMEM)]
    out_specs = [pl.BlockSpec((None, tq, A_HEADS * HEAD_DIM), qmap)]
    out_shape = [jax.ShapeDtypeStruct((b, s, A_HEADS * HEAD_DIM), BF16)]
    scratch = [pltpu.VMEM((n_kb, tk, tq), I32), pltpu.VMEM((n_kb, tk, tq), BF16),
               pltpu.VMEM((n_kb, WORD_BITS, tk // WORD_BITS, tq), I32),
               pltpu.VMEM((n_kb, tk // WORD_BITS, tq), I32),
               pltpu.VMEM((A_HEADS * tq, A_LAT + LANES), F32)]
    lam_init = None
    if diff is not None:
        shift_b, bq, bk, bv, lam_vecs, subln_g, lam_init = diff
        assert B_HEADS * tq == tk and bq.shape[-1] == B_HEADS * LANES
        dqmap = lambda bi, i: (bi, i // B_HEADS, i % B_HEADS)
        dkvmap = lambda bi, i: (bi, 0, 0, i % B_HEADS)
        args += [shift_b, lam_vecs, bq, bk.reshape(b, n_kb, tk, -1), bv.reshape(b, n_kb, tk, -1), subln_g]
        in_specs += [pl.BlockSpec(memory_space=pltpu.SMEM), _const_spec(lam_vecs.shape),
                     pl.BlockSpec((None, tk, LANES), dqmap),
                     pl.BlockSpec((None, n_kb, tk, LANES), dkvmap),
                     pl.BlockSpec((None, n_kb, tk, LANES), dkvmap),
                     _const_spec(subln_g.shape)]
        out_specs.append(pl.BlockSpec((None, tk, LANES), dqmap))
        out_shape.append(jax.ShapeDtypeStruct(bq.shape, BF16))
        scratch.append(pltpu.VMEM((2 * tk, 2 * LANES), F32))
    args += [aq, iq, iwt, blk(akd), blk(av), blk(ikd), tri, wuv_bd]
    in_specs += [pl.BlockSpec((None, tq, aq.shape[-1]), qmap),
                 pl.BlockSpec((None, tq, iq.shape[-1]), qmap),
                 pl.BlockSpec((iwt.shape[0], tq), lambda bi, i: (0, bi * nq + i)),
                 kv_spec, kv_spec, kv_spec,
                 _const_spec(tri.shape), _const_spec(wuv_bd.shape)]
    outs = pl.pallas_call(
        functools.partial(_dsa_kernel, top_k=top_k, bounded=bounded, diff_lam_init=lam_init),
        grid=(b, nq),
        in_specs=in_specs,
        out_specs=out_specs,
        out_shape=out_shape,
        scratch_shapes=scratch,
        compiler_params=_params("parallel", "arbitrary"),
        name="dsa" if diff is None else "dsa_diff",
    )(*args)
    return outs[0] if diff is None else tuple(outs)


def _diff_split_maps(q):
    lane = lax.broadcasted_iota(I32, q.shape, 1)
    zero = jnp.zeros_like(q)
    return jnp.concatenate([jnp.where(lane < HEAD_DIM, q, zero), jnp.where(lane >= HEAD_DIM, q, zero)], axis=0)


def _diff_bounded_group(shift, q_ref, k_ref, v_ref, acc_sc, i):
    t = q_ref.shape[0]
    assert t // CHUNK < LANES
    q_lane = lax.broadcasted_iota(I32, (2 * t, LANES), 1)
    q_chunk = lax.broadcasted_iota(I32, (2 * t, LANES), 0)
    q_chunk = jnp.where(q_chunk >= t, q_chunk - t, q_chunk) >> CHUNK_SHIFT
    q_extra = jnp.where(q_lane == 0, 1.0, jnp.where(q_lane - 1 > q_chunk, NEG_BIAS, 0.0))
    q2 = jnp.concatenate([_diff_split_maps(q_ref[...]), q_extra.astype(BF16)], axis=1)
    k_lane = lax.broadcasted_iota(I32, (t, LANES), 1)
    k_chunk = lax.broadcasted_iota(I32, (t, LANES), 0) >> CHUNK_SHIFT
    k_plain = jnp.where(k_lane == 0, -shift, 0.0)
    k_diag = jnp.where(k_lane - 1 == k_chunk, 1.0, k_plain).astype(BF16)
    k_plain = k_plain.astype(BF16)
    ones = jnp.ones((t, LANES), BF16)

    def probs(j):
        k_extra = jnp.where(j == i, k_diag, k_plain)
        return jnp.exp2(_dot_nt(q2, jnp.concatenate([k_ref[j], k_extra], axis=1))).astype(BF16)

    def group(js):
        p = jnp.concatenate([probs(j) for j in js], axis=1)
        acc_sc[...] += _dot(p, jnp.concatenate([jnp.concatenate([v_ref[j], ones], axis=1) for j in js], axis=0))

    acc_sc[...] = jnp.zeros(acc_sc.shape, F32)
    return group


def _diff_finish(acc, lam_ref, g_ref, o_ref, lam_init):
    t = acc.shape[0] // 2
    vdim = o_ref.shape[-1]
    lam_v = lam_ref[...]
    lam = (jnp.exp(jnp.sum(lam_v[0:1] * lam_v[1:2], axis=1, keepdims=True))
           - jnp.exp(jnp.sum(lam_v[2:3] * lam_v[3:4], axis=1, keepdims=True)) + lam_init)
    o = acc[:, :vdim] / acc[:, vdim:]
    o = o[:t] - lam * o[t:]
    o_ref[...] = (_rms(o, g_ref[...]) * (1.0 - lam_init)).astype(o_ref.dtype)


def _diff_kernel(shift_ref, lam_ref, q_ref, k_ref, v_ref, g_ref, o_ref, acc_sc, *, lam_init, bounded):
    t = q_ref.shape[0]
    i = pl.program_id(2)
    vdim = v_ref.shape[-1]

    if bounded:
        group = _diff_bounded_group(shift_ref[0], q_ref, k_ref, v_ref, acc_sc, i)

        def group_step(js, carry):
            group(js)
            return carry

        _grouped_loop(i + 1, group_step, 0)
        acc = acc_sc[...]
    else:
        q2 = _diff_split_maps(q_ref[...])
        ones = jnp.ones((t, LANES), BF16)

        def values(j):
            return jnp.concatenate([v_ref[j], ones], axis=1)

        acc0 = jnp.zeros((2 * t, vdim + LANES), F32)
        qc = lax.broadcasted_iota(I32, (2 * t, t), 0)
        qc = jnp.where(qc >= t, qc - t, qc) >> CHUNK_SHIFT
        kc = lax.broadcasted_iota(I32, (2 * t, t), 1) >> CHUNK_SHIFT

        def logits(j):
            return _dot_nt(q2, k_ref[j])

        def softmax_step(j, m, acc, s):
            m_new = jnp.maximum(m, jnp.max(s, axis=1, keepdims=True))
            p = jnp.exp2(s - m_new).astype(BF16)
            return m_new, jnp.exp2(m - m_new) * acc + _dot(p, values(j))

        def body(j, carry):
            m, acc, s = carry
            s_next = logits(j + 1)
            m, acc = softmax_step(j, m, acc, s)
            return m, acc, s_next

        m, acc, s = lax.fori_loop(0, i, body, (jnp.full((2 * t, 1), M_INIT, F32), acc0, logits(0)))
        _, acc = softmax_step(i, m, acc, jnp.where(kc <= qc, s, NEG_BIAS))
    _diff_finish(acc, lam_ref, g_ref, o_ref, lam_init)


def _diff_call(shift, bq, bk, bv, lam_vecs, subln_g, lam_init, *, bounded):
    b, s, w = bq.shape
    t = min(T_DIFF, s)
    nb = s // t
    k4 = bk.reshape(b, nb, t, w)
    v4 = bv.reshape(b, nb, t, w)
    qmap = lambda bi, hh, i: (bi, i, hh)
    kvmap = lambda bi, hh, i: (bi, 0, 0, hh)
    return pl.pallas_call(
        functools.partial(_diff_kernel, lam_init=lam_init, bounded=bounded),
        grid=(b, B_HEADS, nb),
        in_specs=[pl.BlockSpec(memory_space=pltpu.SMEM),
                  _const_spec(lam_vecs.shape),
                  pl.BlockSpec((None, t, LANES), qmap),
                  pl.BlockSpec((None, nb, t, LANES), kvmap),
                  pl.BlockSpec((None, nb, t, LANES), kvmap),
                  _const_spec(subln_g.shape)],
        out_specs=pl.BlockSpec((None, t, LANES), qmap),
        out_shape=jax.ShapeDtypeStruct((b, s, w), BF16),
        scratch_shapes=[pltpu.VMEM((2 * t, 2 * LANES), F32)],
        compiler_params=_params("parallel", "parallel", "arbitrary"),
        name="diff_attn",
    )(shift, lam_vecs, bq, k4, v4, subln_g)


def _odd_kernel(h_ref, g_ref, wc_ref, wd_ref, clg_ref, clb_ref, ws_ref, sb_ref,
                cw_ref, cb_ref, dlg_ref, dlb_ref, woc_ref, wod_ref, o_ref, buf_sc, *, tiles_per_seq):
    tm = h_ref.shape[0]
    cwid = clg_ref.shape[1]

    h = h_ref[...]
    xn = _rms(h, g_ref[...]).astype(BF16)

    zc = jax.nn.gelu(_dot(xn, wc_ref[...]), approximate=True)
    u = zc[:, :cwid]
    v = _layer_norm(zc[:, cwid:], clg_ref[...], clb_ref[...]).astype(BF16)
    pi = lax.broadcasted_iota(I32, (SGU_BLOCK, SGU_BLOCK), 0) >> CHUNK_SHIFT
    pj = lax.broadcasted_iota(I32, (SGU_BLOCK, SGU_BLOCK), 1) >> CHUNK_SHIFT
    causal = pi >= pj
    w_s = [jnp.where(causal, ws_ref[gi], jnp.zeros((SGU_BLOCK, SGU_BLOCK), BF16)) for gi in range(C_GROUPS)]
    lane = lax.broadcasted_iota(I32, (SGU_BLOCK, LANES), 1)
    sgb = sb_ref[...]
    rows_out = []
    for r in range(tm // SGU_BLOCK):
        rs = slice(r * SGU_BLOCK, (r + 1) * SGU_BLOCK)
        slabs = []
        for pr in range(cwid // LANES):
            vs = v[rs, pr * LANES:(pr + 1) * LANES]
            lo = _dot(w_s[2 * pr], vs)
            hi = _dot(w_s[2 * pr + 1], vs)
            slabs.append(jnp.where(lane < HEAD_DIM, lo, hi))
        rows_out.append(jnp.concatenate(slabs, axis=1) + sgb)
    c_out = (u * jnp.concatenate(rows_out, axis=0)).astype(BF16)

    zd = _dot(xn, wd_ref[...])
    dwid = zd.shape[1] // 2
    hd = zd[:, :dwid] * jax.nn.sigmoid(zd[:, dwid:])

    @pl.when(pl.program_id(0) % tiles_per_seq == 0)
    def _():
        buf_sc[0:CONV_HALO, :] = jnp.zeros((CONV_HALO, dwid), F32)

    buf_sc[CONV_HALO:CONV_HALO + tm, :] = hd
    buf_sc[CONV_HALO + tm:, :] = jnp.zeros((SUBLANES, dwid), F32)
    conv = jnp.zeros((tm, dwid), F32) + cb_ref[...]
    base = CONV_HALO - (CONV_W - 1)
    for r in range(SUBLANES):
        group = None
        for w in range(CONV_W):
            if (base + w) % SUBLANES == r:
                a = (base + w) // SUBLANES * SUBLANES
                term = buf_sc[a:a + tm + SUBLANES, :] * cw_ref[w:w + 1, :]
                group = term if group is None else group + term
        conv = conv + group[r:r + tm]
    buf_sc[0:CONV_HALO, :] = buf_sc[tm:tm + CONV_HALO, :]
    d_out = _layer_norm(conv, dlg_ref[...], dlb_ref[...])
    d_out = (d_out * jax.nn.sigmoid(d_out)).astype(BF16)

    o_ref[...] = h + _dot(c_out, woc_ref[...]) + _dot(d_out, wod_ref[...])


def _odd_call(h, seq_len, g, w_in, w_out, c_ln_g, c_ln_b, c_w_s, c_b_s, d_conv_w, d_conv_b, d_ln_g, d_ln_b):
    n, d = h.shape
    tm = min(TM_ODD, seq_len)
    cwid = c_ln_g.shape[0]
    dwid = d_ln_g.shape[0]
    row = lambda x: x.reshape(1, -1).astype(F32)
    wc = w_in[:, :2 * cwid].astype(BF16)
    wd = w_in[:, 2 * cwid:].astype(BF16)
    sgb = jnp.repeat(c_b_s.T, cwid // C_GROUPS, axis=1).astype(F32)
    consts = (row(g), wc, wd, row(c_ln_g), row(c_ln_b), c_w_s.astype(BF16), sgb,
              d_conv_w.astype(F32), row(d_conv_b), row(d_ln_g), row(d_ln_b),
              w_out[:cwid].astype(BF16), w_out[cwid:].astype(BF16))
    tok = lambda i: (i, 0)
    return pl.pallas_call(
        functools.partial(_odd_kernel, tiles_per_seq=seq_len // tm),
        grid=(n // tm,),
        in_specs=[pl.BlockSpec((tm, d), tok)] + [_const_spec(c.shape) for c in consts],
        out_specs=pl.BlockSpec((tm, d), tok),
        out_shape=jax.ShapeDtypeStruct((n, d), F32),
        scratch_shapes=[pltpu.VMEM((tm + CONV_HALO + SUBLANES, dwid), F32)],
        compiler_params=_params("arbitrary"),
        name="odd_mixer",
    )(h, *consts)


def _logit_bound(gq, gk):
    unit = HEAD_DIM * HEAD_DIM ** -0.5 * math.log2(math.e) * BF16_SLACK
    return (unit * jnp.max(jnp.abs(gq)) * jnp.max(jnp.abs(gk))).astype(F32).reshape(1)


def _bounded_or_online(bound, call, *args):
    return lax.cond(bound[0] <= MAX_SHIFT,
                    lambda: call(bound, *args, bounded=True),
                    lambda: call(jnp.zeros_like(bound), *args, bounded=False))
def kernel(x, p, pos, ffn1_g, ffn1_wg, ffn1_wu, ffn1_wd, mix_g, ffn2_g, ffn2_wg, ffn2_wu, ffn2_wd, ple_g, ple_wgate, ple_wproj, ev_w_in, ev_w_out, a_q_g, a_k_g, a_w_uv, b_q_g, b_k_g, b_lam_q1, b_lam_k1, b_lam_q2, b_lam_k2, b_subln_g, od_w_in, od_w_out, c_ln_g, c_ln_b, c_w_s, c_b_s, d_conv_w, d_conv_b, d_ln_g, d_ln_b):
    bsz, s_len, d = x.shape
    n = bsz * s_len
    depth = p.shape[0]
    row = lambda v: v.reshape(1, -1).astype(F32)
    h = x.reshape(n, d)
    for layer in range(depth):
        ffn1 = (row(ffn1_g[layer]), ffn1_wg[layer].astype(BF16), ffn1_wu[layer].astype(BF16),
                ffn1_wd[layer].astype(BF16))
        ffn2 = (row(ffn2_g[layer]), ffn2_wg[layer].astype(BF16), ffn2_wu[layer].astype(BF16),
                ffn2_wd[layer].astype(BF16))
        ple = (p.reshape(depth, n, -1), layer, row(ple_g[layer]), ple_wgate[layer].astype(BF16),
               ple_wproj[layer].astype(BF16))
        h = _ffn_call(h, *ffn1)
        if layer % 2 == 0:
            e = layer // 2
            lam_init = 0.8 - 0.6 * math.exp(-0.3 * layer)
            aq, akd, bq, bk, iq, ikd, av, bv, iwt = _even_proj_call(
                h, pos, row(mix_g[layer]), ev_w_in[e], a_q_g[e], a_k_g[e], b_q_g[e], b_k_g[e])
            r3 = lambda a: a.reshape(bsz, s_len, a.shape[-1])
            lam_vecs = jnp.stack([b_lam_q1[e], b_lam_k1[e], b_lam_q2[e], b_lam_k2[e]]).astype(F32)
            a_args = (r3(aq), r3(iq), iwt, r3(akd), r3(av), r3(ikd), a_w_uv[e])
            b_args = (r3(bq), r3(bk), r3(bv), lam_vecs, row(b_subln_g[e]), lam_init)
            bound_a = _logit_bound(a_q_g[e], a_k_g[e])
            bound_b = _logit_bound(b_q_g[e], b_k_g[e])
            a_out, b_out = lax.cond(
                jnp.maximum(bound_a[0], bound_b[0]) <= MAX_SHIFT,
                lambda: _dsa_call(bound_a, *a_args, bounded=True, diff=(bound_b,) + b_args),
                lambda: (_bounded_or_online(bound_a, _dsa_call, *a_args),
                         _bounded_or_online(bound_b, _diff_call, *b_args)))
            a_w = A_HEADS * HEAD_DIM
            mix = (a_out.reshape(n, -1), b_out.reshape(n, -1),
                   ev_w_out[e][:a_w].astype(BF16), ev_w_out[e][a_w:].astype(BF16))
            h = _ffn_call(h, *ffn2, mix=mix, ple=ple)
        else:
            o = layer // 2
            h = _odd_call(h, s_len, mix_g[layer], od_w_in[o], od_w_out[o], c_ln_g[o], c_ln_b[o],
                          c_w_s[o], c_b_s[o], d_conv_w[o], d_conv_b[o], d_ln_g[o], d_ln_b[o])
            h = _ffn_call(h, *ffn2, ple=ple)
    return h.reshape(bsz, s_len, d)
```

```python
import functools
import math

import jax
import jax.numpy as jnp
from jax import lax
from jax.experimental import pallas as pl
from jax.experimental.pallas import tpu as pltpu

F32 = jnp.float32
BF16 = jnp.bfloat16
I32 = jnp.int32

EPS = 1e-6
HEAD_DIM = 64
CHUNK = 64
CHUNK_SHIFT = CHUNK.bit_length() - 1
ROPE_THETA = 10000.0
A_HEADS = 8
A_LAT = 128
IDX_HEADS = 4
TOPK_MAX = 256
B_HEADS = 4
C_GROUPS = 8
SGU_BLOCK = 128
CONV_W = 31
CONV_HALO = 32

LANES = 128
SUBLANES = 8
MXU_WIDTH = 256
INT_MIN = -(2 ** 31)
NEG_BIAS = -1e30
M_INIT = -1e29
MAX_SHIFT = 60.0
BF16_SLACK = 1.02

FF_CHUNK = 256
TM_FFN = 512
TM_PROJ = 512
TM_ODD = 512
TQ_DSA = 128
TK_DSA = 512
T_DIFF = 512
TIE_BLOCK = 256
WORD_BITS = 32
ATT_UNROLL = 4

VMEM_LIMIT_BYTES = 56 * 1024 * 1024


def _params(*sem):
    return pltpu.CompilerParams(dimension_semantics=sem, vmem_limit_bytes=VMEM_LIMIT_BYTES)


def _const_spec(shape):
    zeros = (0,) * len(shape)
    return pl.BlockSpec(shape, lambda *_: zeros)


def _rms(x, g):
    ms = jnp.mean(x * x, axis=-1, keepdims=True)
    return x * lax.rsqrt(ms + EPS) * g


def _layer_norm(x, g, b):
    mu = jnp.mean(x, axis=-1, keepdims=True)
    xc = x - mu
    var = jnp.mean(xc * xc, axis=-1, keepdims=True)
    return xc * lax.rsqrt(var + EPS) * g + b


def _grouped_loop(n, group_step, carry):
    done = 0
    unroll = ATT_UNROLL
    while unroll >= 1:
        def group(g, c, unroll=unroll, done=done):
            first = done + g * unroll
            return group_step([first + u for u in range(unroll)], c)

        n_groups = lax.shift_right_logical(n - done, unroll.bit_length() - 1)
        carry = lax.fori_loop(0, n_groups, group, carry)
        done = done + n_groups * unroll
        unroll //= 2
    return carry


def _unrolled_loop(n, step, carry):
    def group_step(ks, c):
        for k in ks:
            c = step(k, c)
        return c

    return _grouped_loop(n, group_step, carry)


def _tree_sum(xs):
    xs = list(xs)
    while len(xs) > 1:
        xs = [xs[i] + xs[i + 1] for i in range(0, len(xs) - 1, 2)] + ([xs[-1]] if len(xs) % 2 else [])
    return xs[0]


def _dot(a, b):
    return jnp.dot(a, b, preferred_element_type=F32)


def _dot_nt(a, b):
    return lax.dot_general(a, b, (((1,), (1,)), ((), ())), preferred_element_type=F32)


def _ffn_kernel(*refs, has_mix, has_ple, n_chunks):
    refs = list(refs)
    h_ref = refs.pop(0)
    if has_mix:
        a_ref, b_ref, woa_ref, wob_ref = refs[:4]
        refs = refs[4:]
    g_ref, wg_ref, wu_ref, wd_ref = refs[:4]
    refs = refs[4:]
    if has_ple:
        p_ref, pg_ref, wgate_ref, wproj_ref = refs[:4]
        refs = refs[4:]
    (o_ref,) = refs

    h = h_ref[...]
    if has_mix:
        h = h + _dot(a_ref[...], woa_ref[...]) + _dot(b_ref[...], wob_ref[...])
    xn = _rms(h, g_ref[...]).astype(BF16)
    acc = jnp.zeros(h.shape, F32)
    for c in range(n_chunks):
        gt = _dot(xn, wg_ref[:, c * FF_CHUNK:(c + 1) * FF_CHUNK])
        up = _dot(xn, wu_ref[:, c * FF_CHUNK:(c + 1) * FF_CHUNK])
        act = (gt * jax.nn.sigmoid(gt) * up).astype(BF16)
        acc = acc + _dot(act, wd_ref[c * FF_CHUNK:(c + 1) * FF_CHUNK, :])
    h = h + 0.5 * acc
    if has_ple:
        xg = _rms(h, pg_ref[...]).astype(BF16)
        gate = jax.nn.sigmoid(_dot(xg, wgate_ref[...]))
        h = h + gate * _dot(p_ref[...].astype(BF16), wproj_ref[...])
    o_ref[...] = h


def _ffn_call(h, g, wg, wu, wd, mix=None, ple=None):
    n, d = h.shape
    tm = min(TM_FFN, n)
    d_ff = wd.shape[0]
    n_chunks = d_ff // FF_CHUNK
    tok = lambda i: (i, 0)
    args = [h]
    specs = [pl.BlockSpec((tm, d), tok)]
    if mix is not None:
        a, b, woa, wob = mix
        args += [a, b, woa, wob]
        specs += [pl.BlockSpec((tm, a.shape[1]), tok), pl.BlockSpec((tm, b.shape[1]), tok),
                  _const_spec(woa.shape), _const_spec(wob.shape)]
    args += [g, wg, wu, wd]
    specs += [_const_spec(g.shape), _const_spec(wg.shape), _const_spec(wu.shape), _const_spec(wd.shape)]
    if ple is not None:
        p, layer, pg, wgate, wproj = ple
        args += [p, pg, wgate, wproj]
        specs += [pl.BlockSpec((None, tm, p.shape[2]), lambda i: (layer, i, 0)), _const_spec(pg.shape),
                  _const_spec(wgate.shape), _const_spec(wproj.shape)]
    kern = functools.partial(_ffn_kernel, has_mix=mix is not None, has_ple=ple is not None,
                             n_chunks=n_chunks)
    return pl.pallas_call(
        kern,
        grid=(n // tm,),
        in_specs=specs,
        out_specs=pl.BlockSpec((tm, d), tok),
        out_shape=jax.ShapeDtypeStruct((n, d), F32),
        compiler_params=_params("parallel"),
        name="ffn",
    )(*args)


_NR_WIDTHS = (A_HEADS * HEAD_DIM, 2 * HEAD_DIM, B_HEADS * 2 * HEAD_DIM, B_HEADS * 2 * HEAD_DIM)
_R_WIDTHS = (IDX_HEADS * HEAD_DIM, 2 * HEAD_DIM)
_P_WIDTHS = (A_LAT, B_HEADS * 2 * HEAD_DIM)
IW_ROWS = SUBLANES


def _even_proj_kernel(h_ref, pos_ref, g_ref, inv_ref, sgn_ref, wnr_ref, wr_ref, wp_ref, wiwt_ref,
                      gn_ref, gnr_ref, gmat_ref,
                      aq_ref, akd_ref, bq_ref, bk_ref, iq_ref, ikd_ref, av_ref, bv_ref, iwt_ref):
    xn = _rms(h_ref[...], g_ref[...]).astype(BF16)
    ang = pos_ref[...].astype(F32) * inv_ref[...]
    cos = jnp.cos(ang)
    sin = jnp.sin(ang) * sgn_ref[...]
    gmat = gmat_ref[...]
    first_half = lax.broadcasted_iota(I32, cos.shape, 1) % HEAD_DIM < HEAD_DIM // 2

    def swap_halves(xs):
        return jnp.where(first_half, pltpu.roll(xs, LANES - HEAD_DIM // 2, 1), pltpu.roll(xs, HEAD_DIM // 2, 1))

    off = 0
    for width, out_ref in zip(_NR_WIDTHS, (aq_ref, akd_ref, bq_ref, bk_ref)):
        x = _dot(xn, wnr_ref[:, off:off + width])
        mw = min(width, MXU_WIDTH)
        ms = jnp.concatenate(
            [_dot(jnp.square(x[:, c:c + mw]).astype(BF16), gmat[:mw, :mw]) for c in range(0, width, mw)], axis=1)
        for s in range(width // LANES):
            sl = slice(s * LANES, (s + 1) * LANES)
            gsl = slice(off + s * LANES, off + (s + 1) * LANES)
            xs = x[:, sl]
            y = (xs * (gn_ref[:, gsl] * cos) + swap_halves(xs) * (gnr_ref[:, gsl] * sin)) * lax.rsqrt(ms[:, sl] + EPS)
            out_ref[:, sl] = y.astype(out_ref.dtype)
        off += width

    off = 0
    for width, out_ref in zip(_R_WIDTHS, (iq_ref, ikd_ref)):
        x = _dot(xn, wr_ref[:, off:off + width])
        for s in range(width // LANES):
            xs = x[:, s * LANES:(s + 1) * LANES]
            out_ref[:, s * LANES:(s + 1) * LANES] = (xs * cos + swap_halves(xs) * sin).astype(out_ref.dtype)
        off += width

    off = 0
    for width, out_ref in zip(_P_WIDTHS, (av_ref, bv_ref)):
        out_ref[...] = _dot(xn, wp_ref[:, off:off + width]).astype(out_ref.dtype)
        off += width
    iwt_ref[...] = _dot_nt(wiwt_ref[...], xn)


def _swap_halves(w):
    lead = w.shape[:-1]
    w4 = w.reshape(*lead, -1, 2, HEAD_DIM // 2)
    return w4[..., ::-1, :].reshape(*lead, -1)


def _even_proj_call(h, pos, g, w_in, a_q_g, a_k_g, b_q_g, b_k_g):
    n, d = h.shape
    tm = min(TM_PROJ, n)
    hd = HEAD_DIM
    sizes = (A_HEADS * hd, hd, A_LAT, IDX_HEADS * hd, hd, IDX_HEADS,
             B_HEADS * 2 * hd, B_HEADS * 2 * hd, B_HEADS * 2 * hd)
    offs = [0]
    for s in sizes:
        offs.append(offs[-1] + s)
    w_aq, w_ak, w_av, w_iq, w_ik, w_iw, w_bq, w_bk, w_bv = (
        w_in[:, offs[i]:offs[i + 1]] for i in range(9))
    dup = lambda w: jnp.concatenate([w, w], axis=1)
    wnr = jnp.concatenate([w_aq, dup(w_ak), w_bq, w_bk], axis=1).astype(BF16)
    wr = jnp.concatenate([w_iq, dup(w_ik)], axis=1).astype(BF16)
    wp = jnp.concatenate([w_av, w_bv], axis=1).astype(BF16)
    wiwt = jnp.pad(w_iw.T, ((0, IW_ROWS - IDX_HEADS), (0, 0))).astype(BF16)

    scale = hd ** -0.5 * math.log2(math.e)
    gn = jnp.concatenate([jnp.tile(a_q_g, A_HEADS) * scale, jnp.tile(a_k_g, 2),
                          jnp.tile(b_q_g, 2 * B_HEADS) * scale, jnp.tile(b_k_g, 2 * B_HEADS)])
    gnr = _swap_halves(gn)
    gn = gn.reshape(1, -1).astype(F32)
    gnr = gnr.reshape(1, -1).astype(F32)

    inv = ROPE_THETA ** (-jnp.arange(0, hd, 2, dtype=F32) / hd)
    inv128 = jnp.tile(inv, LANES // (hd // 2)).reshape(1, LANES)
    sgn = jnp.tile(jnp.concatenate([-jnp.ones(hd // 2, F32), jnp.ones(hd // 2, F32)]),
                   LANES // hd).reshape(1, LANES)
    head_of_lane = jnp.arange(MXU_WIDTH) // hd
    gmat = ((head_of_lane[:, None] == head_of_lane[None, :]).astype(F32) / hd).astype(BF16)

    tok = lambda i: (i, 0)
    out_widths = (A_HEADS * hd, 2 * hd, B_HEADS * 2 * hd, B_HEADS * 2 * hd,
                  IDX_HEADS * hd, 2 * hd, A_LAT, B_HEADS * 2 * hd)
    consts = (g, inv128, sgn, wnr, wr, wp, wiwt, gn, gnr, gmat)
    return pl.pallas_call(
        _even_proj_kernel,
        grid=(n // tm,),
        in_specs=[pl.BlockSpec((tm, d), tok), pl.BlockSpec((tm, 1), tok)]
                 + [_const_spec(c.shape) for c in consts],
        out_specs=[pl.BlockSpec((tm, w), tok) for w in out_widths]
                  + [pl.BlockSpec((IW_ROWS, tm), lambda i: (0, i))],
        out_shape=[jax.ShapeDtypeStruct((n, w), BF16) for w in out_widths]
                  + [jax.ShapeDtypeStruct((IW_ROWS, n), F32)],
        compiler_params=_params("parallel"),
        name="even_proj",
    )(h, pos.reshape(n, 1), *consts)


def _bit_planes(words):
    a = list(words)
    j, m = WORD_BITS // 2, 0x0000FFFF
    while j:
        k = 0
        while k < WORD_BITS:
            t = (a[k] ^ lax.shift_right_logical(a[k + j], j)) & m
            a[k] = a[k] ^ t
            a[k + j] = a[k + j] ^ lax.shift_left(t, j)
            k = (k + j + 1) & ~j
        j >>= 1
        m ^= m << j
    return a


def _stack_heads(x, n_heads):
    lane = lax.broadcasted_iota(I32, (x.shape[0], LANES), 1)
    parts = []
    for hh in range(n_heads):
        slab = x[:, (hh // 2) * LANES:(hh // 2 + 1) * LANES]
        keep = (lane < HEAD_DIM) if hh % 2 == 0 else (lane >= HEAD_DIM)
        parts.append(jnp.where(keep, slab, jnp.zeros_like(slab)))
    return jnp.concatenate(parts, axis=0)


def _dsa_kernel(*refs, top_k, bounded, diff_lam_init=None):
    fused = diff_lam_init is not None
    refs = list(refs)
    shift_ref = refs.pop(0)
    if fused:
        assert bounded
        shiftb_ref, lam_ref, bq_ref, bk_ref, bv_ref, bg_ref = refs[:6]
        refs = refs[6:]
    aq_ref, iq_ref, iwt_ref, k_ref, v_ref, ik_ref, tri_ref, wuv_ref, o_ref = refs[:9]
    refs = refs[9:]
    if fused:
        ob_ref = refs.pop(0)
    key_sc, bias_sc, plane_sc, alive_sc, acc_sc = refs[:5]
    dacc_sc = refs[5] if fused else None
    tq = aq_ref.shape[0]
    tk = k_ref.shape[1]
    q0 = pl.program_id(1) * tq
    n_kb = (q0 + tq + tk - 1) // tk

    iq_all = _stack_heads(iq_ref[...], IDX_HEADS)
    iwt = iwt_ref[...]
    qchunk = (q0 + lax.broadcasted_iota(I32, (1, tq), 1)) >> CHUNK_SHIFT
    krow_chunk = lax.broadcasted_iota(I32, (tk, tq), 0) >> CHUNK_SHIFT
    groups = tk // (WORD_BITS * SUBLANES)
    n_blocks = key_sc.shape[0]

    def score_block(kb, masked):
        dots = jnp.maximum(_dot_nt(ik_ref[kb], iq_all), 0.0)
        sc = dots[:, 0:tq] * iwt[0:1, :]
        for hh in range(1, IDX_HEADS):
            sc = sc + dots[:, hh * tq:(hh + 1) * tq] * iwt[hh:hh + 1, :]
        bits = pltpu.bitcast(sc, I32)
        key = jnp.where(bits < 0, INT_MIN - bits, bits)
        if masked:
            key = jnp.where(krow_chunk <= qchunk - kb * (tk >> CHUNK_SHIFT), key, INT_MIN)
        key_sc[kb] = key
        for g in range(groups):
            base = g * WORD_BITS * SUBLANES
            planes = _bit_planes([key[base + SUBLANES * j:base + SUBLANES * (j + 1)] for j in range(WORD_BITS)])
            planes[0] = ~planes[0]
            for p in range(WORD_BITS):
                plane_sc[kb, p, SUBLANES * g:SUBLANES * (g + 1), :] = planes[p]
        alive_sc[kb] = jnp.full((SUBLANES * groups, tq), -1, I32)

    if fused:
        diff_group = _diff_bounded_group(shiftb_ref[0], bq_ref, bk_ref, bv_ref, dacc_sc, n_kb - 1)

        def fused_group(kbs, carry):
            diff_group(kbs)
            for kb in kbs:
                score_block(kb, False)
            return carry

        _grouped_loop(n_kb - 1, fused_group, 0)
    else:
        def score_step(kb, carry):
            score_block(kb, False)
            return carry

        _unrolled_loop(n_kb - 1, score_step, 0)
    score_block(n_kb - 1, True)

    def clear_body(kb, carry):
        plane_sc[kb] = jnp.zeros(plane_sc.shape[1:], I32)
        alive_sc[kb] = jnp.zeros(alive_sc.shape[1:], I32)
        return carry

    lax.fori_loop(n_kb, n_blocks, clear_body, 0)

    def radix_body(b, carry):
        t_u, k_rem, keep = carry
        first = jnp.where(b == 0, jnp.int32(-1), jnp.int32(0))
        b_prev = jnp.maximum(b - 1, 0)

        counts = []
        for kb in range(n_blocks):
            a = alive_sc[kb] & ((plane_sc[kb, b_prev] ^ keep) | first)
            alive_sc[kb] = a
            counts.append(lax.population_count(a & plane_sc[kb, b]))
        cnt = jnp.sum(_tree_sum(counts).astype(F32), axis=0, keepdims=True)
        take = cnt >= k_rem
        t_u = t_u | jnp.where(take, lax.shift_left(jnp.int32(1), 31 - b), 0)
        return t_u, jnp.where(take, k_rem, k_rem - cnt), jnp.where(take, 0, -1)

    init = (jnp.zeros((1, tq), I32), jnp.full((1, tq), float(top_k), F32), jnp.zeros((1, tq), I32))
    if fused:
        diff_group([n_kb - 1])
        carry = init
        for b in range(WORD_BITS):
            carry = radix_body(jnp.int32(b), carry)
        t_u, k_rem, keep = carry
        _diff_finish(dacc_sc[...], lam_ref, bg_ref, ob_ref, diff_lam_init)
    else:
        t_u, k_rem, keep = lax.fori_loop(0, WORD_BITS, radix_body, init)

    n_eq = jnp.zeros((SUBLANES * groups, tq), I32)
    for kb in range(n_blocks):
        n_eq = n_eq + lax.population_count(alive_sc[kb] & (plane_sc[kb, WORD_BITS - 1] ^ keep))
    n_eq = jnp.sum(n_eq.astype(F32), axis=0, keepdims=True)
    t = t_u ^ INT_MIN
    few = t == INT_MIN
    need = jnp.where(few, 0.0, k_rem)
    n_tied = jnp.sum(jnp.where(few | (n_eq == need), 0.0, 1.0))

    sel_bias = -shift_ref[0]

    t_min = jnp.where(few, INT_MIN + 1, t)

    def plain_bias_block(kb):
        return jnp.where(key_sc[kb] >= t_min, sel_bias, NEG_BIAS).astype(BF16)

    def plain_bias():
        def body(kb, carry):
            bias_sc[kb] = plain_bias_block(kb)
            return carry

        lax.fori_loop(0, n_kb, body, 0)

    def ranked_bias():
        tri = tri_ref[...]

        def body(kb, carry):
            kk_all = key_sc[kb]
            for c in range(tk // TIE_BLOCK):
                kk = kk_all[c * TIE_BLOCK:(c + 1) * TIE_BLOCK]
                eq = kk == t
                rank = _dot(tri, jnp.where(eq, 1.0, 0.0).astype(BF16)) + carry
                tie_bias = jnp.where(rank <= need, sel_bias, NEG_BIAS)
                bias = jnp.where(kk > t, sel_bias, jnp.where(eq, tie_bias, NEG_BIAS))
                bias_sc[kb, c * TIE_BLOCK:(c + 1) * TIE_BLOCK, :] = bias.astype(BF16)
                carry = rank[TIE_BLOCK - 1:TIE_BLOCK, :]
            return carry

        lax.fori_loop(0, n_kb, body, jnp.zeros((1, tq), F32))

    rows = A_HEADS * tq
    eye = lax.broadcasted_iota(I32, (tq, tq), 0) == lax.broadcasted_iota(I32, (tq, tq), 1)
    eye = jnp.where(eye, 1.0, 0.0).astype(BF16)
    qe = jnp.concatenate([_stack_heads(aq_ref[...], A_HEADS),
                          jnp.concatenate([eye] * A_HEADS, axis=0)], axis=1)
    ones = jnp.ones((tk, LANES), BF16)

    def logits_with(bias_of):
        return lambda kb: _dot_nt(qe, jnp.concatenate([k_ref[kb], bias_of(kb)], axis=1))

    def values(kb):
        return jnp.concatenate([v_ref[kb], ones], axis=1)

    acc0 = jnp.zeros((rows, A_LAT + LANES), F32)
    if bounded:
        def attend(bias_of):
            logits = logits_with(bias_of)

            def att_group(kbs, carry):
                p = jnp.concatenate([jnp.exp2(logits(kb)).astype(BF16) for kb in kbs], axis=1)
                acc_sc[...] += _dot(p, jnp.concatenate([values(kb) for kb in kbs], axis=0))
                return carry

            acc_sc[...] = acc0
            _grouped_loop(n_kb, att_group, 0)

        def attend_ranked():
            ranked_bias()
            attend(lambda kb: bias_sc[kb])

        lax.cond(n_tied == 0.0, lambda: attend(plain_bias_block), attend_ranked)
        acc = acc_sc[...]
    else:
        lax.cond(n_tied == 0.0, plain_bias, ranked_bias)
        logits = logits_with(lambda kb: bias_sc[kb])
        def softmax_step(kb, m, acc, s):
            m_new = jnp.maximum(m, jnp.max(s, axis=1, keepdims=True))
            p = jnp.exp2(s - m_new).astype(BF16)
            return m_new, jnp.exp2(m - m_new) * acc + _dot(p, values(kb))

        def att_body(kb, carry):
            m, acc, s = carry
            s_next = logits(kb + 1)
            m, acc = softmax_step(kb, m, acc, s)
            return m, acc, s_next

        m, acc, s = lax.fori_loop(0, n_kb - 1, att_body, (jnp.full((rows, 1), M_INIT, F32), acc0, logits(0)))
        _, acc = softmax_step(n_kb - 1, m, acc, s)
    o_lat = (acc[:, :A_LAT] / acc[:, A_LAT:]).astype(BF16)
    o_all = jnp.concatenate([o_lat[hh * tq:(hh + 1) * tq] for hh in range(A_HEADS)], axis=1)
    o_ref[...] = _dot(o_all, wuv_ref[...]).astype(o_ref.dtype)


def _dsa_call(shift, aq, iq, iwt, akd, av, ikd, w_uv, *, bounded, diff=None):
    b, s, _ = aq.shape
    tq = min(TQ_DSA, s)
    tk = min(TK_DSA, s)
    assert tq == LANES, "the one-hot mask columns pair one query tile with 128 contraction lanes"
    top_k = min(TOPK_MAX, s // 4)
    n_kb = s // tk
    nq = s // tq
    blk = lambda x: x.reshape(b, n_kb, tk, x.shape[-1])
    tri = (jnp.arange(TIE_BLOCK)[:, None] >= jnp.arange(TIE_BLOCK)[None, :]).astype(BF16)
    eye = jnp.eye(A_HEADS, dtype=w_uv.dtype)
    wuv_bd = (w_uv[:, :, None, :] * eye[:, None, :, None]).reshape(A_HEADS * A_LAT, A_HEADS * HEAD_DIM)
    wuv_bd = wuv_bd.astype(BF16)

    qmap = lambda bi, i: (bi, i, 0)
    kvmap = lambda bi, i: (bi, 0, 0, 0)
    kv_spec = pl.BlockSpec((None, n_kb, tk, LANES), kvmap)
    args = [shift]
    in_specs = [pl.BlockSpec(memory_space=pltpu.SMEM)]
    out_specs = [pl.BlockSpec((None, tq, A_HEADS * HEAD_DIM), qmap)]
    out_shape = [jax.ShapeDtypeStruct((b, s, A_HEADS * HEAD_DIM), BF16)]
    scratch = [pltpu.VMEM((n_kb, tk, tq), I32), pltpu.VMEM((n_kb, tk, tq), BF16),
               pltpu.VMEM((n_kb, WORD_BITS, tk // WORD_BITS, tq), I32),
               pltpu.VMEM((n_kb, tk // WORD_BITS, tq), I32),
               pltpu.VMEM((A_HEADS * tq, A_LAT + LANES), F32)]
    lam_init = None
    if diff is not None:
        shift_b, bq, bk, bv, lam_vecs, subln_g, lam_init = diff
        assert B_HEADS * tq == tk and bq.shape[-1] == B_HEADS * LANES
        dqmap = lambda bi, i: (bi, i // B_HEADS, i % B_HEADS)
        dkvmap = lambda bi, i: (bi, 0, 0, i % B_HEADS)
        args += [shift_b, lam_vecs, bq, bk.reshape(b, n_kb, tk, -1), bv.reshape(b, n_kb, tk, -1), subln_g]
        in_specs += [pl.BlockSpec(memory_space=pltpu.SMEM), _const_spec(lam_vecs.shape),
                     pl.BlockSpec((None, tk, LANES), dqmap),
                     pl.BlockSpec((None, n_kb, tk, LANES), dkvmap),
                     pl.BlockSpec((None, n_kb, tk, LANES), dkvmap),
                     _const_spec(subln_g.shape)]
        out_specs.append(pl.BlockSpec((None, tk, LANES), dqmap))
        out_shape.append(jax.ShapeDtypeStruct(bq.shape, BF16))
        scratch.append(pltpu.VMEM((2 * tk, 2 * LANES), F32))
    args += [aq, iq, iwt, blk(akd), blk(av), blk(ikd), tri, wuv_bd]
    in_specs += [pl.BlockSpec((None, tq, aq.shape[-1]), qmap),
                 pl.BlockSpec((None, tq, iq.shape[-1]), qmap),
                 pl.BlockSpec((iwt.shape[0], tq), lambda bi, i: (0, bi * nq + i)),
                 kv_spec, kv_spec, kv_spec,
                 _const_spec(tri.shape), _const_spec(wuv_bd.shape)]
    outs = pl.pallas_call(
        functools.partial(_dsa_kernel, top_k=top_k, bounded=bounded, diff_lam_init=lam_init),
        grid=(b, nq),
        in_specs=in_specs,
        out_specs=out_specs,
        out_shape=out_shape,
        scratch_shapes=scratch,
        compiler_params=_params("parallel", "arbitrary"),
        name="dsa" if diff is None else "dsa_diff",
    )(*args)
    return outs[0] if diff is None else tuple(outs)


def _diff_split_maps(q):
    lane = lax.broadcasted_iota(I32, q.shape, 1)
    zero = jnp.zeros_like(q)
    return jnp.concatenate([jnp.where(lane < HEAD_DIM, q, zero), jnp.where(lane >= HEAD_DIM, q, zero)], axis=0)


def _diff_bounded_group(shift, q_ref, k_ref, v_ref, acc_sc, i):
    t = q_ref.shape[0]
    assert t // CHUNK < LANES
    q_lane = lax.broadcasted_iota(I32, (2 * t, LANES), 1)
    q_chunk = lax.broadcasted_iota(I32, (2 * t, LANES), 0)
    q_chunk = jnp.where(q_chunk >= t, q_chunk - t, q_chunk) >> CHUNK_SHIFT
    q_extra = jnp.where(q_lane == 0, 1.0, jnp.where(q_lane - 1 > q_chunk, NEG_BIAS, 0.0))
    q2 = jnp.concatenate([_diff_split_maps(q_ref[...]), q_extra.astype(BF16)], axis=1)
    k_lane = lax.broadcasted_iota(I32, (t, LANES), 1)
    k_chunk = lax.broadcasted_iota(I32, (t, LANES), 0) >> CHUNK_SHIFT
    k_plain = jnp.where(k_lane == 0, -shift, 0.0)
    k_diag = jnp.where(k_lane - 1 == k_chunk, 1.0, k_plain).astype(BF16)
    k_plain = k_plain.astype(BF16)
    ones = jnp.ones((t, LANES), BF16)

    def probs(j):
        k_extra = jnp.where(j == i, k_diag, k_plain)
        return jnp.exp2(_dot_nt(q2, jnp.concatenate([k_ref[j], k_extra], axis=1))).astype(BF16)

    def group(js):
        p = jnp.concatenate([probs(j) for j in js], axis=1)
        acc_sc[...] += _dot(p, jnp.concatenate([jnp.concatenate([v_ref[j], ones], axis=1) for j in js], axis=0))

    acc_sc[...] = jnp.zeros(acc_sc.shape, F32)
    return group


def _diff_finish(acc, lam_ref, g_ref, o_ref, lam_init):
    t = acc.shape[0] // 2
    vdim = o_ref.shape[-1]
    lam_v = lam_ref[...]
    lam = (jnp.exp(jnp.sum(lam_v[0:1] * lam_v[1:2], axis=1, keepdims=True))
           - jnp.exp(jnp.sum(lam_v[2:3] * lam_v[3:4], axis=1, keepdims=True)) + lam_init)
    o = acc[:, :vdim] / acc[:, vdim:]
    o = o[:t] - lam * o[t:]
    o_ref[...] = (_rms(o, g_ref[...]) * (1.0 - lam_init)).astype(o_ref.dtype)


def _diff_kernel(shift_ref, lam_ref, q_ref, k_ref, v_ref, g_ref, o_ref, acc_sc, *, lam_init, bounded):
    t = q_ref.shape[0]
    i = pl.program_id(2)
    vdim = v_ref.shape[-1]

    if bounded:
        group = _diff_bounded_group(shift_ref[0], q_ref, k_ref, v_ref, acc_sc, i)

        def group_step(js, carry):
            group(js)
            return carry

        _grouped_loop(i + 1, group_step, 0)
        acc = acc_sc[...]
    else:
        q2 = _diff_split_maps(q_ref[...])
        ones = jnp.ones((t, LANES), BF16)

        def values(j):
            return jnp.concatenate([v_ref[j], ones], axis=1)

        acc0 = jnp.zeros((2 * t, vdim + LANES), F32)
        qc = lax.broadcasted_iota(I32, (2 * t, t), 0)
        qc = jnp.where(qc >= t, qc - t, qc) >> CHUNK_SHIFT
        kc = lax.broadcasted_iota(I32, (2 * t, t), 1) >> CHUNK_SHIFT

        def logits(j):
            return _dot_nt(q2, k_ref[j])

        def softmax_step(j, m, acc, s):
            m_new = jnp.maximum(m, jnp.max(s, axis=1, keepdims=True))
            p = jnp.exp2(s - m_new).astype(BF16)
            return m_new, jnp.exp2(m - m_new) * acc + _dot(p, values(j))

        def body(j, carry):
            m, acc, s = carry
            s_next = logits(j + 1)
            m, acc = softmax_step(j, m, acc, s)
            return m, acc, s_next

        m, acc, s = lax.fori_loop(0, i, body, (jnp.full((2 * t, 1), M_INIT, F32), acc0, logits(0)))
        _, acc = softmax_step(i, m, acc, jnp.where(kc <= qc, s, NEG_BIAS))
    _diff_finish(acc, lam_ref, g_ref, o_ref, lam_init)


def _diff_call(shift, bq, bk, bv, lam_vecs, subln_g, lam_init, *, bounded):
    b, s, w = bq.shape
    t = min(T_DIFF, s)
    nb = s // t
    k4 = bk.reshape(b, nb, t, w)
    v4 = bv.reshape(b, nb, t, w)
    qmap = lambda bi, hh, i: (bi, i, hh)
    kvmap = lambda bi, hh, i: (bi, 0, 0, hh)
    return pl.pallas_call(
        functools.partial(_diff_kernel, lam_init=lam_init, bounded=bounded),
        grid=(b, B_HEADS, nb),
        in_specs=[pl.BlockSpec(memory_space=pltpu.SMEM),
                  _const_spec(lam_vecs.shape),
                  pl.BlockSpec((None, t, LANES), qmap),
                  pl.BlockSpec((None, nb, t, LANES), kvmap),
                  pl.BlockSpec((None, nb, t, LANES), kvmap),
                  _const_spec(subln_g.shape)],
        out_specs=pl.BlockSpec((None, t, LANES), qmap),
        out_shape=jax.ShapeDtypeStruct((b, s, w), BF16),
        scratch_shapes=[pltpu.VMEM((2 * t, 2 * LANES), F32)],
        compiler_params=_params("parallel", "parallel", "arbitrary"),
        name="diff_attn",
    )(shift, lam_vecs, bq, k4, v4, subln_g)


def _odd_kernel(h_ref, g_ref, wc_ref, wd_ref, clg_ref, clb_ref, ws_ref, sb_ref,
                cw_ref, cb_ref, dlg_ref, dlb_ref, woc_ref, wod_ref, o_ref, buf_sc, *, tiles_per_seq):
    tm = h_ref.shape[0]
    cwid = clg_ref.shape[1]

    h = h_ref[...]
    xn = _rms(h, g_ref[...]).astype(BF16)

    zc = jax.nn.gelu(_dot(xn, wc_ref[...]), approximate=True)
    u = zc[:, :cwid]
    v = _layer_norm(zc[:, cwid:], clg_ref[...], clb_ref[...]).astype(BF16)
    pi = lax.broadcasted_iota(I32, (SGU_BLOCK, SGU_BLOCK), 0) >> CHUNK_SHIFT
    pj = lax.broadcasted_iota(I32, (SGU_BLOCK, SGU_BLOCK), 1) >> CHUNK_SHIFT
    causal = pi >= pj
    w_s = [jnp.where(causal, ws_ref[gi], jnp.zeros((SGU_BLOCK, SGU_BLOCK), BF16)) for gi in range(C_GROUPS)]
    lane = lax.broadcasted_iota(I32, (SGU_BLOCK, LANES), 1)
    sgb = sb_ref[...]
    rows_out = []
    for r in range(tm // SGU_BLOCK):
        rs = slice(r * SGU_BLOCK, (r + 1) * SGU_BLOCK)
        slabs = []
        for pr in range(cwid // LANES):
            vs = v[rs, pr * LANES:(pr + 1) * LANES]
            lo = _dot(w_s[2 * pr], vs)
            hi = _dot(w_s[2 * pr + 1], vs)
            slabs.append(jnp.where(lane < HEAD_DIM, lo, hi))
        rows_out.append(jnp.concatenate(slabs, axis=1) + sgb)
    c_out = (u * jnp.concatenate(rows_out, axis=0)).astype(BF16)

    zd = _dot(xn, wd_ref[...])
    dwid = zd.shape[1] // 2
    hd = zd[:, :dwid] * jax.nn.sigmoid(zd[:, dwid:])

    @pl.when(pl.program_id(0) % tiles_per_seq == 0)
    def _():
        buf_sc[0:CONV_HALO, :] = jnp.zeros((CONV_HALO, dwid), F32)

    buf_sc[CONV_HALO:CONV_HALO + tm, :] = hd
    buf_sc[CONV_HALO + tm:, :] = jnp.zeros((SUBLANES, dwid), F32)
    conv = jnp.zeros((tm, dwid), F32) + cb_ref[...]
    base = CONV_HALO - (CONV_W - 1)
    for r in range(SUBLANES):
        group = None
        for w in range(CONV_W):
            if (base + w) % SUBLANES == r:
                a = (base + w) // SUBLANES * SUBLANES
                term = buf_sc[a:a + tm + SUBLANES, :] * cw_ref[w:w + 1, :]
                group = term if group is None else group + term
        conv = conv + group[r:r + tm]
    buf_sc[0:CONV_HALO, :] = buf_sc[tm:tm + CONV_HALO, :]
    d_out = _layer_norm(conv, dlg_ref[...], dlb_ref[...])
    d_out = (d_out * jax.nn.sigmoid(d_out)).astype(BF16)

    o_ref[...] = h + _dot(c_out, woc_ref[...]) + _dot(d_out, wod_ref[...])


def _odd_call(h, seq_len, g, w_in, w_out, c_ln_g, c_ln_b, c_w_s, c_b_s, d_conv_w, d_conv_b, d_ln_g, d_ln_b):
    n, d = h.shape
    tm = min(TM_ODD, seq_len)
    cwid = c_ln_g.shape[0]
    dwid = d_ln_g.shape[0]
    row = lambda x: x.reshape(1, -1).astype(F32)
    wc = w_in[:, :2 * cwid].astype(BF16)
    wd = w_in[:, 2 * cwid:].astype(BF16)
    sgb = jnp.repeat(c_b_s.T, cwid // C_GROUPS, axis=1).astype(F32)
    consts = (row(g), wc, wd, row(c_ln_g), row(c_ln_b), c_w_s.astype(BF16), sgb,
              d_conv_w.astype(F32), row(d_conv_b), row(d_ln_g), row(d_ln_b),
              w_out[:cwid].astype(BF16), w_out[cwid:].astype(BF16))
    tok = lambda i: (i, 0)
    return pl.pallas_call(
        functools.partial(_odd_kernel, tiles_per_seq=seq_len // tm),
        grid=(n // tm,),
        in_specs=[pl.BlockSpec((tm, d), tok)] + [_const_spec(c.shape) for c in consts],
        out_specs=pl.BlockSpec((tm, d), tok),
        out_shape=jax.ShapeDtypeStruct((n, d), F32),
        scratch_shapes=[pltpu.VMEM((tm + CONV_HALO + SUBLANES, dwid), F32)],
        compiler_params=_params("arbitrary"),
        name="odd_mixer",
    )(h, *consts)


def _logit_bound(gq, gk):
    unit = HEAD_DIM * HEAD_DIM ** -0.5 * math.log2(math.e) * BF16_SLACK
    return (unit * jnp.max(jnp.abs(gq)) * jnp.max(jnp.abs(gk))).astype(F32).reshape(1)


def _bounded_or_online(bound, call, *args):
    return lax.cond(bound[0] <= MAX_SHIFT,
                    lambda: call(bound, *args, bounded=True),
                    lambda: call(jnp.zeros_like(bound), *args, bounded=False))
def kernel(x, p, pos, ffn1_g, ffn1_wg, ffn1_wu, ffn1_wd, mix_g, ffn2_g, ffn2_wg, ffn2_wu, ffn2_wd, ple_g, ple_wgate, ple_wproj, ev_w_in, ev_w_out, a_q_g, a_k_g, a_w_uv, b_q_g, b_k_g, b_lam_q1, b_lam_k1, b_lam_q2, b_lam_k2, b_subln_g, od_w_in, od_w_out, c_ln_g, c_ln_b, c_w_s, c_b_s, d_conv_w, d_conv_b, d_ln_g, d_ln_b):
    bsz, s_len, d = x.shape
    n = bsz * s_len
    depth = p.shape[0]
    row = lambda v: v.reshape(1, -1).astype(F32)
    h = x.reshape(n, d)
    for layer in range(depth):
        ffn1 = (row(ffn1_g[layer]), ffn1_wg[layer].astype(BF16), ffn1_wu[layer].astype(BF16),
                ffn1_wd[layer].astype(BF16))
        ffn2 = (row(ffn2_g[layer]), ffn2_wg[layer].astype(BF16), ffn2_wu[layer].astype(BF16),
                ffn2_wd[layer].astype(BF16))
        ple = (p.reshape(depth, n, -1), layer, row(ple_g[layer]), ple_wgate[layer].astype(BF16),
               ple_wproj[layer].astype(BF16))
        h = _ffn_call(h, *ffn1)
        if layer % 2 == 0:
            e = layer // 2
            lam_init = 0.8 - 0.6 * math.exp(-0.3 * layer)
            aq, akd, bq, bk, iq, ikd, av, bv, iwt = _even_proj_call(
                h, pos, row(mix_g[layer]), ev_w_in[e], a_q_g[e], a_k_g[e], b_q_g[e], b_k_g[e])
            r3 = lambda a: a.reshape(bsz, s_len, a.shape[-1])
            lam_vecs = jnp.stack([b_lam_q1[e], b_lam_k1[e], b_lam_q2[e], b_lam_k2[e]]).astype(F32)
            a_args = (r3(aq), r3(iq), iwt, r3(akd), r3(av), r3(ikd), a_w_uv[e])
            b_args = (r3(bq), r3(bk), r3(bv), lam_vecs, row(b_subln_g[e]), lam_init)
            bound_a = _logit_bound(a_q_g[e], a_k_g[e])
            bound_b = _logit_bound(b_q_g[e], b_k_g[e])
            a_out, b_out = lax.cond(
                jnp.maximum(bound_a[0], bound_b[0]) <= MAX_SHIFT,
                lambda: _dsa_call(bound_a, *a_args, bounded=True, diff=(bound_b,) + b_args),
                lambda: (_bounded_or_online(bound_a, _dsa_call, *a_args),
                         _bounded_or_online(bound_b, _diff_call, *b_args)))
            a_w = A_HEADS * HEAD_DIM
            mix = (a_out.reshape(n, -1), b_out.reshape(n, -1),
                   ev_w_out[e][:a_w].astype(BF16), ev_w_out[e][a_w:].astype(BF16))
            h = _ffn_call(h, *ffn2, mix=mix, ple=ple)
        else:
            o = layer // 2
            h = _odd_call(h, s_len, mix_g[layer], od_w_in[o], od_w_out[o], c_ln_g[o], c_ln_b[o],
                          c_w_s[o], c_b_s[o], d_conv_w[o], d_conv_b[o], d_ln_g[o], d_ln_b[o])
            h = _ffn_call(h, *ffn2, ple=ple)
    return h.reshape(bsz, s_len, d)
```

```python
import functools
import math

import jax
import jax.numpy as jnp
from jax import lax
from jax.experimental import pallas as pl
from jax.experimental.pallas import tpu as pltpu

F32 = jnp.float32
BF16 = jnp.bfloat16
I32 = jnp.int32

EPS = 1e-6
HEAD_DIM = 64
CHUNK = 64
CHUNK_SHIFT = CHUNK.bit_length() - 1
ROPE_THETA = 10000.0
A_HEADS = 8
A_LAT = 128
IDX_HEADS = 4
TOPK_MAX = 256
B_HEADS = 4
C_GROUPS = 8
SGU_BLOCK = 128
CONV_W = 31
CONV_HALO = 32

LANES = 128
SUBLANES = 8
MXU_WIDTH = 256
INT_MIN = -(2 ** 31)
NEG_BIAS = -1e30
M_INIT = -1e29
MAX_SHIFT = 60.0
BF16_SLACK = 1.02

FF_CHUNK = 256
TM_FFN = 512
TM_PROJ = 512
TM_ODD = 512
TQ_DSA = 128
TK_DSA = 512
T_DIFF = 512
TIE_BLOCK = 256
WORD_BITS = 32
ATT_UNROLL = 8

VMEM_LIMIT_BYTES = 56 * 1024 * 1024


def _params(*sem):
    return pltpu.CompilerParams(dimension_semantics=sem, vmem_limit_bytes=VMEM_LIMIT_BYTES)


def _const_spec(shape):
    zeros = (0,) * len(shape)
    return pl.BlockSpec(shape, lambda *_: zeros)


def _rms(x, g):
    ms = jnp.mean(x * x, axis=-1, keepdims=True)
    return x * lax.rsqrt(ms + EPS) * g


def _layer_norm(x, g, b):
    mu = jnp.mean(x, axis=-1, keepdims=True)
    xc = x - mu
    var = jnp.mean(xc * xc, axis=-1, keepdims=True)
    return xc * lax.rsqrt(var + EPS) * g + b


def _grouped_loop(n, group_step, carry):
    done = 0
    unroll = ATT_UNROLL
    while unroll >= 1:
        def group(g, c, unroll=unroll, done=done):
            first = done + g * unroll
            return group_step([first + u for u in range(unroll)], c)

        n_groups = lax.shift_right_logical(n - done, unroll.bit_length() - 1)
        carry = lax.fori_loop(0, n_groups, group, carry)
        done = done + n_groups * unroll
        unroll //= 2
    return carry


def _unrolled_loop(n, step, carry):
    def group_step(ks, c):
        for k in ks:
            c = step(k, c)
        return c

    return _grouped_loop(n, group_step, carry)


def _tree_sum(xs):
    xs = list(xs)
    while len(xs) > 1:
        xs = [xs[i] + xs[i + 1] for i in range(0, len(xs) - 1, 2)] + ([xs[-1]] if len(xs) % 2 else [])
    return xs[0]


def _dot(a, b):
    return jnp.dot(a, b, preferred_element_type=F32)


def _dot_nt(a, b):
    return lax.dot_general(a, b, (((1,), (1,)), ((), ())), preferred_element_type=F32)


def _ffn_kernel(*refs, has_mix, has_ple, n_chunks):
    refs = list(refs)
    h_ref = refs.pop(0)
    if has_mix:
        a_ref, b_ref, woa_ref, wob_ref = refs[:4]
        refs = refs[4:]
    g_ref, wg_ref, wu_ref, wd_ref = refs[:4]
    refs = refs[4:]
    if has_ple:
        p_ref, pg_ref, wgate_ref, wproj_ref = refs[:4]
        refs = refs[4:]
    (o_ref,) = refs

    h = h_ref[...]
    if has_mix:
        h = h + _dot(a_ref[...], woa_ref[...]) + _dot(b_ref[...], wob_ref[...])
    xn = _rms(h, g_ref[...]).astype(BF16)
    acc = jnp.zeros(h.shape, F32)
    for c in range(n_chunks):
        gt = _dot(xn, wg_ref[:, c * FF_CHUNK:(c + 1) * FF_CHUNK])
        up = _dot(xn, wu_ref[:, c * FF_CHUNK:(c + 1) * FF_CHUNK])
        act = (gt * jax.nn.sigmoid(gt) * up).astype(BF16)
        acc = acc + _dot(act, wd_ref[c * FF_CHUNK:(c + 1) * FF_CHUNK, :])
    h = h + 0.5 * acc
    if has_ple:
        xg = _rms(h, pg_ref[...]).astype(BF16)
        gate = jax.nn.sigmoid(_dot(xg, wgate_ref[...]))
        h = h + gate * _dot(p_ref[...].astype(BF16), wproj_ref[...])
    o_ref[...] = h


def _ffn_call(h, g, wg, wu, wd, mix=None, ple=None):
    n, d = h.shape
    tm = min(TM_FFN, n)
    d_ff = wd.shape[0]
    n_chunks = d_ff // FF_CHUNK
    tok = lambda i: (i, 0)
    args = [h]
    specs = [pl.BlockSpec((tm, d), tok)]
    if mix is not None:
        a, b, woa, wob = mix
        args += [a, b, woa, wob]
        specs += [pl.BlockSpec((tm, a.shape[1]), tok), pl.BlockSpec((tm, b.shape[1]), tok),
                  _const_spec(woa.shape), _const_spec(wob.shape)]
    args += [g, wg, wu, wd]
    specs += [_const_spec(g.shape), _const_spec(wg.shape), _const_spec(wu.shape), _const_spec(wd.shape)]
    if ple is not None:
        p, layer, pg, wgate, wproj = ple
        args += [p, pg, wgate, wproj]
        specs += [pl.BlockSpec((None, tm, p.shape[2]), lambda i: (layer, i, 0)), _const_spec(pg.shape),
                  _const_spec(wgate.shape), _const_spec(wproj.shape)]
    kern = functools.partial(_ffn_kernel, has_mix=mix is not None, has_ple=ple is not None,
                             n_chunks=n_chunks)
    return pl.pallas_call(
        kern,
        grid=(n // tm,),
        in_specs=specs,
        out_specs=pl.BlockSpec((tm, d), tok),
        out_shape=jax.ShapeDtypeStruct((n, d), F32),
        compiler_params=_params("parallel"),
        name="ffn",
    )(*args)


_NR_WIDTHS = (A_HEADS * HEAD_DIM, 2 * HEAD_DIM, B_HEADS * 2 * HEAD_DIM, B_HEADS * 2 * HEAD_DIM)
_R_WIDTHS = (IDX_HEADS * HEAD_DIM, 2 * HEAD_DIM)
_P_WIDTHS = (A_LAT, B_HEADS * 2 * HEAD_DIM)
IW_ROWS = SUBLANES


def _even_proj_kernel(h_ref, pos_ref, g_ref, inv_ref, sgn_ref, wnr_ref, wr_ref, wp_ref, wiwt_ref,
                      gn_ref, gnr_ref, gmat_ref,
                      aq_ref, akd_ref, bq_ref, bk_ref, iq_ref, ikd_ref, av_ref, bv_ref, iwt_ref):
    xn = _rms(h_ref[...], g_ref[...]).astype(BF16)
    ang = pos_ref[...].astype(F32) * inv_ref[...]
    cos = jnp.cos(ang)
    sin = jnp.sin(ang) * sgn_ref[...]
    gmat = gmat_ref[...]
    first_half = lax.broadcasted_iota(I32, cos.shape, 1) % HEAD_DIM < HEAD_DIM // 2

    def swap_halves(xs):
        return jnp.where(first_half, pltpu.roll(xs, LANES - HEAD_DIM // 2, 1), pltpu.roll(xs, HEAD_DIM // 2, 1))

    off = 0
    for width, out_ref in zip(_NR_WIDTHS, (aq_ref, akd_ref, bq_ref, bk_ref)):
        x = _dot(xn, wnr_ref[:, off:off + width])
        mw = min(width, MXU_WIDTH)
        ms = jnp.concatenate(
            [_dot(jnp.square(x[:, c:c + mw]).astype(BF16), gmat[:mw, :mw]) for c in range(0, width, mw)], axis=1)
        for s in range(width // LANES):
            sl = slice(s * LANES, (s + 1) * LANES)
            gsl = slice(off + s * LANES, off + (s + 1) * LANES)
            xs = x[:, sl]
            y = (xs * (gn_ref[:, gsl] * cos) + swap_halves(xs) * (gnr_ref[:, gsl] * sin)) * lax.rsqrt(ms[:, sl] + EPS)
            out_ref[:, sl] = y.astype(out_ref.dtype)
        off += width

    off = 0
    for width, out_ref in zip(_R_WIDTHS, (iq_ref, ikd_ref)):
        x = _dot(xn, wr_ref[:, off:off + width])
        for s in range(width // LANES):
            xs = x[:, s * LANES:(s + 1) * LANES]
            out_ref[:, s * LANES:(s + 1) * LANES] = (xs * cos + swap_halves(xs) * sin).astype(out_ref.dtype)
        off += width

    off = 0
    for width, out_ref in zip(_P_WIDTHS, (av_ref, bv_ref)):
        out_ref[...] = _dot(xn, wp_ref[:, off:off + width]).astype(out_ref.dtype)
        off += width
    iwt_ref[...] = _dot_nt(wiwt_ref[...], xn)


def _swap_halves(w):
    lead = w.shape[:-1]
    w4 = w.reshape(*lead, -1, 2, HEAD_DIM // 2)
    return w4[..., ::-1, :].reshape(*lead, -1)


def _even_proj_call(h, pos, g, w_in, a_q_g, a_k_g, b_q_g, b_k_g):
    n, d = h.shape
    tm = min(TM_PROJ, n)
    hd = HEAD_DIM
    sizes = (A_HEADS * hd, hd, A_LAT, IDX_HEADS * hd, hd, IDX_HEADS,
             B_HEADS * 2 * hd, B_HEADS * 2 * hd, B_HEADS * 2 * hd)
    offs = [0]
    for s in sizes:
        offs.append(offs[-1] + s)
    w_aq, w_ak, w_av, w_iq, w_ik, w_iw, w_bq, w_bk, w_bv = (
        w_in[:, offs[i]:offs[i + 1]] for i in range(9))
    dup = lambda w: jnp.concatenate([w, w], axis=1)
    wnr = jnp.concatenate([w_aq, dup(w_ak), w_bq, w_bk], axis=1).astype(BF16)
    wr = jnp.concatenate([w_iq, dup(w_ik)], axis=1).astype(BF16)
    wp = jnp.concatenate([w_av, w_bv], axis=1).astype(BF16)
    wiwt = jnp.pad(w_iw.T, ((0, IW_ROWS - IDX_HEADS), (0, 0))).astype(BF16)

    scale = hd ** -0.5 * math.log2(math.e)
    gn = jnp.concatenate([jnp.tile(a_q_g, A_HEADS) * scale, jnp.tile(a_k_g, 2),
                          jnp.tile(b_q_g, 2 * B_HEADS) * scale, jnp.tile(b_k_g, 2 * B_HEADS)])
    gnr = _swap_halves(gn)
    gn = gn.reshape(1, -1).astype(F32)
    gnr = gnr.reshape(1, -1).astype(F32)

    inv = ROPE_THETA ** (-jnp.arange(0, hd, 2, dtype=F32) / hd)
    inv128 = jnp.tile(inv, LANES // (hd // 2)).reshape(1, LANES)
    sgn = jnp.tile(jnp.concatenate([-jnp.ones(hd // 2, F32), jnp.ones(hd // 2, F32)]),
                   LANES // hd).reshape(1, LANES)
    head_of_lane = jnp.arange(MXU_WIDTH) // hd
    gmat = ((head_of_lane[:, None] == head_of_lane[None, :]).astype(F32) / hd).astype(BF16)

    tok = lambda i: (i, 0)
    out_widths = (A_HEADS * hd, 2 * hd, B_HEADS * 2 * hd, B_HEADS * 2 * hd,
                  IDX_HEADS * hd, 2 * hd, A_LAT, B_HEADS * 2 * hd)
    consts = (g, inv128, sgn, wnr, wr, wp, wiwt, gn, gnr, gmat)
    return pl.pallas_call(
        _even_proj_kernel,
        grid=(n // tm,),
        in_specs=[pl.BlockSpec((tm, d), tok), pl.BlockSpec((tm, 1), tok)]
                 + [_const_spec(c.shape) for c in consts],
        out_specs=[pl.BlockSpec((tm, w), tok) for w in out_widths]
                  + [pl.BlockSpec((IW_ROWS, tm), lambda i: (0, i))],
        out_shape=[jax.ShapeDtypeStruct((n, w), BF16) for w in out_widths]
                  + [jax.ShapeDtypeStruct((IW_ROWS, n), F32)],
        compiler_params=_params("parallel"),
        name="even_proj",
    )(h, pos.reshape(n, 1), *consts)


def _bit_planes(words):
    a = list(words)
    j, m = WORD_BITS // 2, 0x0000FFFF
    while j:
        k = 0
        while k < WORD_BITS:
            t = (a[k] ^ lax.shift_right_logical(a[k + j], j)) & m
            a[k] = a[k] ^ t
            a[k + j] = a[k + j] ^ lax.shift_left(t, j)
            k = (k + j + 1) & ~j
        j >>= 1
        m ^= m << j
    return a


def _stack_heads(x, n_heads):
    lane = lax.broadcasted_iota(I32, (x.shape[0], LANES), 1)
    parts = []
    for hh in range(n_heads):
        slab = x[:, (hh // 2) * LANES:(hh // 2 + 1) * LANES]
        keep = (lane < HEAD_DIM) if hh % 2 == 0 else (lane >= HEAD_DIM)
        parts.append(jnp.where(keep, slab, jnp.zeros_like(slab)))
    return jnp.concatenate(parts, axis=0)


def _dsa_kernel(*refs, top_k, bounded, diff_lam_init=None):
    fused = diff_lam_init is not None
    refs = list(refs)
    shift_ref = refs.pop(0)
    if fused:
        assert bounded
        shiftb_ref, lam_ref, bq_ref, bk_ref, bv_ref, bg_ref = refs[:6]
        refs = refs[6:]
    aq_ref, iq_ref, iwt_ref, k_ref, v_ref, ik_ref, tri_ref, wuv_ref, o_ref = refs[:9]
    refs = refs[9:]
    if fused:
        ob_ref = refs.pop(0)
    key_sc, bias_sc, plane_sc, alive_sc, acc_sc = refs[:5]
    dacc_sc = refs[5] if fused else None
    tq = aq_ref.shape[0]
    tk = k_ref.shape[1]
    q0 = pl.program_id(1) * tq
    n_kb = (q0 + tq + tk - 1) // tk

    iq_all = _stack_heads(iq_ref[...], IDX_HEADS)
    iwt = iwt_ref[...]
    qchunk = (q0 + lax.broadcasted_iota(I32, (1, tq), 1)) >> CHUNK_SHIFT
    krow_chunk = lax.broadcasted_iota(I32, (tk, tq), 0) >> CHUNK_SHIFT
    groups = tk // (WORD_BITS * SUBLANES)
    n_blocks = key_sc.shape[0]

    def score_block(kb, masked):
        dots = jnp.maximum(_dot_nt(ik_ref[kb], iq_all), 0.0)
        sc = dots[:, 0:tq] * iwt[0:1, :]
        for hh in range(1, IDX_HEADS):
            sc = sc + dots[:, hh * tq:(hh + 1) * tq] * iwt[hh:hh + 1, :]
        bits = pltpu.bitcast(sc, I32)
        key = jnp.where(bits < 0, INT_MIN - bits, bits)
        if masked:
            key = jnp.where(krow_chunk <= qchunk - kb * (tk >> CHUNK_SHIFT), key, INT_MIN)
        key_sc[kb] = key
        for g in range(groups):
            base = g * WORD_BITS * SUBLANES
            planes = _bit_planes([key[base + SUBLANES * j:base + SUBLANES * (j + 1)] for j in range(WORD_BITS)])
            planes[0] = ~planes[0]
            for p in range(WORD_BITS):
                plane_sc[kb, p, SUBLANES * g:SUBLANES * (g + 1), :] = planes[p]
        alive_sc[kb] = jnp.full((SUBLANES * groups, tq), -1, I32)

    if fused:
        diff_group = _diff_bounded_group(shiftb_ref[0], bq_ref, bk_ref, bv_ref, dacc_sc, n_kb - 1)

        def fused_group(kbs, carry):
            diff_group(kbs)
            for kb in kbs:
                score_block(kb, False)
            return carry

        _grouped_loop(n_kb - 1, fused_group, 0)
    else:
        def score_step(kb, carry):
            score_block(kb, False)
            return carry

        _unrolled_loop(n_kb - 1, score_step, 0)
    score_block(n_kb - 1, True)

    def clear_body(kb, carry):
        plane_sc[kb] = jnp.zeros(plane_sc.shape[1:], I32)
        alive_sc[kb] = jnp.zeros(alive_sc.shape[1:], I32)
        return carry

    lax.fori_loop(n_kb, n_blocks, clear_body, 0)

    def radix_body(b, carry):
        t_u, k_rem, keep = carry
        first = jnp.where(b == 0, jnp.int32(-1), jnp.int32(0))
        b_prev = jnp.maximum(b - 1, 0)

        counts = []
        for kb in range(n_blocks):
            a = alive_sc[kb] & ((plane_sc[kb, b_prev] ^ keep) | first)
            alive_sc[kb] = a
            counts.append(lax.population_count(a & plane_sc[kb, b]))
        cnt = jnp.sum(_tree_sum(counts).astype(F32), axis=0, keepdims=True)
        take = cnt >= k_rem
        t_u = t_u | jnp.where(take, lax.shift_left(jnp.int32(1), 31 - b), 0)
        return t_u, jnp.where(take, k_rem, k_rem - cnt), jnp.where(take, 0, -1)

    init = (jnp.zeros((1, tq), I32), jnp.full((1, tq), float(top_k), F32), jnp.zeros((1, tq), I32))
    if fused:
        diff_group([n_kb - 1])
        carry = init
        for b in range(WORD_BITS):
            carry = radix_body(jnp.int32(b), carry)
        t_u, k_rem, keep = carry
        _diff_finish(dacc_sc[...], lam_ref, bg_ref, ob_ref, diff_lam_init)
    else:
        t_u, k_rem, keep = lax.fori_loop(0, WORD_BITS, radix_body, init)

    n_eq = jnp.zeros((SUBLANES * groups, tq), I32)
    for kb in range(n_blocks):
        n_eq = n_eq + lax.population_count(alive_sc[kb] & (plane_sc[kb, WORD_BITS - 1] ^ keep))
    n_eq = jnp.sum(n_eq.astype(F32), axis=0, keepdims=True)
    t = t_u ^ INT_MIN
    few = t == INT_MIN
    need = jnp.where(few, 0.0, k_rem)
    n_tied = jnp.sum(jnp.where(few | (n_eq == need), 0.0, 1.0))

    sel_bias = -shift_ref[0]

    def plain_bias():
        t_min = jnp.where(few, INT_MIN + 1, t)

        def body(kb, carry):
            bias_sc[kb] = jnp.where(key_sc[kb] >= t_min, sel_bias, NEG_BIAS).astype(BF16)
            return carry

        lax.fori_loop(0, n_kb, body, 0)

    def ranked_bias():
        tri = tri_ref[...]

        def body(kb, carry):
            kk_all = key_sc[kb]
            for c in range(tk // TIE_BLOCK):
                kk = kk_all[c * TIE_BLOCK:(c + 1) * TIE_BLOCK]
                eq = kk == t
                rank = _dot(tri, jnp.where(eq, 1.0, 0.0).astype(BF16)) + carry
                tie_bias = jnp.where(rank <= need, sel_bias, NEG_BIAS)
                bias = jnp.where(kk > t, sel_bias, jnp.where(eq, tie_bias, NEG_BIAS))
                bias_sc[kb, c * TIE_BLOCK:(c + 1) * TIE_BLOCK, :] = bias.astype(BF16)
                carry = rank[TIE_BLOCK - 1:TIE_BLOCK, :]
            return carry

        lax.fori_loop(0, n_kb, body, jnp.zeros((1, tq), F32))

    lax.cond(n_tied == 0.0, plain_bias, ranked_bias)

    rows = A_HEADS * tq
    eye = lax.broadcasted_iota(I32, (tq, tq), 0) == lax.broadcasted_iota(I32, (tq, tq), 1)
    eye = jnp.where(eye, 1.0, 0.0).astype(BF16)
    qe = jnp.concatenate([_stack_heads(aq_ref[...], A_HEADS),
                          jnp.concatenate([eye] * A_HEADS, axis=0)], axis=1)
    ones = jnp.ones((tk, LANES), BF16)

    def logits(kb):
        return _dot_nt(qe, jnp.concatenate([k_ref[kb], bias_sc[kb]], axis=1))

    def values(kb):
        return jnp.concatenate([v_ref[kb], ones], axis=1)

    acc0 = jnp.zeros((rows, A_LAT + LANES), F32)
    if bounded:
        def att_group(kbs, carry):
            p = jnp.concatenate([jnp.exp2(logits(kb)).astype(BF16) for kb in kbs], axis=1)
            acc_sc[...] += _dot(p, jnp.concatenate([values(kb) for kb in kbs], axis=0))
            return carry

        acc_sc[...] = acc0
        _grouped_loop(n_kb, att_group, 0)
        acc = acc_sc[...]
    else:
        def softmax_step(kb, m, acc, s):
            m_new = jnp.maximum(m, jnp.max(s, axis=1, keepdims=True))
            p = jnp.exp2(s - m_new).astype(BF16)
            return m_new, jnp.exp2(m - m_new) * acc + _dot(p, values(kb))

        def att_body(kb, carry):
            m, acc, s = carry
            s_next = logits(kb + 1)
            m, acc = softmax_step(kb, m, acc, s)
            return m, acc, s_next

        m, acc, s = lax.fori_loop(0, n_kb - 1, att_body, (jnp.full((rows, 1), M_INIT, F32), acc0, logits(0)))
        _, acc = softmax_step(n_kb - 1, m, acc, s)
    o_lat = (acc[:, :A_LAT] / acc[:, A_LAT:]).astype(BF16)
    o_all = jnp.concatenate([o_lat[hh * tq:(hh + 1) * tq] for hh in range(A_HEADS)], axis=1)
    o_ref[...] = _dot(o_all, wuv_ref[...]).astype(o_ref.dtype)


def _dsa_call(shift, aq, iq, iwt, akd, av, ikd, w_uv, *, bounded, diff=None):
    b, s, _ = aq.shape
    tq = min(TQ_DSA, s)
    tk = min(TK_DSA, s)
    assert tq == LANES, "the one-hot mask columns pair one query tile with 128 contraction lanes"
    top_k = min(TOPK_MAX, s // 4)
    n_kb = s // tk
    nq = s // tq
    blk = lambda x: x.reshape(b, n_kb, tk, x.shape[-1])
    tri = (jnp.arange(TIE_BLOCK)[:, None] >= jnp.arange(TIE_BLOCK)[None, :]).astype(BF16)
    eye = jnp.eye(A_HEADS, dtype=w_uv.dtype)
    wuv_bd = (w_uv[:, :, None, :] * eye[:, None, :, None]).reshape(A_HEADS * A_LAT, A_HEADS * HEAD_DIM)
    wuv_bd = wuv_bd.astype(BF16)

    qmap = lambda bi, i: (bi, i, 0)
    kvmap = lambda bi, i: (bi, 0, 0, 0)
    kv_spec = pl.BlockSpec((None, n_kb, tk, LANES), kvmap)
    args = [shift]
    in_specs = [pl.BlockSpec(memory_space=pltpu.SMEM)]
    out_specs = [pl.BlockSpec((None, tq, A_HEADS * HEAD_DIM), qmap)]
    out_shape = [jax.ShapeDtypeStruct((b, s, A_HEADS * HEAD_DIM), BF16)]
    scratch = [pltpu.VMEM((n_kb, tk, tq), I32), pltpu.VMEM((n_kb, tk, tq), BF16),
               pltpu.VMEM((n_kb, WORD_BITS, tk // WORD_BITS, tq), I32),
               pltpu.VMEM((n_kb, tk // WORD_BITS, tq), I32),
               pltpu.VMEM((A_HEADS * tq, A_LAT + LANES), F32)]
    lam_init = None
    if diff is not None:
        shift_b, bq, bk, bv, lam_vecs, subln_g, lam_init = diff
        assert B_HEADS * tq == tk and bq.shape[-1] == B_HEADS * LANES
        dqmap = lambda bi, i: (bi, i // B_HEADS, i % B_HEADS)
        dkvmap = lambda bi, i: (bi, 0, 0, i % B_HEADS)
        args += [shift_b, lam_vecs, bq, bk.reshape(b, n_kb, tk, -1), bv.reshape(b, n_kb, tk, -1), subln_g]
        in_specs += [pl.BlockSpec(memory_space=pltpu.SMEM), _const_spec(lam_vecs.shape),
                     pl.BlockSpec((None, tk, LANES), dqmap),
                     pl.BlockSpec((None, n_kb, tk, LANES), dkvmap),
                     pl.BlockSpec((None, n_kb, tk, LANES), dkvmap),
                     _const_spec(subln_g.shape)]
        out_specs.append(pl.BlockSpec((None, tk, LANES), dqmap))
        out_shape.append(jax.ShapeDtypeStruct(bq.shape, BF16))
        scratch.append(pltpu.VMEM((2 * tk, 2 * LANES), F32))
    args += [aq, iq, iwt, blk(akd), blk(av), blk(ikd), tri, wuv_bd]
    in_specs += [pl.BlockSpec((None, tq, aq.shape[-1]), qmap),
                 pl.BlockSpec((None, tq, iq.shape[-1]), qmap),
                 pl.BlockSpec((iwt.shape[0], tq), lambda bi, i: (0, bi * nq + i)),
                 kv_spec, kv_spec, kv_spec,
                 _const_spec(tri.shape), _const_spec(wuv_bd.shape)]
    outs = pl.pallas_call(
        functools.partial(_dsa_kernel, top_k=top_k, bounded=bounded, diff_lam_init=lam_init),
        grid=(b, nq),
        in_specs=in_specs,
        out_specs=out_specs,
        out_shape=out_shape,
        scratch_shapes=scratch,
        compiler_params=_params("parallel", "arbitrary"),
        name="dsa" if diff is None else "dsa_diff",
    )(*args)
    return outs[0] if diff is None else tuple(outs)


def _diff_split_maps(q):
    lane = lax.broadcasted_iota(I32, q.shape, 1)
    zero = jnp.zeros_like(q)
    return jnp.concatenate([jnp.where(lane < HEAD_DIM, q, zero), jnp.where(lane >= HEAD_DIM, q, zero)], axis=0)


def _diff_bounded_group(shift, q_ref, k_ref, v_ref, acc_sc, i):
    t = q_ref.shape[0]
    assert t // CHUNK < LANES
    q_lane = lax.broadcasted_iota(I32, (2 * t, LANES), 1)
    q_chunk = lax.broadcasted_iota(I32, (2 * t, LANES), 0)
    q_chunk = jnp.where(q_chunk >= t, q_chunk - t, q_chunk) >> CHUNK_SHIFT
    q_extra = jnp.where(q_lane == 0, 1.0, jnp.where(q_lane - 1 > q_chunk, NEG_BIAS, 0.0))
    q2 = jnp.concatenate([_diff_split_maps(q_ref[...]), q_extra.astype(BF16)], axis=1)
    k_lane = lax.broadcasted_iota(I32, (t, LANES), 1)
    k_chunk = lax.broadcasted_iota(I32, (t, LANES), 0) >> CHUNK_SHIFT
    k_plain = jnp.where(k_lane == 0, -shift, 0.0)
    k_diag = jnp.where(k_lane - 1 == k_chunk, 1.0, k_plain).astype(BF16)
    k_plain = k_plain.astype(BF16)
    ones = jnp.ones((t, LANES), BF16)

    def probs(j):
        k_extra = jnp.where(j == i, k_diag, k_plain)
        return jnp.exp2(_dot_nt(q2, jnp.concatenate([k_ref[j], k_extra], axis=1))).astype(BF16)

    def group(js):
        p = jnp.concatenate([probs(j) for j in js], axis=1)
        acc_sc[...] += _dot(p, jnp.concatenate([jnp.concatenate([v_ref[j], ones], axis=1) for j in js], axis=0))

    acc_sc[...] = jnp.zeros(acc_sc.shape, F32)
    return group


def _diff_finish(acc, lam_ref, g_ref, o_ref, lam_init):
    t = acc.shape[0] // 2
    vdim = o_ref.shape[-1]
    lam_v = lam_ref[...]
    lam = (jnp.exp(jnp.sum(lam_v[0:1] * lam_v[1:2], axis=1, keepdims=True))
           - jnp.exp(jnp.sum(lam_v[2:3] * lam_v[3:4], axis=1, keepdims=True)) + lam_init)
    o = acc[:, :vdim] / acc[:, vdim:]
    o = o[:t] - lam * o[t:]
    o_ref[...] = (_rms(o, g_ref[...]) * (1.0 - lam_init)).astype(o_ref.dtype)


def _diff_kernel(shift_ref, lam_ref, q_ref, k_ref, v_ref, g_ref, o_ref, acc_sc, *, lam_init, bounded):
    t = q_ref.shape[0]
    i = pl.program_id(2)
    vdim = v_ref.shape[-1]

    if bounded:
        group = _diff_bounded_group(shift_ref[0], q_ref, k_ref, v_ref, acc_sc, i)

        def group_step(js, carry):
            group(js)
            return carry

        _grouped_loop(i + 1, group_step, 0)
        acc = acc_sc[...]
    else:
        q2 = _diff_split_maps(q_ref[...])
        ones = jnp.ones((t, LANES), BF16)

        def values(j):
            return jnp.concatenate([v_ref[j], ones], axis=1)

        acc0 = jnp.zeros((2 * t, vdim + LANES), F32)
        qc = lax.broadcasted_iota(I32, (2 * t, t), 0)
        qc = jnp.where(qc >= t, qc - t, qc) >> CHUNK_SHIFT
        kc = lax.broadcasted_iota(I32, (2 * t, t), 1) >> CHUNK_SHIFT

        def logits(j):
            return _dot_nt(q2, k_ref[j])

        def softmax_step(j, m, acc, s):
            m_new = jnp.maximum(m, jnp.max(s, axis=1, keepdims=True))
            p = jnp.exp2(s - m_new).astype(BF16)
            return m_new, jnp.exp2(m - m_new) * acc + _dot(p, values(j))

        def body(j, carry):
            m, acc, s = carry
            s_next = logits(j + 1)
            m, acc = softmax_step(j, m, acc, s)
            return m, acc, s_next

        m, acc, s = lax.fori_loop(0, i, body, (jnp.full((2 * t, 1), M_INIT, F32), acc0, logits(0)))
        _, acc = softmax_step(i, m, acc, jnp.where(kc <= qc, s, NEG_BIAS))
    _diff_finish(acc, lam_ref, g_ref, o_ref, lam_init)


def _diff_call(shift, bq, bk, bv, lam_vecs, subln_g, lam_init, *, bounded):
    b, s, w = bq.shape
    t = min(T_DIFF, s)
    nb = s // t
    k4 = bk.reshape(b, nb, t, w)
    v4 = bv.reshape(b, nb, t, w)
    qmap = lambda bi, hh, i: (bi, i, hh)
    kvmap = lambda bi, hh, i: (bi, 0, 0, hh)
    return pl.pallas_call(
        functools.partial(_diff_kernel, lam_init=lam_init, bounded=bounded),
        grid=(b, B_HEADS, nb),
        in_specs=[pl.BlockSpec(memory_space=pltpu.SMEM),
                  _const_spec(lam_vecs.shape),
                  pl.BlockSpec((None, t, LANES), qmap),
                  pl.BlockSpec((None, nb, t, LANES), kvmap),
                  pl.BlockSpec((None, nb, t, LANES), kvmap),
                  _const_spec(subln_g.shape)],
        out_specs=pl.BlockSpec((None, t, LANES), qmap),
        out_shape=jax.ShapeDtypeStruct((b, s, w), BF16),
        scratch_shapes=[pltpu.VMEM((2 * t, 2 * LANES), F32)],
        compiler_params=_params("parallel", "parallel", "arbitrary"),
        name="diff_attn",
    )(shift, lam_vecs, bq, k4, v4, subln_g)


def _odd_kernel(h_ref, g_ref, wc_ref, wd_ref, clg_ref, clb_ref, ws_ref, sb_ref,
                cw_ref, cb_ref, dlg_ref, dlb_ref, woc_ref, wod_ref, o_ref, buf_sc, *, tiles_per_seq):
    tm = h_ref.shape[0]
    cwid = clg_ref.shape[1]

    h = h_ref[...]
    xn = _rms(h, g_ref[...]).astype(BF16)

    zc = jax.nn.gelu(_dot(xn, wc_ref[...]), approximate=True)
    u = zc[:, :cwid]
    v = _layer_norm(zc[:, cwid:], clg_ref[...], clb_ref[...]).astype(BF16)
    pi = lax.broadcasted_iota(I32, (SGU_BLOCK, SGU_BLOCK), 0) >> CHUNK_SHIFT
    pj = lax.broadcasted_iota(I32, (SGU_BLOCK, SGU_BLOCK), 1) >> CHUNK_SHIFT
    causal = pi >= pj
    w_s = [jnp.where(causal, ws_ref[gi], jnp.zeros((SGU_BLOCK, SGU_BLOCK), BF16)) for gi in range(C_GROUPS)]
    lane = lax.broadcasted_iota(I32, (SGU_BLOCK, LANES), 1)
    sgb = sb_ref[...]
    rows_out = []
    for r in range(tm // SGU_BLOCK):
        rs = slice(r * SGU_BLOCK, (r + 1) * SGU_BLOCK)
        slabs = []
        for pr in range(cwid // LANES):
            vs = v[rs, pr * LANES:(pr + 1) * LANES]
            lo = _dot(w_s[2 * pr], vs)
            hi = _dot(w_s[2 * pr + 1], vs)
            slabs.append(jnp.where(lane < HEAD_DIM, lo, hi))
        rows_out.append(jnp.concatenate(slabs, axis=1) + sgb)
    c_out = (u * jnp.concatenate(rows_out, axis=0)).astype(BF16)

    zd = _dot(xn, wd_ref[...])
    dwid = zd.shape[1] // 2
    hd = zd[:, :dwid] * jax.nn.sigmoid(zd[:, dwid:])

    @pl.when(pl.program_id(0) % tiles_per_seq == 0)
    def _():
        buf_sc[0:CONV_HALO, :] = jnp.zeros((CONV_HALO, dwid), F32)

    buf_sc[CONV_HALO:CONV_HALO + tm, :] = hd
    buf_sc[CONV_HALO + tm:, :] = jnp.zeros((SUBLANES, dwid), F32)
    conv = jnp.zeros((tm, dwid), F32) + cb_ref[...]
    base = CONV_HALO - (CONV_W - 1)
    for r in range(SUBLANES):
        group = None
        for w in range(CONV_W):
            if (base + w) % SUBLANES == r:
                a = (base + w) // SUBLANES * SUBLANES
                term = buf_sc[a:a + tm + SUBLANES, :] * cw_ref[w:w + 1, :]
                group = term if group is None else group + term
        conv = conv + group[r:r + tm]
    buf_sc[0:CONV_HALO, :] = buf_sc[tm:tm + CONV_HALO, :]
    d_out = _layer_norm(conv, dlg_ref[...], dlb_ref[...])
    d_out = (d_out * jax.nn.sigmoid(d_out)).astype(BF16)

    o_ref[...] = h + _dot(c_out, woc_ref[...]) + _dot(d_out, wod_ref[...])


def _odd_call(h, seq_len, g, w_in, w_out, c_ln_g, c_ln_b, c_w_s, c_b_s, d_conv_w, d_conv_b, d_ln_g, d_ln_b):
    n, d = h.shape
    tm = min(TM_ODD, seq_len)
    cwid = c_ln_g.shape[0]
    dwid = d_ln_g.shape[0]
    row = lambda x: x.reshape(1, -1).astype(F32)
    wc = w_in[:, :2 * cwid].astype(BF16)
    wd = w_in[:, 2 * cwid:].astype(BF16)
    sgb = jnp.repeat(c_b_s.T, cwid // C_GROUPS, axis=1).astype(F32)
    consts = (row(g), wc, wd, row(c_ln_g), row(c_ln_b), c_w_s.astype(BF16), sgb,
              d_conv_w.astype(F32), row(d_conv_b), row(d_ln_g), row(d_ln_b),
              w_out[:cwid].astype(BF16), w_out[cwid:].astype(BF16))
    tok = lambda i: (i, 0)
    return pl.pallas_call(
        functools.partial(_odd_kernel, tiles_per_seq=seq_len // tm),
        grid=(n // tm,),
        in_specs=[pl.BlockSpec((tm, d), tok)] + [_const_spec(c.shape) for c in consts],
        out_specs=pl.BlockSpec((tm, d), tok),
        out_shape=jax.ShapeDtypeStruct((n, d), F32),
        scratch_shapes=[pltpu.VMEM((tm + CONV_HALO + SUBLANES, dwid), F32)],
        compiler_params=_params("arbitrary"),
        name="odd_mixer",
    )(h, *consts)


def _logit_bound(gq, gk):
    unit = HEAD_DIM * HEAD_DIM ** -0.5 * math.log2(math.e) * BF16_SLACK
    return (unit * jnp.max(jnp.abs(gq)) * jnp.max(jnp.abs(gk))).astype(F32).reshape(1)


def _bounded_or_online(bound, call, *args):
    return lax.cond(bound[0] <= MAX_SHIFT,
                    lambda: call(bound, *args, bounded=True),
                    lambda: call(jnp.zeros_like(bound), *args, bounded=False))
def kernel(x, p, pos, ffn1_g, ffn1_wg, ffn1_wu, ffn1_wd, mix_g, ffn2_g, ffn2_wg, ffn2_wu, ffn2_wd, ple_g, ple_wgate, ple_wproj, ev_w_in, ev_w_out, a_q_g, a_k_g, a_w_uv, b_q_g, b_k_g, b_lam_q1, b_lam_k1, b_lam_q2, b_lam_k2, b_subln_g, od_w_in, od_w_out, c_ln_g, c_ln_b, c_w_s, c_b_s, d_conv_w, d_conv_b, d_ln_g, d_ln_b):
    bsz, s_len, d = x.shape
    n = bsz * s_len
    depth = p.shape[0]
    row = lambda v: v.reshape(1, -1).astype(F32)
    h = x.reshape(n, d)
    for layer in range(depth):
        ffn1 = (row(ffn1_g[layer]), ffn1_wg[layer].astype(BF16), ffn1_wu[layer].astype(BF16),
                ffn1_wd[layer].astype(BF16))
        ffn2 = (row(ffn2_g[layer]), ffn2_wg[layer].astype(BF16), ffn2_wu[layer].astype(BF16),
                ffn2_wd[layer].astype(BF16))
        ple = (p.reshape(depth, n, -1), layer, row(ple_g[layer]), ple_wgate[layer].astype(BF16),
               ple_wproj[layer].astype(BF16))
        h = _ffn_call(h, *ffn1)
        if layer % 2 == 0:
            e = layer // 2
            lam_init = 0.8 - 0.6 * math.exp(-0.3 * layer)
            aq, akd, bq, bk, iq, ikd, av, bv, iwt = _even_proj_call(
                h, pos, row(mix_g[layer]), ev_w_in[e], a_q_g[e], a_k_g[e], b_q_g[e], b_k_g[e])
            r3 = lambda a: a.reshape(bsz, s_len, a.shape[-1])
            lam_vecs = jnp.stack([b_lam_q1[e], b_lam_k1[e], b_lam_q2[e], b_lam_k2[e]]).astype(F32)
            a_args = (r3(aq), r3(iq), iwt, r3(akd), r3(av), r3(ikd), a_w_uv[e])
            b_args = (r3(bq), r3(bk), r3(bv), lam_vecs, row(b_subln_g[e]), lam_init)
            bound_a = _logit_bound(a_q_g[e], a_k_g[e])
            bound_b = _logit_bound(b_q_g[e], b_k_g[e])
            a_out, b_out = lax.cond(
                jnp.maximum(bound_a[0], bound_b[0]) <= MAX_SHIFT,
                lambda: _dsa_call(bound_a, *a_args, bounded=True, diff=(bound_b,) + b_args),
                lambda: (_bounded_or_online(bound_a, _dsa_call, *a_args),
                         _bounded_or_online(bound_b, _diff_call, *b_args)))
            a_w = A_HEADS * HEAD_DIM
            mix = (a_out.reshape(n, -1), b_out.reshape(n, -1),
                   ev_w_out[e][:a_w].astype(BF16), ev_w_out[e][a_w:].astype(BF16))
            h = _ffn_call(h, *ffn2, mix=mix, ple=ple)
        else:
            o = layer // 2
            h = _odd_call(h, s_len, mix_g[layer], od_w_in[o], od_w_out[o], c_ln_g[o], c_ln_b[o],
                          c_w_s[o], c_b_s[o], d_conv_w[o], d_conv_b[o], d_ln_g[o], d_ln_b[o])
            h = _ffn_call(h, *ffn2, ple=ple)
    return h.reshape(bsz, s_len, d)
```
